```python
import math
import jax, jax.numpy as jnp
from jax import lax
import numpy as np

D_MODEL = 2048
BATCH = 2
SEQ = 4096
DEPTH = 4
DEC_BATCH = 8
DEC_SEQ = 8
PAST_LEN = 16384
PAGE_SIZE = 128

N_A_LAYERS = DEPTH // 2
N_B_LAYERS = DEPTH - N_A_LAYERS
HEAD_A = 64
H_A = D_MODEL // HEAD_A
D_DECAY_LORA = max(32, int(round(1.8 * math.sqrt(D_MODEL) / 32)) * 32)
D_AAA_LORA = max(32, int(round(1.8 * math.sqrt(D_MODEL) / 32)) * 32)
D_MV_LORA = max(32, int(round(1.3 * math.sqrt(D_MODEL) / 32)) * 32)
D_GATE_LORA = max(32, int(round(0.6 * D_MODEL ** 0.8 / 32)) * 32)
LNX_EPS = 64e-5
HEAD_B = 128
H_B = D_MODEL // HEAD_B
Q_BLOCK = 128
FFN_HIDDEN = -(-(8 * D_MODEL) // (3 * 256)) * 256
RMS_EPS = 1e-6

kernel_name = "yoco_rwkv7_fox_decoder_step"


def rms_norm(x, g):
    xf = x.astype(jnp.float32)
    y = xf * lax.rsqrt(jnp.mean(xf * xf, axis=-1, keepdims=True) + RMS_EPS)
    return (y * g).astype(x.dtype)


def swiglu(x, w1, w3, w2):
    return (jax.nn.silu(x @ w1) * (x @ w3)) @ w2


def wkv7_scan(S0, r, decay, k, v, kk, kka):
    def step(S, inp):
        r_t, w_t, k_t, v_t, kk_t, b_t = inp
        sk = jnp.einsum('bhvk,bhk->bhv', S, kk_t)
        S = S * w_t[:, :, None, :] - sk[..., None] * b_t[:, :, None, :] + v_t[..., None] * k_t[:, :, None, :]
        return S, jnp.einsum('bhvk,bhk->bhv', S, r_t)
    xs = tuple(jnp.moveaxis(z, 1, 0) for z in (r, decay, k, v, kk, kka))
    S, out = lax.scan(step, S0.astype(jnp.float32), xs)
    return S, jnp.moveaxis(out, 0, 1)


def rwkv7_time_mix(xn, shift_prev, S0, v_first, p, l):
    B, T, D = xn.shape
    xx = jnp.concatenate([shift_prev[:, None, :].astype(xn.dtype), xn[:, :-1]], axis=1) - xn
    mu = p["tm_mu"][l]
    xr, xw, xk, xv, xa, xg = [xn + xx * mu[i] for i in range(6)]
    r = xr @ p["tm_w_r"][l]
    w_log = -jax.nn.softplus(-(p["tm_w0"][l] + jnp.tanh(xw @ p["tm_w1"][l]) @ p["tm_w2"][l])) - 0.5
    k = xk @ p["tm_w_k"][l]
    v = xv @ p["tm_w_v"][l]
    if l > 0:
        v = v + (v_first - v) * jax.nn.sigmoid(p["tm_v0"][l - 1] + (xv @ p["tm_v1"][l - 1]) @ p["tm_v2"][l - 1])
    a = jax.nn.sigmoid(p["tm_a0"][l] + (xa @ p["tm_a1"][l]) @ p["tm_a2"][l])
    g = jax.nn.sigmoid(xg @ p["tm_g1"][l]) @ p["tm_g2"][l]
    heads = lambda z: z.reshape(B, T, H_A, HEAD_A).astype(jnp.float32)
    kk = heads(k * p["tm_k_k"][l])
    kk = kk / jnp.maximum(jnp.sqrt(jnp.sum(kk * kk, axis=-1, keepdims=True)), 1e-12)
    k = k * (1 + (a - 1) * p["tm_k_a"][l])
    rh, kh, vh, ah = heads(r), heads(k), heads(v), heads(a)
    decay = jnp.exp(-jnp.exp(heads(w_log)))
    S, o = wkv7_scan(S0, rh, decay, kh, vh, kk, kk * ah)
    mean = jnp.mean(o, axis=-1, keepdims=True)
    var = jnp.mean(jnp.square(o - mean), axis=-1, keepdims=True)
    o = ((o - mean) * lax.rsqrt(var + LNX_EPS)).reshape(B, T, D) * p["tm_lnx_w"][l] + p["tm_lnx_b"][l]
    bonus = jnp.sum(rh * kh * p["tm_r_k"][l], axis=-1, keepdims=True) * vh
    o = (o + bonus.reshape(B, T, D)) * g
    out = o.astype(xn.dtype) @ p["tm_w_o"][l]
    return out, xn[:, -1], S, v


def forgetting_attention(q, k, v, c_q, c_k, q_offset):
    B, Tq, H, Dh = q.shape
    Tk = k.shape[1]
    qb = min(Q_BLOCK, Tq)
    n_blk = -(-Tq // qb)
    pad = n_blk * qb - Tq
    q = jnp.pad(q, ((0, 0), (0, pad), (0, 0), (0, 0)))
    c_q = jnp.pad(c_q, ((0, 0), (0, pad), (0, 0)))
    qs = q.reshape(B, n_blk, qb, H, Dh).transpose(1, 0, 2, 3, 4)
    cs = c_q.reshape(B, n_blk, qb, H).transpose(1, 0, 3, 2)
    c_kT = c_k.transpose(0, 2, 1)
    k_pos = jnp.arange(Tk)
    scale = Dh ** -0.5

    def one_block(args):
        blk, q_blk, c_blk = args
        q_pos = q_offset + blk * qb + jnp.arange(qb)
        s = jnp.einsum('bqhd,bkhd->bhqk', q_blk, k).astype(jnp.float32) * scale
        s = s + c_blk[..., :, None] - c_kT[:, :, None, :]
        s = jnp.where(k_pos[None, :] <= q_pos[:, None], s, -jnp.inf)
        pr = jax.nn.softmax(s, axis=-1)
        return jnp.einsum('bhqk,bkhd->bqhd', pr.astype(v.dtype), v)

    out = lax.map(one_block, (jnp.arange(n_blk), qs, cs))
    return out.transpose(1, 0, 2, 3, 4).reshape(B, n_blk * qb, H, Dh)[:, :Tq]


def _trunk(x, shift0, wkv0, past, q_offset, p):
    B, T, D = x.shape
    h = x
    v_first = None
    shifts, states = [], []
    k_new = v_new = logf_new = None
    for l in range(DEPTH):
        if l == N_A_LAYERS:
            s = rms_norm(h, p["kv_norm"])
            k_new = (s @ p["fox_w_k"]).reshape(B, T, H_B, HEAD_B)
            v_new = (s @ p["fox_w_v"]).reshape(B, T, H_B, HEAD_B)
            logf_new = jax.nn.log_sigmoid((s @ p["fox_w_f"] + p["fox_b_f"]).astype(jnp.float32))
            c_new = jnp.cumsum(logf_new, axis=1)
            if past is None:
                k_all, v_all, c_k = k_new, v_new, c_new
            else:
                k_past, v_past, logf_past = past
                c_past = jnp.cumsum(logf_past.astype(jnp.float32), axis=1)
                c_new = c_new + c_past[:, -1:]
                k_all = jnp.concatenate([k_past.astype(k_new.dtype), k_new], axis=1)
                v_all = jnp.concatenate([v_past.astype(v_new.dtype), v_new], axis=1)
                c_k = jnp.concatenate([c_past, c_new], axis=1)
        xn = rms_norm(h, p["ln1"][l])
        if l < N_A_LAYERS:
            out, sh, S, v_l = rwkv7_time_mix(xn, shift0[l], wkv0[l], v_first, p, l)
            if l == 0:
                v_first = v_l
            shifts.append(sh)
            states.append(S)
        else:
            j = l - N_A_LAYERS
            q = (xn @ p["fox_w_q"][j]).reshape(B, T, H_B, HEAD_B)
            att = forgetting_attention(q, k_all, v_all, c_new, c_k, q_offset)
            out = att.reshape(B, T, D) @ p["fox_w_o"][j]
        h = h + out.astype(h.dtype)
        h = h + swiglu(rms_norm(h, p["ln2"][l]), p["ffn_w1"][l], p["ffn_w3"][l], p["ffn_w2"][l]).astype(h.dtype)
    y = rms_norm(h, p["ln_out"])
    return y, jnp.stack(shifts), jnp.stack(states), k_new, v_new, logf_new


def setup_inputs(seed: int = 0) -> dict:
    key = jax.random.key(seed)
    ks = iter(jax.random.split(key, 64))
    nrm = lambda shape, scale: jax.random.normal(next(ks), shape, jnp.float32) * scale
    uni = lambda shape, lo, hi: jax.random.uniform(next(ks), shape, jnp.float32, lo, hi)
    D, NA, NB, F = D_MODEL, N_A_LAYERS, N_B_LAYERS, FFN_HIDDEN
    n_pages = PAST_LEN // PAGE_SIZE
    n_phys = (DEC_BATCH * n_pages * 5) // 4
    page_table = jax.random.permutation(next(ks), n_phys)[: DEC_BATCH * n_pages].reshape(DEC_BATCH, n_pages).astype(jnp.int32)
    return {
        "x_prompt": nrm((BATCH, SEQ, D), 1.0),
        "x_sample": nrm((DEC_BATCH, DEC_SEQ, D), 1.0),
        "state_wkv": nrm((NA, DEC_BATCH, H_A, HEAD_A, HEAD_A), 0.3),
        "state_shift": nrm((NA, DEC_BATCH, D), 1.0),
        "cache_k": nrm((n_phys, PAGE_SIZE, H_B, HEAD_B), 1.0),
        "cache_v": nrm((n_phys, PAGE_SIZE, H_B, HEAD_B), 1.0),
        "cache_logf": jax.nn.log_sigmoid(uni((n_phys, PAGE_SIZE, H_B), 1.0, 4.0) + nrm((n_phys, PAGE_SIZE, H_B), 0.5)),
        "page_table": page_table,
        "ln1": 1.0 + nrm((DEPTH, D), 0.02),
        "ln2": 1.0 + nrm((DEPTH, D), 0.02),
        "ln_out": 1.0 + nrm((D,), 0.02),
        "tm_mu": uni((NA, 6, D), 0.0, 1.0),
        "tm_w_r": nrm((NA, D, D), D ** -0.5),
        "tm_w_k": nrm((NA, D, D), D ** -0.5),
        "tm_w_v": nrm((NA, D, D), D ** -0.5),
        "tm_w_o": nrm((NA, D, D), D ** -0.5),
        "tm_w0": uni((NA, D), -6.5, -1.5),
        "tm_w1": nrm((NA, D, D_DECAY_LORA), D ** -0.5),
        "tm_w2": nrm((NA, D_DECAY_LORA, D), 0.5 * D_DECAY_LORA ** -0.5),
        "tm_a0": nrm((NA, D), 0.1),
        "tm_a1": nrm((NA, D, D_AAA_LORA), D ** -0.5),
        "tm_a2": nrm((NA, D_AAA_LORA, D), 0.5 * D_AAA_LORA ** -0.5),
        "tm_v0": nrm((NA - 1, D), 0.1),
        "tm_v1": nrm((NA - 1, D, D_MV_LORA), D ** -0.5),
        "tm_v2": nrm((NA - 1, D_MV_LORA, D), 0.5 * D_MV_LORA ** -0.5),
        "tm_g1": nrm((NA, D, D_GATE_LORA), D ** -0.5),
        "tm_g2": nrm((NA, D_GATE_LORA, D), D_GATE_LORA ** -0.5),
        "tm_k_k": 0.85 + nrm((NA, D), 0.05),
        "tm_k_a": 1.0 + nrm((NA, D), 0.05),
        "tm_r_k": -0.04 + nrm((NA, H_A, HEAD_A), 0.02),
        "tm_lnx_w": 1.0 + nrm((NA, D), 0.1),
        "tm_lnx_b": nrm((NA, D), 0.02),
        "kv_norm": 1.0 + nrm((D,), 0.02),
        "fox_w_k": nrm((D, D), D ** -0.5),
        "fox_w_v": nrm((D, D), D ** -0.5),
        "fox_w_f": nrm((D, H_B), D ** -0.5),
        "fox_b_f": uni((H_B,), 1.0, 4.0),
        "fox_w_q": nrm((NB, D, D), D ** -0.5),
        "fox_w_o": nrm((NB, D, D), D ** -0.5),
        "ffn_w1": nrm((DEPTH, D, F), D ** -0.5),
        "ffn_w3": nrm((DEPTH, D, F), D ** -0.5),
        "ffn_w2": nrm((DEPTH, F, D), F ** -0.5),
    }


def reference(x_prompt, x_sample, state_wkv, state_shift, cache_k, cache_v, cache_logf, page_table,
              ln1, ln2, ln_out, tm_mu, tm_w_r, tm_w_k, tm_w_v, tm_w_o, tm_w0, tm_w1, tm_w2,
              tm_a0, tm_a1, tm_a2, tm_v0, tm_v1, tm_v2, tm_g1, tm_g2, tm_k_k, tm_k_a, tm_r_k,
              tm_lnx_w, tm_lnx_b, kv_norm, fox_w_k, fox_w_v, fox_w_f, fox_b_f, fox_w_q, fox_w_o,
              ffn_w1, ffn_w3, ffn_w2):
    p = dict(ln1=ln1, ln2=ln2, ln_out=ln_out, tm_mu=tm_mu, tm_w_r=tm_w_r, tm_w_k=tm_w_k,
             tm_w_v=tm_w_v, tm_w_o=tm_w_o, tm_w0=tm_w0, tm_w1=tm_w1, tm_w2=tm_w2, tm_a0=tm_a0,
             tm_a1=tm_a1, tm_a2=tm_a2, tm_v0=tm_v0, tm_v1=tm_v1, tm_v2=tm_v2, tm_g1=tm_g1,
             tm_g2=tm_g2, tm_k_k=tm_k_k, tm_k_a=tm_k_a, tm_r_k=tm_r_k, tm_lnx_w=tm_lnx_w,
             tm_lnx_b=tm_lnx_b, kv_norm=kv_norm, fox_w_k=fox_w_k, fox_w_v=fox_w_v,
             fox_w_f=fox_w_f, fox_b_f=fox_b_f, fox_w_q=fox_w_q, fox_w_o=fox_w_o,
             ffn_w1=ffn_w1, ffn_w3=ffn_w3, ffn_w2=ffn_w2)
    B = x_prompt.shape[0]
    shift0 = jnp.zeros((N_A_LAYERS, B, D_MODEL), x_prompt.dtype)
    wkv0 = jnp.zeros((N_A_LAYERS, B, H_A, HEAD_A, HEAD_A), jnp.float32)
    y_p, sh_p, wkv_p, k_p, v_p, lf_p = _trunk(x_prompt, shift0, wkv0, None, 0, p)
    DB, n_pages = page_table.shape
    past_len = n_pages * cache_k.shape[1]
    k_past = cache_k[page_table].reshape(DB, past_len, H_B, HEAD_B)
    v_past = cache_v[page_table].reshape(DB, past_len, H_B, HEAD_B)
    lf_past = cache_logf[page_table].reshape(DB, past_len, H_B)
    y_s, sh_s, wkv_s, k_s, v_s, lf_s = _trunk(x_sample, state_shift, state_wkv, (k_past, v_past, lf_past), past_len, p)
    return (y_p, y_s, wkv_p, sh_p, k_p, v_p, lf_p, wkv_s, sh_s, k_s, v_s, lf_s)
```

```python
import functools
import math

import jax
import jax.numpy as jnp
from jax import lax
from jax.experimental import pallas as pl
from jax.experimental.pallas import tpu as pltpu

F32 = jnp.float32
BF16 = jnp.bfloat16

HEAD_A = 64
HEAD_B = 128
LANES = 128
RMS_EPS = 1e-6
LNX_EPS = 64e-5
NEG_BIG = -1e30
VMEM_LIMIT = 56 * 1024 * 1024


def _cparams(*sem):
    return pltpu.CompilerParams(dimension_semantics=sem, vmem_limit_bytes=VMEM_LIMIT)


def _dot(a, b):
    return jnp.dot(a, b, preferred_element_type=F32)


def _dot_nt(a, b):
    return lax.dot_general(a, b, (((1,), (1,)), ((), ())), preferred_element_type=F32)


def _split3(x):
    hi = x.astype(BF16)
    r1 = x - hi.astype(F32)
    mid = r1.astype(BF16)
    lo = (r1 - mid.astype(F32)).astype(BF16)
    return hi, mid, lo


def _dot3_l(x, w):
    hi, mid, lo = _split3(x)
    return _dot(hi, w) + _dot(mid, w) + _dot(lo, w)


def _dot3_r(w, x):
    hi, mid, lo = _split3(x)
    return _dot(w, hi) + _dot(w, mid) + _dot(w, lo)


def _rms(x, g):
    return x * lax.rsqrt(jnp.mean(x * x, axis=-1, keepdims=True) + RMS_EPS) * g


def _softplus(z):
    return jnp.maximum(z, 0.0) + jnp.log1p(jnp.exp(-jnp.abs(z)))


def _row_tile(m):
    return min(512, m)


def _rmsnorm_kernel(x_ref, g_ref, o_ref):
    o_ref[...] = _rms(x_ref[...], g_ref[...])


def _rmsnorm(x, g):
    m, d = x.shape
    tm = _row_tile(m)
    return pl.pallas_call(
        _rmsnorm_kernel,
        grid=(m // tm,),
        in_specs=[pl.BlockSpec((tm, d), lambda i: (i, 0)), pl.BlockSpec((1, d), lambda i: (0, 0))],
        out_specs=pl.BlockSpec((tm, d), lambda i: (i, 0)),
        out_shape=jax.ShapeDtypeStruct((m, d), F32),
        compiler_params=_cparams("parallel"),
        name="rmsnorm",
    )(x, g.reshape(1, d))


def _norm_matmul_kernel(x_ref, g_ref, w_ref, b_ref, *refs, epilogue, scale):
    o_refs, xn_scr = refs[:-1], refs[-1]

    @pl.when(pl.program_id(1) == 0)
    def _():
        xn_scr[...] = _rms(x_ref[...], g_ref[...]).astype(BF16)

    acc = _dot(xn_scr[...], w_ref[...])
    if epilogue == "log_sigmoid":
        acc = -_softplus(-(acc + b_ref[...]))
    elif epilogue == "scale":
        acc = acc * scale
    for o_ref in o_refs:
        o_ref[...] = acc.astype(o_ref.dtype)


def _norm_matmul(x, g, w, out_dtypes, epilogue="none", bias=None, scale=1.0):
    m, d = x.shape
    n = w.shape[1]
    tm, tn = _row_tile(m), min(512, n)
    if bias is None:
        bias = jnp.zeros((1, n), F32)
    outs = pl.pallas_call(
        functools.partial(_norm_matmul_kernel, epilogue=epilogue, scale=scale),
        grid=(m // tm, n // tn),
        in_specs=[pl.BlockSpec((tm, d), lambda i, j: (i, 0)),
                  pl.BlockSpec((1, d), lambda i, j: (0, 0)),
                  pl.BlockSpec((d, tn), lambda i, j: (0, j)),
                  pl.BlockSpec((1, tn), lambda i, j: (0, j))],
        out_specs=[pl.BlockSpec((tm, tn), lambda i, j: (i, j)) for _ in out_dtypes],
        out_shape=[jax.ShapeDtypeStruct((m, n), dt) for dt in out_dtypes],
        scratch_shapes=[pltpu.VMEM((tm, d), BF16)],
        compiler_params=_cparams("parallel", "arbitrary"),
        name="norm_matmul_" + epilogue,
    )(x, g.reshape(1, d), w, bias)
    return outs


def _matmul_residual_kernel(x_ref, w_ref, h_ref, o_ref):
    o_ref[...] = h_ref[...] + _dot(x_ref[...], w_ref[...])


def _matmul_residual(x, w, h):
    m, k = x.shape
    n = w.shape[1]
    tm, tn = _row_tile(m), min(512, n)
    return pl.pallas_call(
        _matmul_residual_kernel,
        grid=(m // tm, n // tn),
        in_specs=[pl.BlockSpec((tm, k), lambda i, j: (i, 0)),
                  pl.BlockSpec((k, tn), lambda i, j: (0, j)),
                  pl.BlockSpec((tm, tn), lambda i, j: (i, j))],
        out_specs=pl.BlockSpec((tm, tn), lambda i, j: (i, j)),
        out_shape=jax.ShapeDtypeStruct((m, n), F32),
        compiler_params=_cparams("parallel", "parallel"),
        name="matmul_residual",
    )(x, w, h)


def _ffn_kernel(h_ref, g_ref, w1_ref, w3_ref, w2_ref, o_ref, xn_scr, acc_scr):
    j = pl.program_id(1)

    @pl.when(j == 0)
    def _():
        xn_scr[...] = _rms(h_ref[...], g_ref[...]).astype(BF16)
        acc_scr[...] = jnp.zeros_like(acc_scr)

    x = xn_scr[...]
    a = _dot(x, w1_ref[...])
    b = _dot(x, w3_ref[...])
    hm = (a * jax.nn.sigmoid(a) * b).astype(BF16)
    acc_scr[...] += _dot(hm, w2_ref[...])

    @pl.when(j == pl.num_programs(1) - 1)
    def _():
        o_ref[...] = h_ref[...] + acc_scr[...]


def _ffn(h, g, w1, w3, w2):
    m, d = h.shape
    f = w1.shape[1]
    tm, tf = _row_tile(m), 512
    return pl.pallas_call(
        _ffn_kernel,
        grid=(m // tm, f // tf),
        in_specs=[pl.BlockSpec((tm, d), lambda i, j: (i, 0)),
                  pl.BlockSpec((1, d), lambda i, j: (0, 0)),
                  pl.BlockSpec((d, tf), lambda i, j: (0, j)),
                  pl.BlockSpec((d, tf), lambda i, j: (0, j)),
                  pl.BlockSpec((tf, d), lambda i, j: (j, 0))],
        out_specs=pl.BlockSpec((tm, d), lambda i, j: (i, 0)),
        out_shape=jax.ShapeDtypeStruct((m, d), F32),
        scratch_shapes=[pltpu.VMEM((tm, d), BF16), pltpu.VMEM((tm, d), F32)],
        compiler_params=_cparams("parallel", "arbitrary"),
        name="ffn",
    )(h, g.reshape(1, d), w1, w3, w2)


def _mix_kernel(xn_ref, xp_ref, vf_ref, mu_ref, wr_ref, wk_ref, wv_ref,
                w1_ref, w2_ref, a1_ref, a2_ref, v1_ref, v2_ref, g1_ref, g2_ref,
                w0_ref, a0_ref, v0_ref,
                r_ref, lw_ref, k_ref, v_ref, a_ref, g_ref,
                xr_scr, xk_scr, xv_scr, hw_scr, ha_scr, hv_scr, hg_scr, *, has_v):
    @pl.when(pl.program_id(1) == 0)
    def _():
        xn = xn_ref[...]
        xx = xp_ref[...] - xn
        mix = lambda i: (xn + xx * mu_ref[i:i + 1, :]).astype(BF16)
        xr_scr[...] = mix(0)
        hw_scr[...] = jnp.tanh(_dot(mix(1), w1_ref[...])).astype(BF16)
        xk_scr[...] = mix(2)
        xv = mix(3)
        xv_scr[...] = xv
        if has_v:
            hv_scr[...] = _dot(xv, v1_ref[...]).astype(BF16)
        ha_scr[...] = _dot(mix(4), a1_ref[...]).astype(BF16)
        hg_scr[...] = jax.nn.sigmoid(_dot(mix(5), g1_ref[...])).astype(BF16)

    r_ref[...] = _dot(xr_scr[...], wr_ref[...])
    k_ref[...] = _dot(xk_scr[...], wk_ref[...])
    v = _dot(xv_scr[...], wv_ref[...])
    if has_v:
        v = v + (vf_ref[...] - v) * jax.nn.sigmoid(v0_ref[...] + _dot(hv_scr[...], v2_ref[...]))
    v_ref[...] = v
    w_log = -_softplus(-(w0_ref[...] + _dot(hw_scr[...], w2_ref[...]))) - 0.5
    lw_ref[...] = -jnp.exp(w_log)
    a_ref[...] = jax.nn.sigmoid(a0_ref[...] + _dot(ha_scr[...], a2_ref[...]))
    g_ref[...] = _dot(hg_scr[...], g2_ref[...])


def _pad_cols(w, n):
    return jnp.pad(w, ((0, 0), (0, n - w.shape[1])))


def _pad_rows(w, n):
    return jnp.pad(w, ((0, n - w.shape[0]), (0, 0)))


def _mix(xn, xprev, vfirst, mu, wr, wk, wv, w1, w2, a1, a2, v1, v2, g1, g2, w0, a0, v0, has_v):
    m, d = xn.shape
    tm, tn = _row_tile(m), 256
    lw_, la_, lv_, lg_ = w1.shape[1], a1.shape[1], v1.shape[1], g1.shape[1]
    row = lambda i, j: (i, 0)
    col = lambda i, j: (0, j)
    tile = lambda i, j: (i, j)
    fixed = lambda i, j: (0, 0)
    outs = pl.pallas_call(
        functools.partial(_mix_kernel, has_v=has_v),
        grid=(m // tm, d // tn),
        in_specs=[pl.BlockSpec((tm, d), row), pl.BlockSpec((tm, d), row), pl.BlockSpec((tm, tn), tile),
                  pl.BlockSpec((8, d), fixed),
                  pl.BlockSpec((d, tn), col), pl.BlockSpec((d, tn), col), pl.BlockSpec((d, tn), col),
                  pl.BlockSpec((d, lw_), fixed), pl.BlockSpec((lw_, tn), col),
                  pl.BlockSpec((d, la_), fixed), pl.BlockSpec((la_, tn), col),
                  pl.BlockSpec((d, lv_), fixed), pl.BlockSpec((lv_, tn), col),
                  pl.BlockSpec((d, lg_), fixed), pl.BlockSpec((lg_, tn), col),
                  pl.BlockSpec((1, tn), col), pl.BlockSpec((1, tn), col), pl.BlockSpec((1, tn), col)],
        out_specs=[pl.BlockSpec((tm, tn), tile) for _ in range(6)],
        out_shape=[jax.ShapeDtypeStruct((m, d), F32) for _ in range(6)],
        scratch_shapes=[pltpu.VMEM((tm, d), BF16), pltpu.VMEM((tm, d), BF16), pltpu.VMEM((tm, d), BF16),
                        pltpu.VMEM((tm, lw_), BF16), pltpu.VMEM((tm, la_), BF16),
                        pltpu.VMEM((tm, lv_), BF16), pltpu.VMEM((tm, lg_), BF16)],
        compiler_params=_cparams("parallel", "arbitrary"),
        name="rwkv_mix_proj",
    )(xn, xprev, vfirst, mu, wr, wk, wv, w1, w2, a1, a2, v1, v2, g1, g2, w0, a0, v0)
    return outs


SCAN_CHUNK = 64
SCAN_PAIRS = 4


def _scan_kernel(r_ref, lw_ref, kr_ref, v_ref, a_ref, g_ref,
                 kk_ref, ka_ref, rk_ref, lnw_ref, lnb_ref, s0_ref,
                 y_ref, sout_ref, st_scr):
    c = pl.program_id(2)
    C = SCAN_CHUNK
    R = 2 * C

    @pl.when(c == 0)
    def _():
        st_scr[...] = s0_ref[0]

    lane_c = lax.broadcasted_iota(jnp.int32, (C, LANES), 1)
    head0 = lane_c < HEAD_A
    row = lax.broadcasted_iota(jnp.int32, (R, R), 0)
    col = lax.broadcasted_iota(jnp.int32, (R, R), 1)
    same = (row // C) == (col // C)
    strict = same & ((col % C) < (row % C))
    incl = same & ((col % C) <= (row % C))
    ones_bd = ((row // HEAD_A) == (col // HEAD_A)).astype(BF16)
    ti = lax.broadcasted_iota(jnp.int32, (C, C), 0)
    tj = lax.broadcasted_iota(jnp.int32, (C, C), 1)
    tri_incl = (tj <= ti).astype(BF16)
    zeros_c = jnp.zeros((C, LANES), F32)

    def stack(x):
        return jnp.concatenate([jnp.where(head0, x, 0.0), jnp.where(head0, 0.0, x)], axis=0)

    for p in range(SCAN_PAIRS):
        sl = slice(p * LANES, (p + 1) * LANES)
        r = r_ref[0, :, sl]
        lw = lw_ref[0, :, sl]
        kr = kr_ref[0, :, sl]
        v = v_ref[0, :, sl]
        a = a_ref[0, :, sl]
        g = g_ref[0, :, sl]

        kkr = kr * kk_ref[:, sl]
        ss = _dot3_l(kkr * kkr, ones_bd)
        kk = kkr / jnp.maximum(jnp.sqrt(ss), 1e-12)
        k = kr * (1.0 + (a - 1.0) * ka_ref[:, sl])
        b = kk * a

        cum = _dot3_r(tri_incl, lw)
        cum_end = cum[C - 1:C, :]
        g_in = jnp.exp(cum)
        g_ex = jnp.exp(cum - lw)
        g_inv = jnp.exp(-cum)
        g_tail = jnp.exp(cum_end - cum)

        at = stack(-kk * g_ex)
        rt = stack(r * g_in).astype(BF16)
        bt = stack(b * g_inv).astype(BF16)
        kt = stack(k * g_inv).astype(BF16)
        bh = stack(b * g_tail)
        kh = stack(k * g_tail)
        vs = stack(v).astype(BF16)
        at_b = at.astype(BF16)

        sc = _dot_nt(jnp.concatenate([at_b, rt], axis=0), jnp.concatenate([bt, kt], axis=0))
        a_ab = jnp.where(strict, sc[:R, :R], 0.0)
        a_ak = jnp.where(strict, sc[:R, R:], 0.0)
        a_rb = jnp.where(incl, sc[R:, :R], 0.0)
        a_rk = jnp.where(incl, sc[R:, R:], 0.0)

        x = jnp.concatenate([at, _dot(a_ak.astype(BF16), vs)], axis=1)
        pw = a_ab
        n_sq = int(math.log2(C))
        for it in range(n_sq):
            pw_b = pw.astype(BF16)
            x = x + _dot(pw_b, x.astype(BF16))
            if it + 1 < n_sq:
                pw = _dot(pw_b, pw_b)
        a_hat = x[:, :LANES].astype(BF16)
        u_bar = x[:, LANES:]

        st = st_scr[p]
        st_b = st.astype(BF16)
        u = _dot(a_hat, st_b) + u_bar
        u_b = u.astype(BF16)
        o_st = _dot(rt, st_b) + _dot(a_rb.astype(BF16), u_b) + _dot(a_rk.astype(BF16), vs)
        o = o_st[:C] + o_st[C:]

        lw_t = jnp.concatenate([lw, zeros_c], axis=0).T if C < LANES else lw.T
        decay_col = jnp.exp(jnp.sum(lw_t, axis=1, keepdims=True))
        st_scr[p] = decay_col * st + _dot(bh.T.astype(BF16), u_b) + _dot(kh.T.astype(BF16), vs)

        mean = _dot3_l(o, ones_bd) * (1.0 / HEAD_A)
        dlt = o - mean
        var = _dot3_l(dlt * dlt, ones_bd) * (1.0 / HEAD_A)
        on = dlt * lax.rsqrt(var + LNX_EPS) * lnw_ref[:, sl] + lnb_ref[:, sl]
        bonus = _dot3_l(r * k * rk_ref[:, sl], ones_bd) * v
        y_ref[0, :, sl] = ((on + bonus) * g).astype(y_ref.dtype)

    @pl.when(c == pl.num_programs(2) - 1)
    def _():
        sout_ref[0] = st_scr[...]


def _scan(r, lw, kr, v, a, g, k_k, k_a, r_k, lnx_w, lnx_b, s0_bd):
    bsz, t, d = r.shape
    C, P = SCAN_CHUNK, SCAN_PAIRS
    w = P * LANES
    n_pg = d // w
    seq = pl.BlockSpec((1, C, w), lambda b, q, c: (b, c, q))
    vec = pl.BlockSpec((1, w), lambda b, q, c: (0, q))
    sbd = pl.BlockSpec((1, P, LANES, LANES), lambda b, q, c: (b, q, 0, 0))
    y, s_out = pl.pallas_call(
        _scan_kernel,
        grid=(bsz, n_pg, t // C),
        in_specs=[seq] * 6 + [vec] * 5 + [sbd],
        out_specs=[seq, sbd],
        out_shape=[jax.ShapeDtypeStruct((bsz, t, d), BF16),
                   jax.ShapeDtypeStruct(s0_bd.shape, F32)],
        scratch_shapes=[pltpu.VMEM((P, LANES, LANES), F32)],
        compiler_params=_cparams("parallel", "parallel", "arbitrary"),
        name="wkv7_scan",
    )(r, lw, kr, v, a, g, k_k.reshape(1, d), k_a.reshape(1, d), r_k.reshape(1, d),
      lnx_w.reshape(1, d), lnx_b.reshape(1, d), s0_bd)
    return y, s_out


def _state_to_bd(s):
    bsz, h = s.shape[:2]
    st = jnp.swapaxes(s, 2, 3).reshape(bsz, h // 2, 2, HEAD_A, HEAD_A)
    eye = jnp.eye(2, dtype=s.dtype)
    bd = st[:, :, :, :, None, :] * eye[None, None, :, None, :, None]
    return bd.reshape(bsz, h // 2, 2 * HEAD_A, 2 * HEAD_A)


def _bd_to_state(bd):
    bsz, hp = bd.shape[:2]
    x = bd.reshape(bsz, hp, 2, HEAD_A, 2, HEAD_A)
    st = jnp.stack([x[:, :, 0, :, 0, :], x[:, :, 1, :, 1, :]], axis=2)
    return jnp.swapaxes(st.reshape(bsz, hp * 2, HEAD_A, HEAD_A), 2, 3)


def _cumsum_kernel(x_ref, o_ref, carry_scr):
    @pl.when(pl.program_id(1) == 0)
    def _():
        carry_scr[...] = jnp.zeros_like(carry_scr)

    tt = x_ref.shape[2]
    u = lax.broadcasted_iota(jnp.int32, (tt, tt), 0)
    s = lax.broadcasted_iota(jnp.int32, (tt, tt), 1)
    tri = (u <= s).astype(BF16)
    c = _dot3_l(x_ref[0], tri) + carry_scr[...]
    o_ref[0] = c
    carry_scr[...] = c[:, tt - 1:tt]


def _cumsum_rows(x):
    bsz, h, t = x.shape
    tt = min(512, t)
    return pl.pallas_call(
        _cumsum_kernel,
        grid=(bsz, t // tt),
        in_specs=[pl.BlockSpec((1, h, tt), lambda b, i: (b, 0, i))],
        out_specs=pl.BlockSpec((1, h, tt), lambda b, i: (b, 0, i)),
        out_shape=jax.ShapeDtypeStruct((bsz, h, t), F32),
        scratch_shapes=[pltpu.VMEM((h, 1), F32)],
        compiler_params=_cparams("parallel", "arbitrary"),
        name="logf_cumsum",
    )(x)


def _flash_kernel(q_ref, k_ref, v_ref, cq_ref, ck_ref, o_ref, m_scr, l_scr, acc_scr):
    i, j = pl.program_id(1), pl.program_id(2)
    tq, tk = q_ref.shape[1], k_ref.shape[1]
    n_heads = q_ref.shape[2] // HEAD_B

    @pl.when(j == 0)
    def _():
        m_scr[...] = jnp.full_like(m_scr, NEG_BIG)
        l_scr[...] = jnp.zeros_like(l_scr)
        acc_scr[...] = jnp.zeros_like(acc_scr)

    def step(masked):
        if masked:
            qpos = lax.broadcasted_iota(jnp.int32, (tq, tk), 0)
            kpos = lax.broadcasted_iota(jnp.int32, (tq, tk), 1)
            keep = kpos <= qpos
        for h in range(n_heads):
            hs = slice(h * HEAD_B, (h + 1) * HEAD_B)
            s = _dot_nt(q_ref[0, :, hs], k_ref[0, :, hs])
            s = s + (cq_ref[0, :, h:h + 1] - ck_ref[0, h:h + 1, :])
            if masked:
                s = jnp.where(keep, s, NEG_BIG)
            m_old = m_scr[h]
            m_new = jnp.maximum(m_old, jnp.max(s, axis=1, keepdims=True))
            alpha = jnp.exp(m_old - m_new)
            p = jnp.exp(s - m_new)
            l_scr[h] = alpha * l_scr[h] + jnp.sum(p, axis=1, keepdims=True)
            acc_scr[:, hs] = alpha * acc_scr[:, hs] + _dot(p.astype(BF16), v_ref[0, :, hs])
            m_scr[h] = m_new

    @pl.when(j < i)
    def _():
        step(False)

    @pl.when(j == i)
    def _():
        step(True)
        for h in range(n_heads):
            hs = slice(h * HEAD_B, (h + 1) * HEAD_B)
            o_ref[0, :, hs] = (acc_scr[:, hs] / l_scr[h]).astype(o_ref.dtype)


def _flash_attention(q, k, v, c_col, c_row):
    bsz, t, d = q.shape
    h = d // HEAD_B
    tq = min(512, t)
    nq = t // tq
    kv_idx = lambda b, i, j: (b, jnp.minimum(j, i), 0)
    return pl.pallas_call(
        _flash_kernel,
        grid=(bsz, nq, nq),
        in_specs=[pl.BlockSpec((1, tq, d), lambda b, i, j: (b, i, 0)),
                  pl.BlockSpec((1, tq, d), kv_idx),
                  pl.BlockSpec((1, tq, d), kv_idx),
                  pl.BlockSpec((1, tq, h), lambda b, i, j: (b, i, 0)),
                  pl.BlockSpec((1, h, tq), lambda b, i, j: (b, 0, jnp.minimum(j, i)))],
        out_specs=pl.BlockSpec((1, tq, d), lambda b, i, j: (b, i, 0)),
        out_shape=jax.ShapeDtypeStruct((bsz, t, d), BF16),
        scratch_shapes=[pltpu.VMEM((h, tq, 1), F32), pltpu.VMEM((h, tq, 1), F32),
                        pltpu.VMEM((tq, d), F32)],
        compiler_params=_cparams("parallel", "parallel", "arbitrary"),
        name="fox_prompt_attention",
    )(q, k, v, c_col, c_row)


def _paged_kernel(pt_ref, qbd_ref, kn_ref, vn_ref, gn_ref, kp_ref, vp_ref, lf_ref, o_ref,
                  m_scr, l_scr, acc_scr, cq_scr, carry_scr):
    del pt_ref
    p = pl.program_id(1)
    rows = qbd_ref.shape[1]
    nq = rows // (qbd_ref.shape[2] // HEAD_B)
    n_heads = rows // nq
    row = lax.broadcasted_iota(jnp.int32, (rows, LANES), 0)
    lane = lax.broadcasted_iota(jnp.int32, (rows, LANES), 1)
    u = lax.broadcasted_iota(jnp.int32, (LANES, LANES), 0)
    s_ = lax.broadcasted_iota(jnp.int32, (LANES, LANES), 1)

    def update(s, k_unused, v_blk):
        m_old = m_scr[...]
        m_new = jnp.maximum(m_old, jnp.max(s, axis=1, keepdims=True))
        alpha = jnp.exp(m_old - m_new)
        pr = jnp.exp(s - m_new)
        l_scr[...] = alpha * l_scr[...] + jnp.sum(pr, axis=1, keepdims=True)
        pr_b = pr.astype(BF16)
        pv = [_dot(pr_b[h * nq:(h + 1) * nq, :], v_blk[:, h * HEAD_B:(h + 1) * HEAD_B]) for h in range(n_heads)]
        acc_scr[...] = alpha * acc_scr[...] + jnp.concatenate(pv, axis=0)
        m_scr[...] = m_new

    @pl.when(p == 0)
    def _():
        m_scr[...] = jnp.full_like(m_scr, NEG_BIG)
        l_scr[...] = jnp.zeros_like(l_scr)
        acc_scr[...] = jnp.zeros_like(acc_scr)
        carry_scr[...] = jnp.zeros_like(carry_scr)
        gn = gn_ref[0]
        qi = row % nq
        c_new_col = jnp.sum(jnp.where(lane <= qi, gn, 0.0), axis=1, keepdims=True)
        cq_scr[...] = c_new_col
        c_new_row = _dot3_l(gn, (u <= s_).astype(BF16))
        s = _dot_nt(qbd_ref[0], kn_ref[0]) + (c_new_col - c_new_row)
        s = jnp.where(lane <= qi, s, NEG_BIG)
        update(s, None, vn_ref[0])

    @pl.when(p > 0)
    def _():
        lf = lf_ref[0]
        suffix = _dot3_l(lf, (u > s_).astype(BF16)) + carry_scr[...]
        carry_scr[...] += jnp.sum(lf, axis=1, keepdims=True)
        s = _dot_nt(qbd_ref[0], kp_ref[0].astype(BF16)) + (cq_scr[...] + suffix)
        update(s, None, vp_ref[0].astype(BF16))

    @pl.when(p == pl.num_programs(1) - 1)
    def _():
        o_ref[0] = acc_scr[...] / l_scr[...]


def _paged_attention(page_table, qbd, k_new, v_new, g_new, cache_k, cache_v, lf_t):
    dbs, rows, d = qbd.shape
    n_pages = page_table.shape[1]
    page = cache_k.shape[1]
    per_b = lambda b, p, pt: (b, 0, 0)
    paged = lambda b, p, pt: (pt[b, n_pages - jnp.maximum(p, 1)], 0, 0)
    grid_spec = pltpu.PrefetchScalarGridSpec(
        num_scalar_prefetch=1,
        grid=(dbs, n_pages + 1),
        in_specs=[pl.BlockSpec((1, rows, d), per_b),
                  pl.BlockSpec((1, LANES, d), per_b),
                  pl.BlockSpec((1, LANES, d), per_b),
                  pl.BlockSpec((1, rows, LANES), per_b),
                  pl.BlockSpec((1, page, d), paged),
                  pl.BlockSpec((1, page, d), paged),
                  pl.BlockSpec((1, rows, page), paged)],
        out_specs=pl.BlockSpec((1, rows, HEAD_B), per_b),
        scratch_shapes=[pltpu.VMEM((rows, 1), F32), pltpu.VMEM((rows, 1), F32),
                        pltpu.VMEM((rows, HEAD_B), F32), pltpu.VMEM((rows, 1), F32),
                        pltpu.VMEM((rows, 1), F32)],
    )
    return pl.pallas_call(
        _paged_kernel,
        grid_spec=grid_spec,
        out_shape=jax.ShapeDtypeStruct((dbs, rows, HEAD_B), F32),
        compiler_params=_cparams("parallel", "arbitrary"),
        name="fox_paged_attention",
    )(page_table, qbd, k_new, v_new, g_new, cache_k, cache_v, lf_t)


def _trunk(x, shift0, wkv0, past, p):
    bsz, t, d = x.shape
    m = bsz * t
    n_a = p["tm_w_r"].shape[0]
    depth = p["ln1"].shape[0]
    h_a = d // HEAD_A
    h_b = d // HEAD_B
    h = x.reshape(m, d)
    v_first = None
    shifts, states = [], []
    t_scan = -(-t // SCAN_CHUNK) * SCAN_CHUNK
    for l in range(depth):
        if l == n_a:
            k_new, k_bf = _norm_matmul(h, p["kv_norm"], p["fox_w_k"], (F32, BF16))
            v_new, v_bf = _norm_matmul(h, p["kv_norm"], p["fox_w_v"], (F32, BF16))
            (lf_pad,) = _norm_matmul(h, p["kv_norm"], p["fox_w_f"], (F32,), epilogue="log_sigmoid",
                                     bias=p["fox_b_f"])
            logf_new = lf_pad[:, :h_b].reshape(bsz, t, h_b)
            if past is None:
                c_row = _cumsum_rows(jnp.swapaxes(logf_new, 1, 2))
                c_col = jnp.swapaxes(c_row, 1, 2)
            else:
                pad_t = LANES - t
                k_pad = jnp.pad(k_bf.reshape(bsz, t, d), ((0, 0), (0, pad_t), (0, 0)))
                v_pad = jnp.pad(v_bf.reshape(bsz, t, d), ((0, 0), (0, pad_t), (0, 0)))
                g_new = jnp.repeat(jnp.swapaxes(logf_new, 1, 2), t, axis=1)
                g_new = jnp.pad(g_new, ((0, 0), (0, 0), (0, pad_t)))
        if l < n_a:
            xn = _rmsnorm(h, p["ln1"][l])
            xn3 = xn.reshape(bsz, t, d)
            shifts.append(xn3[:, -1])
            xprev = jnp.concatenate([shift0[l][:, None, :], xn3[:, :-1]], axis=1).reshape(m, d)
            has_v = l > 0
            lv = max(l - 1, 0)
            r, lw, kr, v, a, g = _mix(
                xn, xprev, v_first if has_v else xn, p["mu8"][l],
                p["tm_w_r"][l], p["tm_w_k"][l], p["tm_w_v"][l],
                p["tm_w1"][l], p["tm_w2"][l], p["tm_a1"][l], p["tm_a2"][l],
                p["tm_v1"][lv], p["tm_v2"][lv], p["tm_g1"][l], p["tm_g2"][l],
                p["tm_w0"][l].reshape(1, d), p["tm_a0"][l].reshape(1, d), p["tm_v0"][lv].reshape(1, d),
                has_v)
            if l == 0:
                v_first = v
            seq = lambda z: jnp.pad(z.reshape(bsz, t, d), ((0, 0), (0, t_scan - t), (0, 0)))
            y, s_bd = _scan(seq(r), seq(lw), seq(kr), seq(v), seq(a), seq(g),
                            p["tm_k_k"][l], p["tm_k_a"][l], p["tm_r_k"][l].reshape(d),
                            p["tm_lnx_w"][l], p["tm_lnx_b"][l], _state_to_bd(wkv0[l]))
            states.append(_bd_to_state(s_bd))
            h = _matmul_residual(y[:, :t].reshape(m, d), p["tm_w_o"][l], h)
        else:
            j = l - n_a
            (q,) = _norm_matmul(h, p["ln1"][l], p["fox_w_q"][j], (BF16,), epilogue="scale",
                                scale=HEAD_B ** -0.5)
            if past is None:
                att = _flash_attention(q.reshape(bsz, t, d), k_bf.reshape(bsz, t, d),
                                       v_bf.reshape(bsz, t, d), c_col, c_row).reshape(m, d)
            else:
                cache_k, cache_v, lf_t, page_table = past
                q4 = jnp.swapaxes(q.reshape(bsz, t, h_b, HEAD_B), 1, 2)
                eye = jnp.eye(h_b, dtype=q.dtype)
                qbd = (q4[:, :, :, None, :] * eye[None, :, None, :, None]).reshape(bsz, h_b * t, d)
                o = _paged_attention(page_table, qbd, k_pad, v_pad, g_new, cache_k, cache_v, lf_t)
                att = jnp.swapaxes(o.reshape(bsz, h_b, t, HEAD_B), 1, 2).reshape(m, d).astype(BF16)
            h = _matmul_residual(att, p["fox_w_o"][j], h)
        h = _ffn(h, p["ln2"][l], p["ffn_w1"][l], p["ffn_w3"][l], p["ffn_w2"][l])
    y = _rmsnorm(h, p["ln_out"])
    h_sh = (bsz, t, h_b, HEAD_B)
    return (y.reshape(bsz, t, d), jnp.stack(shifts), jnp.stack(states),
            k_new.reshape(h_sh), v_new.reshape(h_sh), logf_new)


def kernel(x_prompt, x_sample, state_wkv, state_shift, cache_k, cache_v, cache_logf, page_table, ln1, ln2, ln_out, tm_mu, tm_w_r, tm_w_k, tm_w_v, tm_w_o, tm_w0, tm_w1, tm_w2, tm_a0, tm_a1, tm_a2, tm_v0, tm_v1, tm_v2, tm_g1, tm_g2, tm_k_k, tm_k_a, tm_r_k, tm_lnx_w, tm_lnx_b, kv_norm, fox_w_k, fox_w_v, fox_w_f, fox_b_f, fox_w_q, fox_w_o, ffn_w1, ffn_w3, ffn_w2):
    bf = lambda w: w.astype(BF16)
    n_a, d = tm_w0.shape
    h_b = d // HEAD_B
    lora = lambda w1, w2: (bf(jnp.pad(w1, ((0, 0), (0, 0), (0, -w1.shape[2] % LANES)))),
                           bf(jnp.pad(w2, ((0, 0), (0, -w2.shape[1] % LANES), (0, 0)))))
    w1, w2 = lora(tm_w1, tm_w2)
    a1, a2 = lora(tm_a1, tm_a2)
    if tm_v1.shape[0] == 0:
        tm_v0 = jnp.zeros((1, d), F32)
        tm_v1 = jnp.zeros((1, d, LANES), F32)
        tm_v2 = jnp.zeros((1, LANES, d), F32)
    v1, v2 = lora(tm_v1, tm_v2)
    g1, g2 = lora(tm_g1, tm_g2)
    p = dict(ln1=ln1, ln2=ln2, ln_out=ln_out,
             mu8=jnp.pad(tm_mu, ((0, 0), (0, 2), (0, 0))),
             tm_w_r=bf(tm_w_r), tm_w_k=bf(tm_w_k), tm_w_v=bf(tm_w_v), tm_w_o=bf(tm_w_o),
             tm_w0=tm_w0, tm_w1=w1, tm_w2=w2, tm_a0=tm_a0, tm_a1=a1, tm_a2=a2,
             tm_v0=tm_v0, tm_v1=v1, tm_v2=v2, tm_g1=g1, tm_g2=g2,
             tm_k_k=tm_k_k, tm_k_a=tm_k_a, tm_r_k=tm_r_k, tm_lnx_w=tm_lnx_w, tm_lnx_b=tm_lnx_b,
             kv_norm=kv_norm, fox_w_k=bf(fox_w_k), fox_w_v=bf(fox_w_v),
             fox_w_f=bf(jnp.pad(fox_w_f, ((0, 0), (0, LANES - h_b)))),
             fox_b_f=jnp.pad(fox_b_f, (0, LANES - h_b)).reshape(1, LANES),
             fox_w_q=bf(fox_w_q), fox_w_o=bf(fox_w_o),
             ffn_w1=bf(ffn_w1), ffn_w3=bf(ffn_w3), ffn_w2=bf(ffn_w2))

    bsz = x_prompt.shape[0]
    h_a = d // HEAD_A
    shift0 = jnp.zeros((n_a, bsz, d), x_prompt.dtype)
    wkv0 = jnp.zeros((n_a, bsz, h_a, HEAD_A, HEAD_A), F32)
    y_p, sh_p, wkv_p, k_p, v_p, lf_p = _trunk(x_prompt, shift0, wkv0, None, p)

    n_phys, page = cache_k.shape[:2]
    dec_t = x_sample.shape[1]
    lf_t = jnp.repeat(jnp.swapaxes(cache_logf, 1, 2), dec_t, axis=1)
    past = (cache_k.reshape(n_phys, page, d), cache_v.reshape(n_phys, page, d), lf_t, page_table)
    y_s, sh_s, wkv_s, k_s, v_s, lf_s = _trunk(x_sample, state_shift, state_wkv, past, p)
    return (y_p, y_s, wkv_p, sh_p, k_p, v_p, lf_p, wkv_s, sh_s, k_s, v_s, lf_s)
```

```python
import functools
import math

import jax
import jax.numpy as jnp
from jax import lax
from jax.experimental import pallas as pl
from jax.experimental.pallas import tpu as pltpu

F32 = jnp.float32
BF16 = jnp.bfloat16

HEAD_A = 64
HEAD_B = 128
LANES = 128
RMS_EPS = 1e-6
LNX_EPS = 64e-5
NEG_BIG = -1e30
LOG2E = math.log2(math.e)
VMEM_LIMIT = 56 * 1024 * 1024


def _cparams(*sem):
    return pltpu.CompilerParams(dimension_semantics=sem, vmem_limit_bytes=VMEM_LIMIT)


def _dot(a, b):
    return jnp.dot(a, b, preferred_element_type=F32)


def _dot_nt(a, b):
    return lax.dot_general(a, b, (((1,), (1,)), ((), ())), preferred_element_type=F32)


def _split(x, n):
    pieces = []
    for _ in range(n - 1):
        hi = x.astype(BF16)
        pieces.append(hi)
        x = x - hi.astype(F32)
    pieces.append(x.astype(BF16))
    return pieces


def _dot_l(x, w, n):
    return sum(_dot(piece, w) for piece in _split(x, n))


def _dot_r(w, x, n):
    return sum(_dot(w, piece) for piece in _split(x, n))


def _rms(x, g):
    return x * lax.rsqrt(jnp.mean(x * x, axis=-1, keepdims=True) + RMS_EPS) * g


def _softplus(z):
    return jnp.maximum(z, 0.0) + jnp.log1p(jnp.exp(-jnp.abs(z)))


def _row_tile(m):
    return min(512, m)


def _rmsnorm_kernel(x_ref, g_ref, o_ref):
    o_ref[...] = _rms(x_ref[...], g_ref[...])


def _rmsnorm(x, g):
    m, d = x.shape
    tm = _row_tile(m)
    return pl.pallas_call(
        _rmsnorm_kernel,
        grid=(m // tm,),
        in_specs=[pl.BlockSpec((tm, d), lambda i: (i, 0)), pl.BlockSpec((1, d), lambda i: (0, 0))],
        out_specs=pl.BlockSpec((tm, d), lambda i: (i, 0)),
        out_shape=jax.ShapeDtypeStruct((m, d), F32),
        compiler_params=_cparams("parallel"),
        name="rmsnorm",
    )(x, g.reshape(1, d))


def _norm_matmul_kernel(x_ref, g_ref, w_ref, b_ref, *refs, epilogue, scale):
    o_refs, xn_scr = refs[:-1], refs[-1]

    @pl.when(pl.program_id(1) == 0)
    def _():
        xn_scr[...] = _rms(x_ref[...], g_ref[...]).astype(BF16)

    acc = _dot(xn_scr[...], w_ref[...])
    if epilogue == "log_sigmoid":
        acc = -_softplus(-(acc + b_ref[...]))
    elif epilogue == "scale":
        acc = acc * scale
    for o_ref in o_refs:
        o_ref[...] = acc.astype(o_ref.dtype)


def _norm_matmul(x, g, w, out_dtypes, epilogue="none", bias=None, scale=1.0):
    m, d = x.shape
    n = w.shape[1]
    tm, tn = _row_tile(m), min(512, n)
    if bias is None:
        bias = jnp.zeros((1, n), F32)
    outs = pl.pallas_call(
        functools.partial(_norm_matmul_kernel, epilogue=epilogue, scale=scale),
        grid=(m // tm, n // tn),
        in_specs=[pl.BlockSpec((tm, d), lambda i, j: (i, 0)),
                  pl.BlockSpec((1, d), lambda i, j: (0, 0)),
                  pl.BlockSpec((d, tn), lambda i, j: (0, j)),
                  pl.BlockSpec((1, tn), lambda i, j: (0, j))],
        out_specs=[pl.BlockSpec((tm, tn), lambda i, j: (i, j)) for _ in out_dtypes],
        out_shape=[jax.ShapeDtypeStruct((m, n), dt) for dt in out_dtypes],
        scratch_shapes=[pltpu.VMEM((tm, d), BF16)],
        compiler_params=_cparams("parallel", "arbitrary"),
        name="norm_matmul_" + epilogue,
    )(x, g.reshape(1, d), w, bias)
    return outs


def _matmul_residual_kernel(x_ref, w_ref, h_ref, o_ref):
    o_ref[...] = h_ref[...] + _dot(x_ref[...], w_ref[...])


def _matmul_residual(x, w, h):
    m, k = x.shape
    n = w.shape[1]
    tm, tn = _row_tile(m), min(512, n)
    return pl.pallas_call(
        _matmul_residual_kernel,
        grid=(m // tm, n // tn),
        in_specs=[pl.BlockSpec((tm, k), lambda i, j: (i, 0)),
                  pl.BlockSpec((k, tn), lambda i, j: (0, j)),
                  pl.BlockSpec((tm, tn), lambda i, j: (i, j))],
        out_specs=pl.BlockSpec((tm, tn), lambda i, j: (i, j)),
        out_shape=jax.ShapeDtypeStruct((m, n), F32),
        compiler_params=_cparams("parallel", "parallel"),
        name="matmul_residual",
    )(x, w, h)


def _ffn_kernel(h_ref, g_ref, w1_ref, w3_ref, w2_ref, o_ref, xn_scr, acc_scr):
    j = pl.program_id(1)

    @pl.when(j == 0)
    def _():
        xn_scr[...] = _rms(h_ref[...], g_ref[...]).astype(BF16)
        acc_scr[...] = jnp.zeros_like(acc_scr)

    x = xn_scr[...]
    a = _dot(x, w1_ref[...])
    b = _dot(x, w3_ref[...])
    hm = (a * jax.nn.sigmoid(a) * b).astype(BF16)
    acc_scr[...] += _dot(hm, w2_ref[...])

    @pl.when(j == pl.num_programs(1) - 1)
    def _():
        o_ref[...] = h_ref[...] + acc_scr[...]


def _ffn(h, g, w1, w3, w2):
    m, d = h.shape
    f = w1.shape[1]
    tm, tf = _row_tile(m), 512
    return pl.pallas_call(
        _ffn_kernel,
        grid=(m // tm, f // tf),
        in_specs=[pl.BlockSpec((tm, d), lambda i, j: (i, 0)),
                  pl.BlockSpec((1, d), lambda i, j: (0, 0)),
                  pl.BlockSpec((d, tf), lambda i, j: (0, j)),
                  pl.BlockSpec((d, tf), lambda i, j: (0, j)),
                  pl.BlockSpec((tf, d), lambda i, j: (j, 0))],
        out_specs=pl.BlockSpec((tm, d), lambda i, j: (i, 0)),
        out_shape=jax.ShapeDtypeStruct((m, d), F32),
        scratch_shapes=[pltpu.VMEM((tm, d), BF16), pltpu.VMEM((tm, d), F32)],
        compiler_params=_cparams("parallel", "arbitrary"),
        name="ffn",
    )(h, g.reshape(1, d), w1, w3, w2)


def _mix_kernel(xn_ref, xp_ref, vf_ref, mu_ref, wr_ref, wk_ref, wv_ref,
                w1_ref, w2_ref, a1_ref, a2_ref, v1_ref, v2_ref, g1_ref, g2_ref,
                w0_ref, a0_ref, v0_ref,
                r_ref, lw_ref, k_ref, v_ref, a_ref, g_ref,
                xr_scr, xk_scr, xv_scr, hw_scr, ha_scr, hv_scr, hg_scr, *, has_v):
    @pl.when(pl.program_id(1) == 0)
    def _():
        xn = xn_ref[...]
        xx = xp_ref[...] - xn
        mix = lambda i: (xn + xx * mu_ref[i:i + 1, :]).astype(BF16)
        xr_scr[...] = mix(0)
        hw_scr[...] = jnp.tanh(_dot(mix(1), w1_ref[...])).astype(BF16)
        xk_scr[...] = mix(2)
        xv = mix(3)
        xv_scr[...] = xv
        if has_v:
            hv_scr[...] = _dot(xv, v1_ref[...]).astype(BF16)
        ha_scr[...] = _dot(mix(4), a1_ref[...]).astype(BF16)
        hg_scr[...] = jax.nn.sigmoid(_dot(mix(5), g1_ref[...])).astype(BF16)

    r_ref[...] = _dot(xr_scr[...], wr_ref[...])
    k_ref[...] = _dot(xk_scr[...], wk_ref[...])
    v = _dot(xv_scr[...], wv_ref[...])
    if has_v:
        v = v + (vf_ref[...] - v) * jax.nn.sigmoid(v0_ref[...] + _dot(hv_scr[...], v2_ref[...]))
    v_ref[...] = v
    w_log = -_softplus(-(w0_ref[...] + _dot(hw_scr[...], w2_ref[...]))) - 0.5
    lw_ref[...] = -jnp.exp(w_log)
    a_ref[...] = jax.nn.sigmoid(a0_ref[...] + _dot(ha_scr[...], a2_ref[...]))
    g_ref[...] = _dot(hg_scr[...], g2_ref[...])


def _mix(xn, xprev, vfirst, mu, wr, wk, wv, w1, w2, a1, a2, v1, v2, g1, g2, w0, a0, v0, has_v):
    m, d = xn.shape
    tm, tn = _row_tile(m), 256
    lw_, la_, lv_, lg_ = w1.shape[1], a1.shape[1], v1.shape[1], g1.shape[1]
    row = lambda i, j: (i, 0)
    col = lambda i, j: (0, j)
    tile = lambda i, j: (i, j)
    fixed = lambda i, j: (0, 0)
    outs = pl.pallas_call(
        functools.partial(_mix_kernel, has_v=has_v),
        grid=(m // tm, d // tn),
        in_specs=[pl.BlockSpec((tm, d), row), pl.BlockSpec((tm, d), row), pl.BlockSpec((tm, tn), tile),
                  pl.BlockSpec((8, d), fixed),
                  pl.BlockSpec((d, tn), col), pl.BlockSpec((d, tn), col), pl.BlockSpec((d, tn), col),
                  pl.BlockSpec((d, lw_), fixed), pl.BlockSpec((lw_, tn), col),
                  pl.BlockSpec((d, la_), fixed), pl.BlockSpec((la_, tn), col),
                  pl.BlockSpec((d, lv_), fixed), pl.BlockSpec((lv_, tn), col),
                  pl.BlockSpec((d, lg_), fixed), pl.BlockSpec((lg_, tn), col),
                  pl.BlockSpec((1, tn), col), pl.BlockSpec((1, tn), col), pl.BlockSpec((1, tn), col)],
        out_specs=[pl.BlockSpec((tm, tn), tile) for _ in range(6)],
        out_shape=[jax.ShapeDtypeStruct((m, d), F32) for _ in range(6)],
        scratch_shapes=[pltpu.VMEM((tm, d), BF16), pltpu.VMEM((tm, d), BF16), pltpu.VMEM((tm, d), BF16),
                        pltpu.VMEM((tm, lw_), BF16), pltpu.VMEM((tm, la_), BF16),
                        pltpu.VMEM((tm, lv_), BF16), pltpu.VMEM((tm, lg_), BF16)],
        compiler_params=_cparams("parallel", "arbitrary"),
        name="rwkv_mix_proj",
    )(xn, xprev, vfirst, mu, wr, wk, wv, w1, w2, a1, a2, v1, v2, g1, g2, w0, a0, v0)
    return outs


SCAN_CHUNK = 64
SCAN_PAIRS = 8


def _scan_kernel(r_ref, lw_ref, kr_ref, v_ref, a_ref, g_ref,
                 kk_ref, ka_ref, rk_ref, lnw_ref, lnb_ref, s0_ref,
                 y_ref, sout_ref, st_scr):
    c = pl.program_id(2)
    C = SCAN_CHUNK
    R = 2 * C
    n_pairs = st_scr.shape[0]

    @pl.when(c == 0)
    def _():
        st_scr[...] = s0_ref[0]

    lane_c = lax.broadcasted_iota(jnp.int32, (C, LANES), 1)
    head0 = lane_c < HEAD_A
    row = lax.broadcasted_iota(jnp.int32, (R, R), 0)
    col = lax.broadcasted_iota(jnp.int32, (R, R), 1)
    same = (row // C) == (col // C)
    strict = same & ((col % C) < (row % C))
    incl = same & ((col % C) <= (row % C))
    ones_bd = ((row // HEAD_A) == (col // HEAD_A)).astype(BF16)
    ti = lax.broadcasted_iota(jnp.int32, (C, C), 0)
    tj = lax.broadcasted_iota(jnp.int32, (C, C), 1)
    tri_incl = (tj <= ti).astype(BF16)
    zeros_c = jnp.zeros((C, LANES), F32)

    def stack(x):
        return jnp.concatenate([jnp.where(head0, x, 0.0), jnp.where(head0, 0.0, x)], axis=0)

    sls = [slice(p * LANES, (p + 1) * LANES) for p in range(n_pairs)]
    each = lambda f, *cols: [f(*xs) for xs in zip(*cols)]
    bf = lambda xs: [x.astype(BF16) for x in xs]

    r = [r_ref[0, :, sl] for sl in sls]
    lw = [lw_ref[0, :, sl] for sl in sls]
    kr = [kr_ref[0, :, sl] for sl in sls]
    v = [v_ref[0, :, sl] for sl in sls]
    a = [a_ref[0, :, sl] for sl in sls]

    kkr = each(lambda x, sl: x * kk_ref[:, sl], kr, sls)
    ss = each(lambda x: _dot_l(x * x, ones_bd, 2), kkr)
    kk = each(lambda x, s: x / jnp.maximum(jnp.sqrt(s), 1e-12), kkr, ss)
    k = each(lambda x, y, sl: x * (1.0 + (y - 1.0) * ka_ref[:, sl]), kr, a, sls)
    b = each(lambda x, y: x * y, kk, a)

    cum = each(lambda x: _dot_r(tri_incl, x, 3), lw)
    g_in = each(jnp.exp, cum)
    g_ex = each(lambda x, y: jnp.exp(x - y), cum, lw)
    g_inv = each(lambda x: jnp.exp(-x), cum)
    g_tail = each(lambda x: jnp.exp(x[C - 1:C, :] - x), cum)

    at = each(lambda x, y: stack(-x * y), kk, g_ex)
    at_b = bf(at)
    rt = bf(each(lambda x, y: stack(x * y), r, g_in))
    bt = bf(each(lambda x, y: stack(x * y), b, g_inv))
    kt = bf(each(lambda x, y: stack(x * y), k, g_inv))
    bh = each(lambda x, y: stack(x * y), b, g_tail)
    kh = each(lambda x, y: stack(x * y), k, g_tail)
    vs = bf(each(stack, v))

    sc = each(lambda w, x, y, z: _dot_nt(jnp.concatenate([w, x], axis=0), jnp.concatenate([y, z], axis=0)),
              at_b, rt, bt, kt)
    a_ab = each(lambda x: jnp.where(strict, x[:R, :R], 0.0), sc)
    a_ak = bf(each(lambda x: jnp.where(strict, x[:R, R:], 0.0), sc))
    a_rb = bf(each(lambda x: jnp.where(incl, x[R:, :R], 0.0), sc))
    a_rk = bf(each(lambda x: jnp.where(incl, x[R:, R:], 0.0), sc))

    x = each(lambda w, y, z: jnp.concatenate([w, _dot(y, z)], axis=1), at, a_ak, vs)
    pw = a_ab
    n_sq = int(math.log2(C))
    for it in range(n_sq):
        pw_b = bf(pw)
        x = each(lambda y, z: y + _dot(z, y.astype(BF16)), x, pw_b)
        if it + 1 < n_sq:
            pw = each(lambda z: _dot(z, z), pw_b)
    a_hat = bf(each(lambda y: y[:, :LANES], x))
    u_bar = each(lambda y: y[:, LANES:], x)

    st = [st_scr[p] for p in range(n_pairs)]
    st_b = bf(st)
    u = each(lambda w, y, z: _dot(w, y) + z, a_hat, st_b, u_bar)
    u_b = bf(u)
    o_st = each(lambda q, s, w, y, z, t: _dot(q, s) + _dot(w, y) + _dot(z, t), rt, st_b, a_rb, u_b, a_rk, vs)
    o = each(lambda y: y[:C] + y[C:], o_st)

    def decay_col(x):
        x_t = jnp.concatenate([x, zeros_c], axis=0).T if C < LANES else x.T
        return jnp.exp(jnp.sum(x_t, axis=1, keepdims=True))

    dcol = each(decay_col, lw)
    st_new = each(lambda dc, s, w, y, z, t: dc * s + _dot(w.T.astype(BF16), y) + _dot(z.T.astype(BF16), t),
                  dcol, st, bh, u_b, kh, vs)
    for p in range(n_pairs):
        st_scr[p] = st_new[p]

    mean = each(lambda y: _dot_l(y, ones_bd, 2) * (1.0 / HEAD_A), o)
    dlt = each(lambda y, z: y - z, o, mean)
    var = each(lambda y: _dot_l(y * y, ones_bd, 2) * (1.0 / HEAD_A), dlt)
    on = each(lambda y, z, sl: y * lax.rsqrt(z + LNX_EPS) * lnw_ref[:, sl] + lnb_ref[:, sl], dlt, var, sls)
    bonus = each(lambda x, y, z, sl: _dot_l(x * y * rk_ref[:, sl], ones_bd, 2) * z, r, k, v, sls)
    for p, sl in enumerate(sls):
        y_ref[0, :, sl] = ((on[p] + bonus[p]) * g_ref[0, :, sl]).astype(y_ref.dtype)

    @pl.when(c == pl.num_programs(2) - 1)
    def _():
        sout_ref[0] = st_scr[...]


def _scan(r, lw, kr, v, a, g, k_k, k_a, r_k, lnx_w, lnx_b, s0_bd):
    bsz, t, d = r.shape
    C = SCAN_CHUNK
    P = min(SCAN_PAIRS, d // LANES)
    w = P * LANES
    n_pg = d // w
    seq = pl.BlockSpec((1, C, w), lambda b, q, c: (b, c, q))
    vec = pl.BlockSpec((1, w), lambda b, q, c: (0, q))
    sbd = pl.BlockSpec((1, P, LANES, LANES), lambda b, q, c: (b, q, 0, 0))
    y, s_out = pl.pallas_call(
        _scan_kernel,
        grid=(bsz, n_pg, t // C),
        in_specs=[seq] * 6 + [vec] * 5 + [sbd],
        out_specs=[seq, sbd],
        out_shape=[jax.ShapeDtypeStruct((bsz, t, d), BF16),
                   jax.ShapeDtypeStruct(s0_bd.shape, F32)],
        scratch_shapes=[pltpu.VMEM((P, LANES, LANES), F32)],
        compiler_params=_cparams("parallel", "parallel", "arbitrary"),
        name="wkv7_scan",
    )(r, lw, kr, v, a, g, k_k.reshape(1, d), k_a.reshape(1, d), r_k.reshape(1, d),
      lnx_w.reshape(1, d), lnx_b.reshape(1, d), s0_bd)
    return y, s_out


def _state_to_bd(s):
    bsz, h = s.shape[:2]
    st = jnp.swapaxes(s, 2, 3).reshape(bsz, h // 2, 2, HEAD_A, HEAD_A)
    eye = jnp.eye(2, dtype=s.dtype)
    bd = st[:, :, :, :, None, :] * eye[None, None, :, None, :, None]
    return bd.reshape(bsz, h // 2, 2 * HEAD_A, 2 * HEAD_A)


def _bd_to_state(bd):
    bsz, hp = bd.shape[:2]
    x = bd.reshape(bsz, hp, 2, HEAD_A, 2, HEAD_A)
    st = jnp.stack([x[:, :, 0, :, 0, :], x[:, :, 1, :, 1, :]], axis=2)
    return jnp.swapaxes(st.reshape(bsz, hp * 2, HEAD_A, HEAD_A), 2, 3)


def _cumsum_kernel(x_ref, hi_ref, mid_ref, lo_ref, carry_scr):
    @pl.when(pl.program_id(1) == 0)
    def _():
        carry_scr[...] = jnp.zeros_like(carry_scr)

    tt = x_ref.shape[2]
    u = lax.broadcasted_iota(jnp.int32, (tt, tt), 0)
    s = lax.broadcasted_iota(jnp.int32, (tt, tt), 1)
    tri = (u <= s).astype(BF16)
    c = _dot_l(x_ref[0], tri, 3) + carry_scr[...]
    carry_scr[...] = c[:, tt - 1:tt]
    hi_ref[0], mid_ref[0], lo_ref[0] = _split(c * LOG2E, 3)


def _cumsum_rows_log2(x):
    bsz, h, t = x.shape
    tt = min(512, t)
    spec = pl.BlockSpec((1, h, tt), lambda b, i: (b, 0, i))
    return pl.pallas_call(
        _cumsum_kernel,
        grid=(bsz, t // tt),
        in_specs=[spec],
        out_specs=[spec] * 3,
        out_shape=[jax.ShapeDtypeStruct((bsz, h, t), BF16)] * 3,
        scratch_shapes=[pltpu.VMEM((h, 1), F32)],
        compiler_params=_cparams("parallel", "arbitrary"),
        name="logf_cumsum",
    )(x)


def _flash_kernel(q_ref, qa_ref, k_ref, ka_ref, v_ref, o_ref, m_scr, acc_scr):
    i, j = pl.program_id(1), pl.program_id(2)
    tq, tk = q_ref.shape[1], k_ref.shape[1]
    n_heads = q_ref.shape[2] // HEAD_B

    @pl.when(j == 0)
    def _():
        m_scr[...] = jnp.full_like(m_scr, NEG_BIG)
        acc_scr[...] = jnp.zeros_like(acc_scr)

    def step(masked):
        if masked:
            qpos = lax.broadcasted_iota(jnp.int32, (tq, tk), 0)
            kpos = lax.broadcasted_iota(jnp.int32, (tq, tk), 1)
            keep = kpos <= qpos
        ones = jnp.ones((tk, HEAD_B), BF16)
        for h in range(n_heads):
            hs = slice(h * HEAD_B, (h + 1) * HEAD_B)
            q_aug = jnp.concatenate([q_ref[0, :, hs], qa_ref[0, :, hs]], axis=1)
            k_aug = jnp.concatenate([k_ref[0, :, hs], ka_ref[0, :, hs]], axis=1)
            s = _dot_nt(q_aug, k_aug)
            if masked:
                s = jnp.where(keep, s, NEG_BIG)
            m_prev = m_scr[h]
            m_next = jnp.maximum(m_prev, jnp.max(s, axis=1, keepdims=True))
            alpha = jnp.exp2(m_prev - m_next)
            p = jnp.exp2(s - pltpu.repeat(m_next, tk // LANES, axis=1))
            v_aug = jnp.concatenate([v_ref[0, :, hs], ones], axis=1)
            acc_scr[h] = pltpu.repeat(alpha, 2, axis=1) * acc_scr[h] + _dot(p.astype(BF16), v_aug)
            m_scr[h] = m_next

    @pl.when(j < i)
    def _():
        step(False)

    @pl.when(j == i)
    def _():
        step(True)
        for h in range(n_heads):
            hs = slice(h * HEAD_B, (h + 1) * HEAD_B)
            acc = acc_scr[h]
            o_ref[0, :, hs] = (acc[:, :HEAD_B] / acc[:, HEAD_B:]).astype(o_ref.dtype)


def _flash_attention(q, q_aug, k, k_aug, v):
    bsz, t, d = q.shape
    h = d // HEAD_B
    tq = min(512, t)
    nq = t // tq
    q_idx = lambda b, i, j: (b, i, 0)
    kv_idx = lambda b, i, j: (b, jnp.minimum(j, i), 0)
    return pl.pallas_call(
        _flash_kernel,
        grid=(bsz, nq, nq),
        in_specs=[pl.BlockSpec((1, tq, d), q_idx), pl.BlockSpec((1, tq, d), q_idx),
                  pl.BlockSpec((1, tq, d), kv_idx), pl.BlockSpec((1, tq, d), kv_idx),
                  pl.BlockSpec((1, tq, d), kv_idx)],
        out_specs=pl.BlockSpec((1, tq, d), q_idx),
        out_shape=jax.ShapeDtypeStruct((bsz, t, d), BF16),
        scratch_shapes=[pltpu.VMEM((h, tq, LANES), F32), pltpu.VMEM((h, tq, 2 * HEAD_B), F32)],
        compiler_params=_cparams("parallel", "parallel", "arbitrary"),
        name="fox_prompt_attention",
    )(q, q_aug, k, k_aug, v)


def _bias_lanes(hi, mid, lo):
    bsz, h, t = hi.shape
    one = jnp.ones_like(hi)

    def lanes(parts):
        x = jnp.stack(parts, axis=-1)
        x = jnp.pad(jnp.swapaxes(x, 1, 2), ((0, 0), (0, 0), (0, 0), (0, HEAD_B - len(parts))))
        return x.reshape(bsz, t, h * HEAD_B)

    return lanes([hi, mid, lo, one, one, one]), lanes([one, one, one, -hi, -mid, -lo])


PAGES_PER_STEP = 4


def _paged_kernel(pt_ref, q_ref, kn_ref, vn_ref, gn_ref, *refs, n_group):
    del pt_ref
    kp_refs, vp_refs, lf_refs = refs[:n_group], refs[n_group:2 * n_group], refs[2 * n_group:3 * n_group]
    o_ref, m_scr, l_scr, acc_scr, cq_scr, carry_scr = refs[3 * n_group:]
    p = pl.program_id(1)
    rows = q_ref.shape[1]
    n_heads = kn_ref.shape[1]
    nq = rows // n_heads
    page = lf_refs[0].shape[2]
    row = lax.broadcasted_iota(jnp.int32, (rows, page), 0)
    lane = lax.broadcasted_iota(jnp.int32, (rows, page), 1)
    u = lax.broadcasted_iota(jnp.int32, (page, page), 0)
    s_ = lax.broadcasted_iota(jnp.int32, (page, page), 1)

    def update(s, v_of_head):
        m_old = m_scr[...]
        m_new = jnp.maximum(m_old, jnp.max(s, axis=1, keepdims=True))
        alpha = jnp.exp(m_old - m_new)
        pr = jnp.exp(s - m_new)
        l_scr[...] = alpha * l_scr[...] + jnp.sum(pr, axis=1, keepdims=True)
        pv = [_dot(pr[h * nq:(h + 1) * nq, :].astype(BF16), v_of_head(h)) for h in range(n_heads)]
        acc_scr[...] = alpha * acc_scr[...] + jnp.concatenate(pv, axis=0)
        m_scr[...] = m_new

    def logits(k_of_head):
        return jnp.concatenate(
            [_dot_nt(q_ref[0, h * nq:(h + 1) * nq, :], k_of_head(h)) for h in range(n_heads)], axis=0)

    @pl.when(p == 0)
    def _():
        m_scr[...] = jnp.full_like(m_scr, NEG_BIG)
        l_scr[...] = jnp.zeros_like(l_scr)
        acc_scr[...] = jnp.zeros_like(acc_scr)
        carry_scr[...] = jnp.zeros_like(carry_scr)
        gn = gn_ref[0]
        qi = row % nq
        c_new_col = jnp.sum(jnp.where(lane <= qi, gn, 0.0), axis=1, keepdims=True)
        cq_scr[...] = c_new_col
        c_new_row = _dot_l(gn, (u <= s_).astype(BF16), 3)
        s = logits(lambda h: kn_ref[0, h]) + (c_new_col - c_new_row)
        s = jnp.where(lane <= qi, s, NEG_BIG)
        update(s, lambda h: vn_ref[0, h])

    @pl.when(p > 0)
    def _():
        tri = (u > s_).astype(BF16)
        carry = carry_scr[...]
        bias = []
        for g in range(n_group):
            lf = lf_refs[g][0]
            bias.append(_dot_l(lf, tri, 3) + carry)
            carry = carry + jnp.sum(lf, axis=1, keepdims=True)
        carry_scr[...] = carry
        head_tile = lambda refs_, h: jnp.concatenate(
            [r_[0, pl.ds(h, page, stride=n_heads), :] for r_ in refs_], axis=0).astype(BF16)
        s = logits(lambda h: head_tile(kp_refs, h)) + (cq_scr[...] + jnp.concatenate(bias, axis=1))
        update(s, lambda h: head_tile(vp_refs, h))

    @pl.when(p == pl.num_programs(1) - 1)
    def _():
        o_ref[0] = acc_scr[...] / l_scr[...]


def _paged_attention(page_table, q, k_new, v_new, g_new, cache_k, cache_v, lf_t):
    dbs, rows, _ = q.shape
    n_heads = k_new.shape[1]
    n_pages = page_table.shape[1]
    page = lf_t.shape[2]
    n_group = math.gcd(PAGES_PER_STEP, n_pages)
    per_b = lambda b, p, pt: (b, 0, 0)
    per_b4 = lambda b, p, pt: (b, 0, 0, 0)

    def paged(g):
        return lambda b, p, pt: (pt[b, n_pages - 1 - ((jnp.maximum(p, 1) - 1) * n_group + g)], 0, 0)

    grid_spec = pltpu.PrefetchScalarGridSpec(
        num_scalar_prefetch=1,
        grid=(dbs, n_pages // n_group + 1),
        in_specs=[pl.BlockSpec((1, rows, HEAD_B), per_b),
                  pl.BlockSpec((1, n_heads, page, HEAD_B), per_b4),
                  pl.BlockSpec((1, n_heads, page, HEAD_B), per_b4),
                  pl.BlockSpec((1, rows, page), per_b)]
                 + [pl.BlockSpec((1, page * n_heads, HEAD_B), paged(g)) for g in range(n_group)]
                 + [pl.BlockSpec((1, page * n_heads, HEAD_B), paged(g)) for g in range(n_group)]
                 + [pl.BlockSpec((1, rows, page), paged(g)) for g in range(n_group)],
        out_specs=pl.BlockSpec((1, rows, HEAD_B), per_b),
        scratch_shapes=[pltpu.VMEM((rows, 1), F32), pltpu.VMEM((rows, 1), F32),
                        pltpu.VMEM((rows, HEAD_B), F32), pltpu.VMEM((rows, 1), F32),
                        pltpu.VMEM((rows, 1), F32)],
    )
    return pl.pallas_call(
        functools.partial(_paged_kernel, n_group=n_group),
        grid_spec=grid_spec,
        out_shape=jax.ShapeDtypeStruct((dbs, rows, HEAD_B), F32),
        compiler_params=_cparams("parallel", "arbitrary"),
        name="fox_paged_attention",
    )(page_table, q, k_new, v_new, g_new, *([cache_k] * n_group), *([cache_v] * n_group),
      *([lf_t] * n_group))


def _trunk(x, shift0, wkv0, past, p):
    bsz, t, d = x.shape
    m = bsz * t
    n_a = p["tm_w_r"].shape[0]
    depth = p["ln1"].shape[0]
    h_b = d // HEAD_B
    h = x.reshape(m, d)
    v_first = None
    shifts, states = [], []
    t_scan = -(-t // SCAN_CHUNK) * SCAN_CHUNK
    heads_first = lambda z: jnp.swapaxes(z.reshape(bsz, t, h_b, HEAD_B), 1, 2)
    for l in range(depth):
        if l == n_a:
            k_new, k_bf = _norm_matmul(h, p["kv_norm"], p["fox_w_k"], (F32, BF16))
            v_new, v_bf = _norm_matmul(h, p["kv_norm"], p["fox_w_v"], (F32, BF16))
            (lf_pad,) = _norm_matmul(h, p["kv_norm"], p["fox_w_f"], (F32,), epilogue="log_sigmoid",
                                     bias=p["fox_b_f"])
            logf_new = lf_pad[:, :h_b].reshape(bsz, t, h_b)
            if past is None:
                q_aug, k_aug = _bias_lanes(*_cumsum_rows_log2(jnp.swapaxes(logf_new, 1, 2)))
            else:
                page = past[2].shape[2]
                pad_keys = lambda z: jnp.pad(heads_first(z), ((0, 0), (0, 0), (0, page - t), (0, 0)))
                k_pad, v_pad = pad_keys(k_bf), pad_keys(v_bf)
                g_new = jnp.repeat(jnp.swapaxes(logf_new, 1, 2), t, axis=1)
                g_new = jnp.pad(g_new, ((0, 0), (0, 0), (0, page - t)))
        if l < n_a:
            xn = _rmsnorm(h, p["ln1"][l])
            xn3 = xn.reshape(bsz, t, d)
            shifts.append(xn3[:, -1])
            xprev = jnp.concatenate([shift0[l][:, None, :], xn3[:, :-1]], axis=1).reshape(m, d)
            has_v = l > 0
            lv = max(l - 1, 0)
            r, lw, kr, v, a, g = _mix(
                xn, xprev, v_first if has_v else xn, p["mu8"][l],
                p["tm_w_r"][l], p["tm_w_k"][l], p["tm_w_v"][l],
                p["tm_w1"][l], p["tm_w2"][l], p["tm_a1"][l], p["tm_a2"][l],
                p["tm_v1"][lv], p["tm_v2"][lv], p["tm_g1"][l], p["tm_g2"][l],
                p["tm_w0"][l].reshape(1, d), p["tm_a0"][l].reshape(1, d), p["tm_v0"][lv].reshape(1, d),
                has_v)
            if l == 0:
                v_first = v
            seq = lambda z: jnp.pad(z.reshape(bsz, t, d), ((0, 0), (0, t_scan - t), (0, 0)))
            y, s_bd = _scan(seq(r), seq(lw), seq(kr), seq(v), seq(a), seq(g),
                            p["tm_k_k"][l], p["tm_k_a"][l], p["tm_r_k"][l].reshape(d),
                            p["tm_lnx_w"][l], p["tm_lnx_b"][l], _state_to_bd(wkv0[l]))
            states.append(_bd_to_state(s_bd))
            h = _matmul_residual(y[:, :t].reshape(m, d), p["tm_w_o"][l], h)
        else:
            j = l - n_a
            if past is None:
                (q,) = _norm_matmul(h, p["ln1"][l], p["fox_w_q"][j], (BF16,), epilogue="scale",
                                    scale=HEAD_B ** -0.5 * LOG2E)
                att = _flash_attention(q.reshape(bsz, t, d), q_aug, k_bf.reshape(bsz, t, d), k_aug,
                                       v_bf.reshape(bsz, t, d)).reshape(m, d)
            else:
                (q,) = _norm_matmul(h, p["ln1"][l], p["fox_w_q"][j], (BF16,), epilogue="scale",
                                    scale=HEAD_B ** -0.5)
                cache_k, cache_v, lf_t, page_table = past
                o = _paged_attention(page_table, heads_first(q).reshape(bsz, h_b * t, HEAD_B),
                                     k_pad, v_pad, g_new, cache_k, cache_v, lf_t)
                att = jnp.swapaxes(o.reshape(bsz, h_b, t, HEAD_B), 1, 2).reshape(m, d).astype(BF16)
            h = _matmul_residual(att, p["fox_w_o"][j], h)
        h = _ffn(h, p["ln2"][l], p["ffn_w1"][l], p["ffn_w3"][l], p["ffn_w2"][l])
    y = _rmsnorm(h, p["ln_out"])
    h_sh = (bsz, t, h_b, HEAD_B)
    return (y.reshape(bsz, t, d), jnp.stack(shifts), jnp.stack(states),
            k_new.reshape(h_sh), v_new.reshape(h_sh), logf_new)


def kernel(x_prompt, x_sample, state_wkv, state_shift, cache_k, cache_v, cache_logf, page_table, ln1, ln2, ln_out, tm_mu, tm_w_r, tm_w_k, tm_w_v, tm_w_o, tm_w0, tm_w1, tm_w2, tm_a0, tm_a1, tm_a2, tm_v0, tm_v1, tm_v2, tm_g1, tm_g2, tm_k_k, tm_k_a, tm_r_k, tm_lnx_w, tm_lnx_b, kv_norm, fox_w_k, fox_w_v, fox_w_f, fox_b_f, fox_w_q, fox_w_o, ffn_w1, ffn_w3, ffn_w2):
    bf = lambda w: w.astype(BF16)
    n_a, d = tm_w0.shape
    h_b = d // HEAD_B
    lora = lambda w1, w2: (bf(jnp.pad(w1, ((0, 0), (0, 0), (0, -w1.shape[2] % LANES)))),
                           bf(jnp.pad(w2, ((0, 0), (0, -w2.shape[1] % LANES), (0, 0)))))
    w1, w2 = lora(tm_w1, tm_w2)
    a1, a2 = lora(tm_a1, tm_a2)
    if tm_v1.shape[0] == 0:
        tm_v0 = jnp.zeros((1, d), F32)
        tm_v1 = jnp.zeros((1, d, LANES), F32)
        tm_v2 = jnp.zeros((1, LANES, d), F32)
    v1, v2 = lora(tm_v1, tm_v2)
    g1, g2 = lora(tm_g1, tm_g2)
    p = dict(ln1=ln1, ln2=ln2, ln_out=ln_out,
             mu8=jnp.pad(tm_mu, ((0, 0), (0, 2), (0, 0))),
             tm_w_r=bf(tm_w_r), tm_w_k=bf(tm_w_k), tm_w_v=bf(tm_w_v), tm_w_o=bf(tm_w_o),
             tm_w0=tm_w0, tm_w1=w1, tm_w2=w2, tm_a0=tm_a0, tm_a1=a1, tm_a2=a2,
             tm_v0=tm_v0, tm_v1=v1, tm_v2=v2, tm_g1=g1, tm_g2=g2,
             tm_k_k=tm_k_k, tm_k_a=tm_k_a, tm_r_k=tm_r_k, tm_lnx_w=tm_lnx_w, tm_lnx_b=tm_lnx_b,
             kv_norm=kv_norm, fox_w_k=bf(fox_w_k), fox_w_v=bf(fox_w_v),
             fox_w_f=bf(jnp.pad(fox_w_f, ((0, 0), (0, LANES - h_b)))),
             fox_b_f=jnp.pad(fox_b_f, (0, LANES - h_b)).reshape(1, LANES),
             fox_w_q=bf(fox_w_q), fox_w_o=bf(fox_w_o),
             ffn_w1=bf(ffn_w1), ffn_w3=bf(ffn_w3), ffn_w2=bf(ffn_w2))

    bsz = x_prompt.shape[0]
    h_a = d // HEAD_A
    shift0 = jnp.zeros((n_a, bsz, d), x_prompt.dtype)
    wkv0 = jnp.zeros((n_a, bsz, h_a, HEAD_A, HEAD_A), F32)
    y_p, sh_p, wkv_p, k_p, v_p, lf_p = _trunk(x_prompt, shift0, wkv0, None, p)

    n_phys, page = cache_k.shape[:2]
    dec_t = x_sample.shape[1]
    lf_t = jnp.repeat(jnp.swapaxes(cache_logf, 1, 2), dec_t, axis=1)
    past = (cache_k.reshape(n_phys, page * h_b, HEAD_B), cache_v.reshape(n_phys, page * h_b, HEAD_B),
            lf_t, page_table)
    y_s, sh_s, wkv_s, k_s, v_s, lf_s = _trunk(x_sample, state_shift, state_wkv, past, p)
    return (y_p, y_s, wkv_p, sh_p, k_p, v_p, lf_p, wkv_s, sh_s, k_s, v_s, lf_s)
```

```python
import functools
import math

import jax
import jax.numpy as jnp
from jax import lax
from jax.experimental import pallas as pl
from jax.experimental.pallas import tpu as pltpu

F32 = jnp.float32
BF16 = jnp.bfloat16

HEAD_A = 64
HEAD_B = 128
LANES = 128
RMS_EPS = 1e-6
LNX_EPS = 64e-5
NEG_BIG = -1e30
LOG2E = math.log2(math.e)
VMEM_LIMIT = 56 * 1024 * 1024


def _cparams(*sem):
    return pltpu.CompilerParams(dimension_semantics=sem, vmem_limit_bytes=VMEM_LIMIT)


def _dot(a, b):
    return jnp.dot(a, b, preferred_element_type=F32)


def _dot_nt(a, b):
    return lax.dot_general(a, b, (((1,), (1,)), ((), ())), preferred_element_type=F32)


def _split(x, n):
    pieces = []
    for _ in range(n - 1):
        hi = x.astype(BF16)
        pieces.append(hi)
        x = x - hi.astype(F32)
    pieces.append(x.astype(BF16))
    return pieces


def _dot_l(x, w, n):
    return sum(_dot(piece, w) for piece in _split(x, n))


def _dot_r(w, x, n):
    return sum(_dot(w, piece) for piece in _split(x, n))


def _rms(x, g):
    return x * lax.rsqrt(jnp.mean(x * x, axis=-1, keepdims=True) + RMS_EPS) * g


def _softplus(z):
    return jnp.maximum(z, 0.0) + jnp.log1p(jnp.exp(-jnp.abs(z)))


def _row_tile(m):
    return min(512, m)


def _rmsnorm_kernel(x_ref, g_ref, o_ref):
    o_ref[...] = _rms(x_ref[...], g_ref[...])


def _rmsnorm(x, g):
    m, d = x.shape
    tm = _row_tile(m)
    return pl.pallas_call(
        _rmsnorm_kernel,
        grid=(m // tm,),
        in_specs=[pl.BlockSpec((tm, d), lambda i: (i, 0)), pl.BlockSpec((1, d), lambda i: (0, 0))],
        out_specs=pl.BlockSpec((tm, d), lambda i: (i, 0)),
        out_shape=jax.ShapeDtypeStruct((m, d), F32),
        compiler_params=_cparams("parallel"),
        name="rmsnorm",
    )(x, g.reshape(1, d))


def _norm_matmul_kernel(x_ref, g_ref, b_ref, *refs, n_w, n_out, epilogue, scale):
    w_refs, o_refs, xn_scr = refs[:n_w], refs[n_w:-1], refs[-1]

    @pl.when(pl.program_id(1) == 0)
    def _():
        xn_scr[...] = _rms(x_ref[...], g_ref[...]).astype(BF16)

    for wi, w_ref in enumerate(w_refs):
        acc = _dot(xn_scr[...], w_ref[...])
        if epilogue == "log_sigmoid":
            acc = -_softplus(-(acc + b_ref[...]))
        elif epilogue == "scale":
            acc = acc * scale
        for o_ref in o_refs[wi * n_out:(wi + 1) * n_out]:
            o_ref[...] = acc.astype(o_ref.dtype)


def _norm_matmul(x, g, ws, out_dtypes, epilogue="none", bias=None, scale=1.0):
    m, d = x.shape
    n = ws[0].shape[1]
    tm, tn = min(1024, m), min(512, n)
    if bias is None:
        bias = jnp.zeros((1, n), F32)
    n_res = len(ws) * len(out_dtypes)
    outs = pl.pallas_call(
        functools.partial(_norm_matmul_kernel, n_w=len(ws), n_out=len(out_dtypes), epilogue=epilogue,
                          scale=scale),
        grid=(m // tm, n // tn),
        in_specs=[pl.BlockSpec((tm, d), lambda i, j: (i, 0)),
                  pl.BlockSpec((1, d), lambda i, j: (0, 0)),
                  pl.BlockSpec((1, tn), lambda i, j: (0, j))]
                 + [pl.BlockSpec((d, tn), lambda i, j: (0, j)) for _ in ws],
        out_specs=[pl.BlockSpec((tm, tn), lambda i, j: (i, j)) for _ in range(n_res)],
        out_shape=[jax.ShapeDtypeStruct((m, n), dt) for _ in ws for dt in out_dtypes],
        scratch_shapes=[pltpu.VMEM((tm, d), BF16)],
        compiler_params=_cparams("parallel", "arbitrary"),
        name="norm_matmul_" + epilogue,
    )(x, g.reshape(1, d), bias, *ws)
    return outs


def _matmul_residual_kernel(x_ref, w_ref, h_ref, o_ref):
    o_ref[...] = h_ref[...] + _dot(x_ref[...], w_ref[...])


def _matmul_residual(x, w, h):
    m, k = x.shape
    n = w.shape[1]
    tm, tn = min(1024, m), min(512, n)
    return pl.pallas_call(
        _matmul_residual_kernel,
        grid=(m // tm, n // tn),
        in_specs=[pl.BlockSpec((tm, k), lambda i, j: (i, 0)),
                  pl.BlockSpec((k, tn), lambda i, j: (0, j)),
                  pl.BlockSpec((tm, tn), lambda i, j: (i, j))],
        out_specs=pl.BlockSpec((tm, tn), lambda i, j: (i, j)),
        out_shape=jax.ShapeDtypeStruct((m, n), F32),
        compiler_params=_cparams("parallel", "parallel"),
        name="matmul_residual",
    )(x, w, h)


def _ffn_kernel(h_ref, g_ref, w1_ref, w3_ref, w2_ref, o_ref, xn_scr, acc_scr):
    j = pl.program_id(1)

    @pl.when(j == 0)
    def _():
        xn_scr[...] = _rms(h_ref[...], g_ref[...]).astype(BF16)
        acc_scr[...] = jnp.zeros_like(acc_scr)

    x = xn_scr[...]
    a = _dot(x, w1_ref[...])
    b = _dot(x, w3_ref[...])
    hm = (a * jax.nn.sigmoid(a) * b).astype(BF16)
    acc_scr[...] += _dot(hm, w2_ref[...])

    @pl.when(j == pl.num_programs(1) - 1)
    def _():
        o_ref[...] = h_ref[...] + acc_scr[...]


def _ffn(h, g, w1, w3, w2):
    m, d = h.shape
    f = w1.shape[1]
    tm, tf = _row_tile(m), 512
    return pl.pallas_call(
        _ffn_kernel,
        grid=(m // tm, f // tf),
        in_specs=[pl.BlockSpec((tm, d), lambda i, j: (i, 0)),
                  pl.BlockSpec((1, d), lambda i, j: (0, 0)),
                  pl.BlockSpec((d, tf), lambda i, j: (0, j)),
                  pl.BlockSpec((d, tf), lambda i, j: (0, j)),
                  pl.BlockSpec((tf, d), lambda i, j: (j, 0))],
        out_specs=pl.BlockSpec((tm, d), lambda i, j: (i, 0)),
        out_shape=jax.ShapeDtypeStruct((m, d), F32),
        scratch_shapes=[pltpu.VMEM((tm, d), BF16), pltpu.VMEM((tm, d), F32)],
        compiler_params=_cparams("parallel", "arbitrary"),
        name="ffn",
    )(h, g.reshape(1, d), w1, w3, w2)


def _mix_kernel(h_ref, hp_ref, st_ref, ln_ref, vf_ref, mu_ref, wr_ref, wk_ref, wv_ref,
                w1_ref, w2_ref, a1_ref, a2_ref, v1_ref, v2_ref, g1_ref, g2_ref,
                w0_ref, a0_ref, v0_ref,
                r_ref, lw_ref, k_ref, v_ref, a_ref, g_ref,
                xr_scr, xk_scr, xv_scr, hw_scr, ha_scr, hv_scr, hg_scr, *, has_v, seq_len):
    @pl.when(pl.program_id(1) == 0)
    def _():
        tm = h_ref.shape[0]
        xn = _rms(h_ref[...], ln_ref[...])
        row = lax.broadcasted_iota(jnp.int32, (tm, 1), 0)
        before_tile = _rms(hp_ref[7:8, :], ln_ref[...])
        xprev = jnp.where(row == 0, before_tile, pltpu.roll(xn, 1, axis=0))
        if seq_len >= tm:
            starts_here = pl.program_id(0) % (seq_len // tm) == 0
            xprev = jnp.where((row == 0) & starts_here, st_ref[0, 0:1, :], xprev)
        else:
            for s in range(tm // seq_len):
                xprev = jnp.where(row == s * seq_len, st_ref[0, s:s + 1, :], xprev)
        xx = xprev - xn
        mix = lambda i: (xn + xx * mu_ref[i:i + 1, :]).astype(BF16)
        xr_scr[...] = mix(0)
        hw_scr[...] = jnp.tanh(_dot(mix(1), w1_ref[...])).astype(BF16)
        xk_scr[...] = mix(2)
        xv = mix(3)
        xv_scr[...] = xv
        if has_v:
            hv_scr[...] = _dot(xv, v1_ref[...]).astype(BF16)
        ha_scr[...] = _dot(mix(4), a1_ref[...]).astype(BF16)
        hg_scr[...] = jax.nn.sigmoid(_dot(mix(5), g1_ref[...])).astype(BF16)

    r_ref[...] = _dot(xr_scr[...], wr_ref[...])
    k_ref[...] = _dot(xk_scr[...], wk_ref[...])
    v = _dot(xv_scr[...], wv_ref[...])
    if has_v:
        v = v + (vf_ref[...] - v) * jax.nn.sigmoid(v0_ref[...] + _dot(hv_scr[...], v2_ref[...]))
    v_ref[...] = v
    w_log = -_softplus(-(w0_ref[...] + _dot(hw_scr[...], w2_ref[...]))) - 0.5
    lw_ref[...] = -jnp.exp(w_log)
    a_ref[...] = jax.nn.sigmoid(a0_ref[...] + _dot(ha_scr[...], a2_ref[...]))
    g_ref[...] = _dot(hg_scr[...], g2_ref[...])


def _mix(h, shift_state, seq_len, ln, vfirst, mu, wr, wk, wv, w1, w2, a1, a2, v1, v2, g1, g2, w0, a0, v0,
         has_v):
    m, d = h.shape
    tm, tn = _row_tile(m), 256
    assert seq_len % tm == 0 or tm % seq_len == 0
    n_starts = max(1, tm // seq_len)
    seq_of = [[(i * tm + s * seq_len) // seq_len for s in range(n_starts)] for i in range(m // tm)]
    starts = shift_state[jnp.asarray(seq_of)]
    lw_, la_, lv_, lg_ = w1.shape[1], a1.shape[1], v1.shape[1], g1.shape[1]
    row = lambda i, j: (i, 0)
    col = lambda i, j: (0, j)
    tile = lambda i, j: (i, j)
    fixed = lambda i, j: (0, 0)
    outs = pl.pallas_call(
        functools.partial(_mix_kernel, has_v=has_v, seq_len=seq_len),
        grid=(m // tm, d // tn),
        in_specs=[pl.BlockSpec((tm, d), row),
                  pl.BlockSpec((8, d), lambda i, j: (jnp.maximum(i * (tm // 8) - 1, 0), 0)),
                  pl.BlockSpec((1, n_starts, d), lambda i, j: (i, 0, 0)),
                  pl.BlockSpec((1, d), fixed),
                  pl.BlockSpec((tm, tn), tile),
                  pl.BlockSpec((8, d), fixed),
                  pl.BlockSpec((d, tn), col), pl.BlockSpec((d, tn), col), pl.BlockSpec((d, tn), col),
                  pl.BlockSpec((d, lw_), fixed), pl.BlockSpec((lw_, tn), col),
                  pl.BlockSpec((d, la_), fixed), pl.BlockSpec((la_, tn), col),
                  pl.BlockSpec((d, lv_), fixed), pl.BlockSpec((lv_, tn), col),
                  pl.BlockSpec((d, lg_), fixed), pl.BlockSpec((lg_, tn), col),
                  pl.BlockSpec((1, tn), col), pl.BlockSpec((1, tn), col), pl.BlockSpec((1, tn), col)],
        out_specs=[pl.BlockSpec((tm, tn), tile) for _ in range(6)],
        out_shape=[jax.ShapeDtypeStruct((m, d), F32) for _ in range(6)],
        scratch_shapes=[pltpu.VMEM((tm, d), BF16), pltpu.VMEM((tm, d), BF16), pltpu.VMEM((tm, d), BF16),
                        pltpu.VMEM((tm, lw_), BF16), pltpu.VMEM((tm, la_), BF16),
                        pltpu.VMEM((tm, lv_), BF16), pltpu.VMEM((tm, lg_), BF16)],
        compiler_params=_cparams("parallel", "arbitrary"),
        name="rwkv_mix_proj",
    )(h, h, starts, ln.reshape(1, d), vfirst, mu, wr, wk, wv, w1, w2, a1, a2, v1, v2, g1, g2, w0, a0, v0)
    return outs


SCAN_CHUNK = 64
SCAN_PAIRS = 8


def _scan_kernel(r_ref, lw_ref, kr_ref, v_ref, a_ref, g_ref,
                 kk_ref, ka_ref, rk_ref, lnw_ref, lnb_ref, s0_ref,
                 y_ref, sout_ref, st_scr):
    c = pl.program_id(2)
    C = SCAN_CHUNK
    R = 2 * C
    n_pairs = st_scr.shape[0]

    @pl.when(c == 0)
    def _():
        st_scr[...] = s0_ref[0]

    lane_c = lax.broadcasted_iota(jnp.int32, (C, LANES), 1)
    head0 = lane_c < HEAD_A
    row = lax.broadcasted_iota(jnp.int32, (R, R), 0)
    col = lax.broadcasted_iota(jnp.int32, (R, R), 1)
    same = (row // C) == (col // C)
    strict = same & ((col % C) < (row % C))
    incl = same & ((col % C) <= (row % C))
    ones_bd = ((row // HEAD_A) == (col // HEAD_A)).astype(BF16)
    ti = lax.broadcasted_iota(jnp.int32, (C, C), 0)
    tj = lax.broadcasted_iota(jnp.int32, (C, C), 1)
    tri_incl = (tj <= ti).astype(BF16)
    zeros_c = jnp.zeros((C, LANES), F32)

    def stack(x):
        return jnp.concatenate([jnp.where(head0, x, 0.0), jnp.where(head0, 0.0, x)], axis=0)

    sls = [slice(p * LANES, (p + 1) * LANES) for p in range(n_pairs)]
    each = lambda f, *cols: [f(*xs) for xs in zip(*cols)]
    bf = lambda xs: [x.astype(BF16) for x in xs]

    r = [r_ref[0, :, sl] for sl in sls]
    lw = [lw_ref[0, :, sl] for sl in sls]
    kr = [kr_ref[0, :, sl] for sl in sls]
    v = [v_ref[0, :, sl] for sl in sls]
    a = [a_ref[0, :, sl] for sl in sls]

    kkr = each(lambda x, sl: x * kk_ref[:, sl], kr, sls)
    ss = each(lambda x: _dot_l(x * x, ones_bd, 2), kkr)
    kk = each(lambda x, s: x / jnp.maximum(jnp.sqrt(s), 1e-12), kkr, ss)
    k = each(lambda x, y, sl: x * (1.0 + (y - 1.0) * ka_ref[:, sl]), kr, a, sls)
    b = each(lambda x, y: x * y, kk, a)

    cum = each(lambda x: _dot_r(tri_incl, x, 3), lw)
    g_in = each(jnp.exp, cum)
    g_ex = each(lambda x, y: jnp.exp(x - y), cum, lw)
    g_inv = each(lambda x: jnp.exp(-x), cum)
    g_tail = each(lambda x: jnp.exp(x[C - 1:C, :] - x), cum)

    at = each(lambda x, y: stack(-x * y), kk, g_ex)
    at_b = bf(at)
    rt = bf(each(lambda x, y: stack(x * y), r, g_in))
    bt = bf(each(lambda x, y: stack(x * y), b, g_inv))
    kt = bf(each(lambda x, y: stack(x * y), k, g_inv))
    bh = each(lambda x, y: stack(x * y), b, g_tail)
    kh = each(lambda x, y: stack(x * y), k, g_tail)
    vs = bf(each(stack, v))

    sc = each(lambda w, x, y, z: _dot_nt(jnp.concatenate([w, x], axis=0), jnp.concatenate([y, z], axis=0)),
              at_b, rt, bt, kt)
    a_ab = each(lambda x: jnp.where(strict, x[:R, :R], 0.0), sc)
    a_ak = bf(each(lambda x: jnp.where(strict, x[:R, R:], 0.0), sc))
    a_rb = bf(each(lambda x: jnp.where(incl, x[R:, :R], 0.0), sc))
    a_rk = bf(each(lambda x: jnp.where(incl, x[R:, R:], 0.0), sc))

    x = each(lambda w, y, z: jnp.concatenate([w, _dot(y, z)], axis=1), at, a_ak, vs)
    pw = a_ab
    n_sq = int(math.log2(C))
    for it in range(n_sq):
        pw_b = bf(pw)
        x = each(lambda y, z: y + _dot(z, y.astype(BF16)), x, pw_b)
        if it + 1 < n_sq:
            pw = each(lambda z: _dot(z, z), pw_b)
    a_hat = bf(each(lambda y: y[:, :LANES], x))
    u_bar = each(lambda y: y[:, LANES:], x)

    st = [st_scr[p] for p in range(n_pairs)]
    st_b = bf(st)
    u = each(lambda w, y, z: _dot(w, y) + z, a_hat, st_b, u_bar)
    u_b = bf(u)
    uv = each(lambda y, t: jnp.concatenate([y, t], axis=0), u_b, vs)
    o_st = each(lambda q, w, z, s, y: _dot(jnp.concatenate([q, w, z], axis=1), jnp.concatenate([s, y], axis=0)),
                rt, a_rb, a_rk, st_b, uv)
    o = each(lambda y: y[:C] + y[C:], o_st)

    def decay_col(x):
        x_t = jnp.concatenate([x, zeros_c], axis=0).T if C < LANES else x.T
        return jnp.exp(jnp.sum(x_t, axis=1, keepdims=True))

    dcol = each(decay_col, lw)
    st_new = each(lambda dc, s, w, z, y: dc * s + _dot(jnp.concatenate([w.T, z.T], axis=1).astype(BF16), y),
                  dcol, st, bh, kh, uv)
    for p in range(n_pairs):
        st_scr[p] = st_new[p]

    mean = each(lambda y: _dot_l(y, ones_bd, 2) * (1.0 / HEAD_A), o)
    dlt = each(lambda y, z: y - z, o, mean)
    var = each(lambda y: _dot_l(y * y, ones_bd, 2) * (1.0 / HEAD_A), dlt)
    on = each(lambda y, z, sl: y * lax.rsqrt(z + LNX_EPS) * lnw_ref[:, sl] + lnb_ref[:, sl], dlt, var, sls)
    bonus = each(lambda x, y, z, sl: _dot_l(x * y * rk_ref[:, sl], ones_bd, 2) * z, r, k, v, sls)
    for p, sl in enumerate(sls):
        y_ref[0, :, sl] = ((on[p] + bonus[p]) * g_ref[0, :, sl]).astype(y_ref.dtype)

    @pl.when(c == pl.num_programs(2) - 1)
    def _():
        sout_ref[0] = st_scr[...]


def _scan(r, lw, kr, v, a, g, k_k, k_a, r_k, lnx_w, lnx_b, s0_bd):
    bsz, t, d = r.shape
    C = SCAN_CHUNK
    P = min(SCAN_PAIRS, d // LANES)
    w = P * LANES
    n_pg = d // w
    seq = pl.BlockSpec((1, C, w), lambda b, q, c: (b, c, q))
    vec = pl.BlockSpec((1, w), lambda b, q, c: (0, q))
    sbd = pl.BlockSpec((1, P, LANES, LANES), lambda b, q, c: (b, q, 0, 0))
    y, s_out = pl.pallas_call(
        _scan_kernel,
        grid=(bsz, n_pg, t // C),
        in_specs=[seq] * 6 + [vec] * 5 + [sbd],
        out_specs=[seq, sbd],
        out_shape=[jax.ShapeDtypeStruct((bsz, t, d), BF16),
                   jax.ShapeDtypeStruct(s0_bd.shape, F32)],
        scratch_shapes=[pltpu.VMEM((P, LANES, LANES), F32)],
        compiler_params=_cparams("parallel", "parallel", "arbitrary"),
        name="wkv7_scan",
    )(r, lw, kr, v, a, g, k_k.reshape(1, d), k_a.reshape(1, d), r_k.reshape(1, d),
      lnx_w.reshape(1, d), lnx_b.reshape(1, d), s0_bd)
    return y, s_out


def _state_to_bd(s):
    bsz, h = s.shape[:2]
    st = jnp.swapaxes(s, 2, 3).reshape(bsz, h // 2, 2, HEAD_A, HEAD_A)
    eye = jnp.eye(2, dtype=s.dtype)
    bd = st[:, :, :, :, None, :] * eye[None, None, :, None, :, None]
    return bd.reshape(bsz, h // 2, 2 * HEAD_A, 2 * HEAD_A)


def _bd_to_state(bd):
    bsz, hp = bd.shape[:2]
    x = bd.reshape(bsz, hp, 2, HEAD_A, 2, HEAD_A)
    st = jnp.stack([x[:, :, 0, :, 0, :], x[:, :, 1, :, 1, :]], axis=2)
    return jnp.swapaxes(st.reshape(bsz, hp * 2, HEAD_A, HEAD_A), 2, 3)


def _cumsum_kernel(x_ref, hi_ref, mid_ref, lo_ref, carry_scr):
    @pl.when(pl.program_id(1) == 0)
    def _():
        carry_scr[...] = jnp.zeros_like(carry_scr)

    tt = x_ref.shape[2]
    u = lax.broadcasted_iota(jnp.int32, (tt, tt), 0)
    s = lax.broadcasted_iota(jnp.int32, (tt, tt), 1)
    tri = (u <= s).astype(BF16)
    c = _dot_l(x_ref[0], tri, 3) + carry_scr[...]
    carry_scr[...] = c[:, tt - 1:tt]
    hi_ref[0], mid_ref[0], lo_ref[0] = _split(c * LOG2E, 3)


def _cumsum_rows_log2(x):
    bsz, h, t = x.shape
    tt = min(512, t)
    spec = pl.BlockSpec((1, h, tt), lambda b, i: (b, 0, i))
    return pl.pallas_call(
        _cumsum_kernel,
        grid=(bsz, t // tt),
        in_specs=[spec],
        out_specs=[spec] * 3,
        out_shape=[jax.ShapeDtypeStruct((bsz, h, t), BF16)] * 3,
        scratch_shapes=[pltpu.VMEM((h, 1), F32)],
        compiler_params=_cparams("parallel", "arbitrary"),
        name="logf_cumsum",
    )(x)


def _flash_kernel(q_ref, qa_ref, k_ref, ka_ref, v_ref, o_ref, m_scr, acc_scr):
    i, j = pl.program_id(1), pl.program_id(2)
    tq, tk = q_ref.shape[1], k_ref.shape[1]
    n_heads = q_ref.shape[2] // HEAD_B

    @pl.when(j == 0)
    def _():
        m_scr[...] = jnp.full_like(m_scr, NEG_BIG)
        acc_scr[...] = jnp.zeros_like(acc_scr)

    def step(masked):
        if masked:
            qpos = lax.broadcasted_iota(jnp.int32, (tq, tk), 0)
            kpos = lax.broadcasted_iota(jnp.int32, (tq, tk), 1)
            keep = kpos <= qpos
        ones = jnp.ones((tk, HEAD_B), BF16)
        for h in range(n_heads):
            hs = slice(h * HEAD_B, (h + 1) * HEAD_B)
            q_aug = jnp.concatenate([q_ref[0, :, hs], qa_ref[0, :, hs]], axis=1)
            k_aug = jnp.concatenate([k_ref[0, :, hs], ka_ref[0, :, hs]], axis=1)
            s = _dot_nt(q_aug, k_aug)
            if masked:
                s = jnp.where(keep, s, NEG_BIG)
            m_prev = m_scr[h]
            m_next = jnp.maximum(m_prev, jnp.max(s, axis=1, keepdims=True))
            alpha = jnp.exp2(m_prev - m_next)
            p = jnp.exp2(s - jnp.concatenate([m_next] * (tk // LANES), axis=1))
            v_aug = jnp.concatenate([v_ref[0, :, hs], ones], axis=1)
            acc_scr[h] = jnp.concatenate([alpha, alpha], axis=1) * acc_scr[h] + _dot(p.astype(BF16), v_aug)
            m_scr[h] = m_next

    @pl.when(j < i)
    def _():
        step(False)

    @pl.when(j == i)
    def _():
        step(True)
        for h in range(n_heads):
            hs = slice(h * HEAD_B, (h + 1) * HEAD_B)
            acc = acc_scr[h]
            o_ref[0, :, hs] = (acc[:, :HEAD_B] / acc[:, HEAD_B:]).astype(o_ref.dtype)


def _flash_attention(q, q_aug, k, k_aug, v):
    bsz, t, d = q.shape
    h = d // HEAD_B
    tq = min(512, t)
    nq = t // tq
    q_idx = lambda b, i, j: (b, i, 0)
    kv_idx = lambda b, i, j: (b, jnp.minimum(j, i), 0)
    return pl.pallas_call(
        _flash_kernel,
        grid=(bsz, nq, nq),
        in_specs=[pl.BlockSpec((1, tq, d), q_idx), pl.BlockSpec((1, tq, d), q_idx),
                  pl.BlockSpec((1, tq, d), kv_idx), pl.BlockSpec((1, tq, d), kv_idx),
                  pl.BlockSpec((1, tq, d), kv_idx)],
        out_specs=pl.BlockSpec((1, tq, d), q_idx),
        out_shape=jax.ShapeDtypeStruct((bsz, t, d), BF16),
        scratch_shapes=[pltpu.VMEM((h, tq, LANES), F32), pltpu.VMEM((h, tq, 2 * HEAD_B), F32)],
        compiler_params=_cparams("parallel", "parallel", "arbitrary"),
        name="fox_prompt_attention",
    )(q, q_aug, k, k_aug, v)


def _bias_lanes(hi, mid, lo):
    bsz, h, t = hi.shape
    one = jnp.ones_like(hi)

    def lanes(parts):
        x = jnp.stack(parts, axis=-1)
        x = jnp.pad(jnp.swapaxes(x, 1, 2), ((0, 0), (0, 0), (0, 0), (0, HEAD_B - len(parts))))
        return x.reshape(bsz, t, h * HEAD_B)

    return lanes([hi, mid, lo, one, one, one]), lanes([one, one, one, -hi, -mid, -lo])


PAGES_PER_STEP = 4
SLAB = 8


def _page_suffix_kernel(lf_ref, suf_ref, tot_ref):
    pb, h, page = lf_ref.shape
    u = lax.broadcasted_iota(jnp.int32, (page, page), 0)
    s = lax.broadcasted_iota(jnp.int32, (page, page), 1)
    lf = lf_ref[...].reshape(pb * h, page) * LOG2E
    suf_ref[...] = _dot_l(lf, (u > s).astype(BF16), 3).reshape(pb, h, page)
    tot_ref[...] = _dot_l(lf, jnp.ones((page, page), BF16), 3).reshape(pb, h, page)


def _page_suffix(lf_t):
    n_phys, h, page = lf_t.shape
    pb = math.gcd(n_phys, 64)
    spec = pl.BlockSpec((pb, h, page), lambda i: (i, 0, 0))
    return pl.pallas_call(
        _page_suffix_kernel,
        grid=(n_phys // pb,),
        in_specs=[spec],
        out_specs=[spec, spec],
        out_shape=[jax.ShapeDtypeStruct(lf_t.shape, F32)] * 2,
        compiler_params=_cparams("parallel"),
        name="page_logf_suffix",
    )(lf_t)


def _paged_kernel(pt_ref, q_ref, kn_ref, vn_ref, gn_ref, *refs, n_group, n_slab):
    del pt_ref
    n_kv = n_group * n_slab
    kp_refs, vp_refs = refs[:n_kv], refs[n_kv:2 * n_kv]
    suf_refs, tot_refs = refs[2 * n_kv:3 * n_kv], refs[3 * n_kv:4 * n_kv]
    o_ref, m_scr, l_scr, acc_scr, cq_scr, carry_scr = refs[4 * n_kv:]
    p = pl.program_id(1)
    rows = q_ref.shape[1]
    n_heads = kn_ref.shape[1]
    nq = rows // n_heads
    page = kn_ref.shape[2]
    srows = SLAB * nq
    flat = page * SLAB

    def update(sl, s, pv_of):
        m_old = m_scr[sl]
        m_new = jnp.maximum(m_old, jnp.max(s, axis=1, keepdims=True))
        alpha = jnp.exp2(m_old - m_new)
        pr = jnp.exp2(s - m_new)
        l_scr[sl] = alpha * l_scr[sl] + jnp.sum(pr, axis=1, keepdims=True)
        acc_scr[sl] = alpha * acc_scr[sl] + pv_of(pr)
        m_scr[sl] = m_new

    @pl.when(p == 0)
    def _():
        m_scr[...] = jnp.full_like(m_scr, NEG_BIG)
        l_scr[...] = jnp.zeros_like(l_scr)
        acc_scr[...] = jnp.zeros_like(acc_scr)
        carry_scr[...] = jnp.zeros_like(carry_scr)
        row = lax.broadcasted_iota(jnp.int32, (rows, page), 0)
        lane = lax.broadcasted_iota(jnp.int32, (rows, page), 1)
        u = lax.broadcasted_iota(jnp.int32, (page, page), 0)
        s_ = lax.broadcasted_iota(jnp.int32, (page, page), 1)
        gn = gn_ref[0] * LOG2E
        qi = row % nq
        c_new_col = jnp.sum(jnp.where(lane <= qi, gn, 0.0), axis=1, keepdims=True)
        hi, mid, lo = [x.astype(F32) for x in _split(c_new_col, 3)]
        lane_a = lax.broadcasted_iota(jnp.int32, (rows, LANES), 1)
        cq_scr[...] = jnp.where(lane_a == 0, hi, jnp.where(lane_a == 1, mid,
                                                           jnp.where(lane_a == 2, lo, 0.0))).astype(BF16)
        c_new_row = _dot_l(gn, (u <= s_).astype(BF16), 3)
        s = jnp.concatenate([_dot_nt(q_ref[0, h * nq:(h + 1) * nq, :], kn_ref[0, h])
                             for h in range(n_heads)], axis=0) + (c_new_col - c_new_row)
        s = jnp.where(lane <= qi, s, NEG_BIG)
        for c in range(n_slab):
            sl = slice(c * srows, (c + 1) * srows)
            update(sl, s[sl], lambda pr, c=c: jnp.concatenate(
                [_dot(pr[hh * nq:(hh + 1) * nq, :].astype(BF16), vn_ref[0, c * SLAB + hh])
                 for hh in range(SLAB)], axis=0))

    @pl.when(p > 0)
    def _():
        row_f = lax.broadcasted_iota(jnp.int32, (srows, flat), 0)
        lane_f = lax.broadcasted_iota(jnp.int32, (srows, flat), 1)
        same_head = (row_f // nq) == (lane_f % SLAB)
        lane_k = lax.broadcasted_iota(jnp.int32, (flat, LANES), 1)
        ones_k = (lane_k < 3).astype(BF16)
        for c in range(n_slab):
            sl = slice(c * srows, (c + 1) * srows)
            q_aug = jnp.concatenate([q_ref[0, sl, :], cq_scr[sl, :]], axis=1)
            carry = carry_scr[c:c + 1, :]
            s = []
            for g in range(n_group):
                k_rows = kp_refs[g * n_slab + c][...].reshape(flat, HEAD_B).astype(BF16)
                z = _dot_nt(q_aug, jnp.concatenate([k_rows, ones_k], axis=1))
                s.append(jnp.where(same_head, z + (suf_refs[g * n_slab + c][...] + carry), NEG_BIG))
                carry = carry + tot_refs[g * n_slab + c][...]
            carry_scr[c:c + 1, :] = carry
            v_rows = jnp.concatenate([vp_refs[g * n_slab + c][...].reshape(flat, HEAD_B)
                                      for g in range(n_group)], axis=0).astype(BF16)
            update(sl, jnp.concatenate(s, axis=1), lambda pr: _dot(pr.astype(BF16), v_rows))

    @pl.when(p == pl.num_programs(1) - 1)
    def _():
        o_ref[0] = acc_scr[...] / l_scr[...]


def _paged_attention(page_table, q, k_new, v_new, g_new, cache_k, cache_v, suffix, total):
    dbs, rows, _ = q.shape
    n_heads = k_new.shape[1]
    n_slab = n_heads // SLAB
    n_pages = page_table.shape[1]
    page = k_new.shape[2]
    flat = page * SLAB
    n_group = math.gcd(PAGES_PER_STEP, n_pages)
    per_b = lambda b, p, pt: (b, 0, 0)
    per_b4 = lambda b, p, pt: (b, 0, 0, 0)

    def slot_page(b, p, pt, g):
        return pt[b, n_pages - 1 - ((jnp.maximum(p, 1) - 1) * n_group + g)]

    slots = [(g, c) for g in range(n_group) for c in range(n_slab)]
    kv_specs = [pl.BlockSpec((None, page, None, SLAB, HEAD_B),
                             lambda b, p, pt, g=g, c=c: (slot_page(b, p, pt, g), 0, c, 0, 0)) for g, c in slots]
    row_specs = [pl.BlockSpec((None, None, 1, flat),
                              lambda b, p, pt, g=g, c=c: (slot_page(b, p, pt, g), c, 0, 0)) for g, c in slots]
    grid_spec = pltpu.PrefetchScalarGridSpec(
        num_scalar_prefetch=1,
        grid=(dbs, n_pages // n_group + 1),
        in_specs=[pl.BlockSpec((1, rows, HEAD_B), per_b),
                  pl.BlockSpec((1, n_heads, page, HEAD_B), per_b4),
                  pl.BlockSpec((1, n_heads, page, HEAD_B), per_b4),
                  pl.BlockSpec((1, rows, page), per_b)]
                 + kv_specs + kv_specs + row_specs + row_specs,
        out_specs=pl.BlockSpec((1, rows, HEAD_B), per_b),
        scratch_shapes=[pltpu.VMEM((rows, 1), F32), pltpu.VMEM((rows, 1), F32),
                        pltpu.VMEM((rows, HEAD_B), F32), pltpu.VMEM((rows, LANES), BF16),
                        pltpu.VMEM((8, flat), F32)],
    )
    n_kv = len(slots)
    return pl.pallas_call(
        functools.partial(_paged_kernel, n_group=n_group, n_slab=n_slab),
        grid_spec=grid_spec,
        out_shape=jax.ShapeDtypeStruct((dbs, rows, HEAD_B), F32),
        compiler_params=_cparams("parallel", "arbitrary"),
        name="fox_paged_attention",
    )(page_table, q, k_new, v_new, g_new, *([cache_k] * n_kv), *([cache_v] * n_kv),
      *([suffix] * n_kv), *([total] * n_kv))


def _trunk(x, shift0, wkv0, past, p):
    bsz, t, d = x.shape
    m = bsz * t
    n_a = p["tm_w_r"].shape[0]
    depth = p["ln1"].shape[0]
    h_b = d // HEAD_B
    h = x.reshape(m, d)
    v_first = None
    shifts, states = [], []
    t_scan = -(-t // SCAN_CHUNK) * SCAN_CHUNK
    heads_first = lambda z: jnp.swapaxes(z.reshape(bsz, t, h_b, HEAD_B), 1, 2)
    for l in range(depth):
        if l == n_a:
            k_new, k_bf, v_new, v_bf = _norm_matmul(h, p["kv_norm"], [p["fox_w_k"], p["fox_w_v"]], (F32, BF16))
            (lf_pad,) = _norm_matmul(h, p["kv_norm"], [p["fox_w_f"]], (F32,), epilogue="log_sigmoid",
                                     bias=p["fox_b_f"])
            logf_new = lf_pad[:, :h_b].reshape(bsz, t, h_b)
            if past is None:
                q_aug, k_aug = _bias_lanes(*_cumsum_rows_log2(jnp.swapaxes(logf_new, 1, 2)))
            else:
                page = past[0].shape[1]
                pad_keys = lambda z: jnp.pad(heads_first(z), ((0, 0), (0, 0), (0, page - t), (0, 0)))
                k_pad, v_pad = pad_keys(k_bf), pad_keys(v_bf)
                g_new = jnp.repeat(jnp.swapaxes(logf_new, 1, 2), t, axis=1)
                g_new = jnp.pad(g_new, ((0, 0), (0, 0), (0, page - t)))
        if l < n_a:
            shifts.append(_rmsnorm(h.reshape(bsz, t, d)[:, -1], p["ln1"][l]))
            has_v = l > 0
            lv = max(l - 1, 0)
            r, lw, kr, v, a, g = _mix(
                h, shift0[l], t, p["ln1"][l], v_first if has_v else h, p["mu8"][l],
                p["tm_w_r"][l], p["tm_w_k"][l], p["tm_w_v"][l],
                p["tm_w1"][l], p["tm_w2"][l], p["tm_a1"][l], p["tm_a2"][l],
                p["tm_v1"][lv], p["tm_v2"][lv], p["tm_g1"][l], p["tm_g2"][l],
                p["tm_w0"][l].reshape(1, d), p["tm_a0"][l].reshape(1, d), p["tm_v0"][lv].reshape(1, d),
                has_v)
            if l == 0:
                v_first = v
            seq = lambda z: jnp.pad(z.reshape(bsz, t, d), ((0, 0), (0, t_scan - t), (0, 0)))
            y, s_bd = _scan(seq(r), seq(lw), seq(kr), seq(v), seq(a), seq(g),
                            p["tm_k_k"][l], p["tm_k_a"][l], p["tm_r_k"][l].reshape(d),
                            p["tm_lnx_w"][l], p["tm_lnx_b"][l], _state_to_bd(wkv0[l]))
            states.append(_bd_to_state(s_bd))
            h = _matmul_residual(y[:, :t].reshape(m, d), p["tm_w_o"][l], h)
        else:
            j = l - n_a
            (q,) = _norm_matmul(h, p["ln1"][l], [p["fox_w_q"][j]], (BF16,), epilogue="scale",
                                scale=HEAD_B ** -0.5 * LOG2E)
            if past is None:
                att = _flash_attention(q.reshape(bsz, t, d), q_aug, k_bf.reshape(bsz, t, d), k_aug,
                                       v_bf.reshape(bsz, t, d)).reshape(m, d)
            else:
                cache_k, cache_v, suffix, total, page_table = past
                o = _paged_attention(page_table, heads_first(q).reshape(bsz, h_b * t, HEAD_B),
                                     k_pad, v_pad, g_new, cache_k, cache_v, suffix, total)
                att = jnp.swapaxes(o.reshape(bsz, h_b, t, HEAD_B), 1, 2).reshape(m, d).astype(BF16)
            h = _matmul_residual(att, p["fox_w_o"][j], h)
        h = _ffn(h, p["ln2"][l], p["ffn_w1"][l], p["ffn_w3"][l], p["ffn_w2"][l])
    y = _rmsnorm(h, p["ln_out"])
    h_sh = (bsz, t, h_b, HEAD_B)
    return (y.reshape(bsz, t, d), jnp.stack(shifts), jnp.stack(states),
            k_new.reshape(h_sh), v_new.reshape(h_sh), logf_new)


def kernel(x_prompt, x_sample, state_wkv, state_shift, cache_k, cache_v, cache_logf, page_table, ln1, ln2, ln_out, tm_mu, tm_w_r, tm_w_k, tm_w_v, tm_w_o, tm_w0, tm_w1, tm_w2, tm_a0, tm_a1, tm_a2, tm_v0, tm_v1, tm_v2, tm_g1, tm_g2, tm_k_k, tm_k_a, tm_r_k, tm_lnx_w, tm_lnx_b, kv_norm, fox_w_k, fox_w_v, fox_w_f, fox_b_f, fox_w_q, fox_w_o, ffn_w1, ffn_w3, ffn_w2):
    bf = lambda w: w.astype(BF16)
    n_a, d = tm_w0.shape
    h_b = d // HEAD_B
    lora = lambda w1, w2: (bf(jnp.pad(w1, ((0, 0), (0, 0), (0, -w1.shape[2] % LANES)))),
                           bf(jnp.pad(w2, ((0, 0), (0, -w2.shape[1] % LANES), (0, 0)))))
    w1, w2 = lora(tm_w1, tm_w2)
    a1, a2 = lora(tm_a1, tm_a2)
    if tm_v1.shape[0] == 0:
        tm_v0 = jnp.zeros((1, d), F32)
        tm_v1 = jnp.zeros((1, d, LANES), F32)
        tm_v2 = jnp.zeros((1, LANES, d), F32)
    v1, v2 = lora(tm_v1, tm_v2)
    g1, g2 = lora(tm_g1, tm_g2)
    p = dict(ln1=ln1, ln2=ln2, ln_out=ln_out,
             mu8=jnp.pad(tm_mu, ((0, 0), (0, 2), (0, 0))),
             tm_w_r=bf(tm_w_r), tm_w_k=bf(tm_w_k), tm_w_v=bf(tm_w_v), tm_w_o=bf(tm_w_o),
             tm_w0=tm_w0, tm_w1=w1, tm_w2=w2, tm_a0=tm_a0, tm_a1=a1, tm_a2=a2,
             tm_v0=tm_v0, tm_v1=v1, tm_v2=v2, tm_g1=g1, tm_g2=g2,
             tm_k_k=tm_k_k, tm_k_a=tm_k_a, tm_r_k=tm_r_k, tm_lnx_w=tm_lnx_w, tm_lnx_b=tm_lnx_b,
             kv_norm=kv_norm, fox_w_k=bf(fox_w_k), fox_w_v=bf(fox_w_v),
             fox_w_f=bf(jnp.pad(fox_w_f, ((0, 0), (0, LANES - h_b)))),
             fox_b_f=jnp.pad(fox_b_f, (0, LANES - h_b)).reshape(1, LANES),
             fox_w_q=bf(fox_w_q), fox_w_o=bf(fox_w_o),
             ffn_w1=bf(ffn_w1), ffn_w3=bf(ffn_w3), ffn_w2=bf(ffn_w2))

    bsz = x_prompt.shape[0]
    h_a = d // HEAD_A
    shift0 = jnp.zeros((n_a, bsz, d), x_prompt.dtype)
    wkv0 = jnp.zeros((n_a, bsz, h_a, HEAD_A, HEAD_A), F32)
    y_p, sh_p, wkv_p, k_p, v_p, lf_p = _trunk(x_prompt, shift0, wkv0, None, p)

    n_phys, page = cache_k.shape[:2]
    dec_t = x_sample.shape[1]
    n_slab = h_b // SLAB
    slabs = lambda c: c.reshape(n_phys, page, n_slab, SLAB, HEAD_B)
    key_head_rows = lambda z: jnp.transpose(z.reshape(n_phys, n_slab, SLAB, page), (0, 1, 3, 2)).reshape(
        n_phys, n_slab, 1, page * SLAB)
    suffix, total = _page_suffix(jnp.swapaxes(cache_logf, 1, 2))
    past = (slabs(cache_k), slabs(cache_v), key_head_rows(suffix), key_head_rows(total), page_table)
    y_s, sh_s, wkv_s, k_s, v_s, lf_s = _trunk(x_sample, state_shift, state_wkv, past, p)
    return (y_p, y_s, wkv_p, sh_p, k_p, v_p, lf_p, wkv_s, sh_s, k_s, v_s, lf_s)
```

```python
import functools
import math

import jax
import jax.numpy as jnp
from jax import lax
from jax.experimental import pallas as pl
from jax.experimental.pallas import tpu as pltpu

F32 = jnp.float32
BF16 = jnp.bfloat16

HEAD_A = 64
HEAD_B = 128
LANES = 128
RMS_EPS = 1e-6
LNX_EPS = 64e-5
NEG_BIG = -1e30
LOG2E = math.log2(math.e)
VMEM_LIMIT = 56 * 1024 * 1024


def _cparams(*sem):
    return pltpu.CompilerParams(dimension_semantics=sem, vmem_limit_bytes=VMEM_LIMIT)


def _dot(a, b):
    return jnp.dot(a, b, preferred_element_type=F32)


def _dot_nt(a, b):
    return lax.dot_general(a, b, (((1,), (1,)), ((), ())), preferred_element_type=F32)


def _split(x, n):
    pieces = []
    for _ in range(n - 1):
        hi = x.astype(BF16)
        pieces.append(hi)
        x = x - hi.astype(F32)
    pieces.append(x.astype(BF16))
    return pieces


def _dot_l(x, w, n):
    return sum(_dot(piece, w) for piece in _split(x, n))


def _dot_l_stacked(x, w_stacked, n):
    return _dot(jnp.concatenate(_split(x, n), axis=1), w_stacked)


def _rms(x, g):
    return x * lax.rsqrt(jnp.mean(x * x, axis=-1, keepdims=True) + RMS_EPS) * g


def _softplus(z):
    return jnp.maximum(z, 0.0) + jnp.log1p(jnp.exp(-jnp.abs(z)))


def _row_tile(m):
    return min(512, m)


def _layer_spec(w, layer, block, index):
    if w.ndim == 2:
        return pl.BlockSpec(block, index)
    return pl.BlockSpec((None,) + block, lambda *args: (layer,) + index(*args))


def _rmsnorm_kernel(x_ref, g_ref, o_ref):
    o_ref[...] = _rms(x_ref[...], g_ref[...])


def _rmsnorm(x, g):
    m, d = x.shape
    tm = _row_tile(m)
    return pl.pallas_call(
        _rmsnorm_kernel,
        grid=(m // tm,),
        in_specs=[pl.BlockSpec((tm, d), lambda i: (i, 0)), pl.BlockSpec((1, d), lambda i: (0, 0))],
        out_specs=pl.BlockSpec((tm, d), lambda i: (i, 0)),
        out_shape=jax.ShapeDtypeStruct((m, d), F32),
        compiler_params=_cparams("parallel"),
        name="rmsnorm",
    )(x, g.reshape(1, d))


def _norm_matmul_kernel(x_ref, g_ref, b_ref, *refs, n_w, n_out, epilogue, scale):
    w_refs, o_refs, xn_scr = refs[:n_w], refs[n_w:-1], refs[-1]

    @pl.when(pl.program_id(1) == 0)
    def _():
        xn_scr[...] = _rms(x_ref[...], g_ref[...]).astype(BF16)

    for wi, w_ref in enumerate(w_refs):
        acc = _dot(xn_scr[...], w_ref[...])
        if epilogue == "log_sigmoid":
            acc = -_softplus(-(acc + b_ref[...]))
        elif epilogue == "scale":
            acc = acc * scale
        for o_ref in o_refs[wi * n_out:(wi + 1) * n_out]:
            o_ref[...] = acc.astype(o_ref.dtype)


def _norm_matmul(x, g, ws, out_dtypes, epilogue="none", bias=None, scale=1.0, layer=None):
    m, d = x.shape
    n = ws[0].shape[-1]
    tm, tn = min(1024, m), min(512, n)
    if bias is None:
        bias = jnp.zeros((1, n), F32)
    n_res = len(ws) * len(out_dtypes)
    outs = pl.pallas_call(
        functools.partial(_norm_matmul_kernel, n_w=len(ws), n_out=len(out_dtypes), epilogue=epilogue,
                          scale=scale),
        grid=(m // tm, n // tn),
        in_specs=[pl.BlockSpec((tm, d), lambda i, j: (i, 0)),
                  pl.BlockSpec((1, d), lambda i, j: (0, 0)),
                  pl.BlockSpec((1, tn), lambda i, j: (0, j))]
                 + [_layer_spec(w, layer, (d, tn), lambda i, j: (0, j)) for w in ws],
        out_specs=[pl.BlockSpec((tm, tn), lambda i, j: (i, j)) for _ in range(n_res)],
        out_shape=[jax.ShapeDtypeStruct((m, n), dt) for _ in ws for dt in out_dtypes],
        scratch_shapes=[pltpu.VMEM((tm, d), BF16)],
        compiler_params=_cparams("parallel", "arbitrary"),
        name="norm_matmul_" + epilogue,
    )(x, g.reshape(1, d), bias, *ws)
    return outs


def _matmul_residual_kernel(x_ref, w_ref, h_ref, o_ref):
    o_ref[...] = h_ref[...] + _dot(x_ref[...], w_ref[...])


def _matmul_residual(x, w, h, layer):
    m, k = x.shape
    n = w.shape[-1]
    tm, tn = min(1024, m), min(512, n)
    return pl.pallas_call(
        _matmul_residual_kernel,
        grid=(m // tm, n // tn),
        in_specs=[pl.BlockSpec((tm, k), lambda i, j: (i, 0)),
                  _layer_spec(w, layer, (k, tn), lambda i, j: (0, j)),
                  pl.BlockSpec((tm, tn), lambda i, j: (i, j))],
        out_specs=pl.BlockSpec((tm, tn), lambda i, j: (i, j)),
        out_shape=jax.ShapeDtypeStruct((m, n), F32),
        compiler_params=_cparams("parallel", "parallel"),
        name="matmul_residual",
    )(x, w, h)


def _ffn_kernel(h_ref, g_ref, w1_ref, w3_ref, w2_ref, o_ref, xn_scr, acc_scr):
    j = pl.program_id(1)

    @pl.when(j == 0)
    def _():
        xn_scr[...] = _rms(h_ref[...], g_ref[...]).astype(BF16)
        acc_scr[...] = jnp.zeros_like(acc_scr)

    x = xn_scr[...]
    a = _dot(x, w1_ref[...])
    b = _dot(x, w3_ref[...])
    hm = (a * jax.nn.sigmoid(a) * b).astype(BF16)
    acc_scr[...] += _dot(hm, w2_ref[...])

    @pl.when(j == pl.num_programs(1) - 1)
    def _():
        o_ref[...] = h_ref[...] + acc_scr[...]


def _ffn(h, g, w1, w3, w2, layer):
    m, d = h.shape
    f = w1.shape[-1]
    tm, tf = _row_tile(m), 512
    return pl.pallas_call(
        _ffn_kernel,
        grid=(m // tm, f // tf),
        in_specs=[pl.BlockSpec((tm, d), lambda i, j: (i, 0)),
                  pl.BlockSpec((1, d), lambda i, j: (0, 0)),
                  _layer_spec(w1, layer, (d, tf), lambda i, j: (0, j)),
                  _layer_spec(w3, layer, (d, tf), lambda i, j: (0, j)),
                  _layer_spec(w2, layer, (tf, d), lambda i, j: (j, 0))],
        out_specs=pl.BlockSpec((tm, d), lambda i, j: (i, 0)),
        out_shape=jax.ShapeDtypeStruct((m, d), F32),
        scratch_shapes=[pltpu.VMEM((tm, d), BF16), pltpu.VMEM((tm, d), F32)],
        compiler_params=_cparams("parallel", "arbitrary"),
        name="ffn",
    )(h, g.reshape(1, d), w1, w3, w2)


def _mix_kernel(h_ref, hp_ref, st_ref, ln_ref, vf_ref, mu_ref, wr_ref, wk_ref, wv_ref,
                w1_ref, w2_ref, a1_ref, a2_ref, v1_ref, v2_ref, g1_ref, g2_ref,
                w0_ref, a0_ref, v0_ref,
                r_ref, lw_ref, k_ref, v_ref, a_ref, g_ref,
                xr_scr, xk_scr, xv_scr, hw_scr, ha_scr, hv_scr, hg_scr, *, has_v, seq_len):
    @pl.when(pl.program_id(1) == 0)
    def _():
        tm = h_ref.shape[0]
        xn = _rms(h_ref[...], ln_ref[...])
        row = lax.broadcasted_iota(jnp.int32, (tm, 1), 0)
        before_tile = _rms(hp_ref[7:8, :], ln_ref[...])
        xprev = jnp.where(row == 0, before_tile, pltpu.roll(xn, 1, axis=0))
        if seq_len >= tm:
            starts_here = pl.program_id(0) % (seq_len // tm) == 0
            xprev = jnp.where((row == 0) & starts_here, st_ref[0, 0:1, :], xprev)
        else:
            for s in range(tm // seq_len):
                xprev = jnp.where(row == s * seq_len, st_ref[0, s:s + 1, :], xprev)
        xx = xprev - xn
        mix = lambda i: (xn + xx * mu_ref[i:i + 1, :]).astype(BF16)
        xr_scr[...] = mix(0)
        hw_scr[...] = jnp.tanh(_dot(mix(1), w1_ref[...])).astype(BF16)
        xk_scr[...] = mix(2)
        xv = mix(3)
        xv_scr[...] = xv
        if has_v:
            hv_scr[...] = _dot(xv, v1_ref[...]).astype(BF16)
        ha_scr[...] = _dot(mix(4), a1_ref[...]).astype(BF16)
        hg_scr[...] = jax.nn.sigmoid(_dot(mix(5), g1_ref[...])).astype(BF16)

    r_ref[...] = _dot(xr_scr[...], wr_ref[...])
    k_ref[...] = _dot(xk_scr[...], wk_ref[...])
    v = _dot(xv_scr[...], wv_ref[...])
    if has_v:
        v = v + (vf_ref[...] - v) * jax.nn.sigmoid(v0_ref[...] + _dot(hv_scr[...], v2_ref[...]))
    v_ref[...] = v
    w_log = -_softplus(-(w0_ref[...] + _dot(hw_scr[...], w2_ref[...]))) - 0.5
    lw_ref[...] = -jnp.exp(w_log)
    a_ref[...] = jax.nn.sigmoid(a0_ref[...] + _dot(ha_scr[...], a2_ref[...]))
    g_ref[...] = _dot(hg_scr[...], g2_ref[...])


def _mix(h, shift_state, seq_len, ln, vfirst, mu, wr, wk, wv, w1, w2, a1, a2, v1, v2, g1, g2, w0, a0, v0,
         has_v, layer, layer_v):
    m, d = h.shape
    tm, tn = _row_tile(m), 256
    assert seq_len % tm == 0 or tm % seq_len == 0
    n_starts = max(1, tm // seq_len)
    seq_of = [[(i * tm + s * seq_len) // seq_len for s in range(n_starts)] for i in range(m // tm)]
    starts = shift_state[jnp.asarray(seq_of)]
    lw_, la_, lv_, lg_ = w1.shape[-1], a1.shape[-1], v1.shape[-1], g1.shape[-1]
    row = lambda i, j: (i, 0)
    col = lambda i, j: (0, j)
    tile = lambda i, j: (i, j)
    fixed = lambda i, j: (0, 0)
    down = lambda w, lyr: _layer_spec(w, lyr, (d, w.shape[-1]), fixed)
    up = lambda w, lyr: _layer_spec(w, lyr, (w.shape[-2], tn), col)
    outs = pl.pallas_call(
        functools.partial(_mix_kernel, has_v=has_v, seq_len=seq_len),
        grid=(m // tm, d // tn),
        in_specs=[pl.BlockSpec((tm, d), row),
                  pl.BlockSpec((8, d), lambda i, j: (jnp.maximum(i * (tm // 8) - 1, 0), 0)),
                  pl.BlockSpec((1, n_starts, d), lambda i, j: (i, 0, 0)),
                  pl.BlockSpec((1, d), fixed),
                  pl.BlockSpec((tm, tn), tile),
                  pl.BlockSpec((8, d), fixed),
                  up(wr, layer), up(wk, layer), up(wv, layer),
                  down(w1, layer), up(w2, layer), down(a1, layer), up(a2, layer),
                  down(v1, layer_v), up(v2, layer_v), down(g1, layer), up(g2, layer),
                  pl.BlockSpec((1, tn), col), pl.BlockSpec((1, tn), col), pl.BlockSpec((1, tn), col)],
        out_specs=[pl.BlockSpec((tm, tn), tile) for _ in range(6)],
        out_shape=[jax.ShapeDtypeStruct((m, d), F32) for _ in range(6)],
        scratch_shapes=[pltpu.VMEM((tm, d), BF16), pltpu.VMEM((tm, d), BF16), pltpu.VMEM((tm, d), BF16),
                        pltpu.VMEM((tm, lw_), BF16), pltpu.VMEM((tm, la_), BF16),
                        pltpu.VMEM((tm, lv_), BF16), pltpu.VMEM((tm, lg_), BF16)],
        compiler_params=_cparams("parallel", "arbitrary"),
        name="rwkv_mix_proj",
    )(h, h, starts, ln.reshape(1, d), vfirst, mu, wr, wk, wv, w1, w2, a1, a2, v1, v2, g1, g2, w0, a0, v0)
    return outs


SCAN_CHUNK = 64
SCAN_PAIRS = 16


def _scan_kernel(r_ref, lw_ref, kr_ref, v_ref, a_ref, g_ref,
                 kk_ref, ka_ref, rk_ref, lnw_ref, lnb_ref, s0_ref,
                 y_ref, sout_ref, st_scr):
    c = pl.program_id(2)
    C = SCAN_CHUNK
    R = 2 * C
    n_pairs = st_scr.shape[0]

    @pl.when(c == 0)
    def _():
        st_scr[...] = s0_ref[0]

    lane_c = lax.broadcasted_iota(jnp.int32, (C, LANES), 1)
    head0 = lane_c < HEAD_A
    row = lax.broadcasted_iota(jnp.int32, (R, R), 0)
    col = lax.broadcasted_iota(jnp.int32, (R, R), 1)
    same = (row // C) == (col // C)
    strict = same & ((col % C) < (row % C))
    incl = same & ((col % C) <= (row % C))
    row2 = lax.broadcasted_iota(jnp.int32, (2 * LANES, LANES), 0)
    col2 = lax.broadcasted_iota(jnp.int32, (2 * LANES, LANES), 1)
    ones2 = (((row2 % LANES) // HEAD_A) == (col2 // HEAD_A)).astype(BF16)
    head_sum = lambda x: _dot_l_stacked(x, ones2, 2)
    ti = lax.broadcasted_iota(jnp.int32, (C, 3 * C), 0)
    tj = lax.broadcasted_iota(jnp.int32, (C, 3 * C), 1)
    tri3 = ((tj % C) <= ti).astype(BF16)
    zeros_c = jnp.zeros((C, LANES), F32)

    def stack(x):
        return jnp.concatenate([jnp.where(head0, x, 0.0), jnp.where(head0, 0.0, x)], axis=0)

    sls = [slice(p * LANES, (p + 1) * LANES) for p in range(n_pairs)]
    each = lambda f, *cols: [f(*xs) for xs in zip(*cols)]
    bf = lambda xs: [x.astype(BF16) for x in xs]

    r = [r_ref[0, :, sl] for sl in sls]
    lw = [lw_ref[0, :, sl] for sl in sls]
    kr = [kr_ref[0, :, sl] for sl in sls]
    v = [v_ref[0, :, sl] for sl in sls]
    a = [a_ref[0, :, sl] for sl in sls]

    kkr = each(lambda x, sl: x * kk_ref[:, sl], kr, sls)
    ss = each(lambda x: head_sum(x * x), kkr)
    kk = each(lambda x, s: x / jnp.maximum(jnp.sqrt(s), 1e-12), kkr, ss)
    k = each(lambda x, y, sl: x * (1.0 + (y - 1.0) * ka_ref[:, sl]), kr, a, sls)
    b = each(lambda x, y: x * y, kk, a)

    cum = each(lambda x: _dot(tri3, jnp.concatenate(_split(x, 3), axis=0)), lw)
    g_in = each(jnp.exp, cum)
    g_ex = each(lambda x, y: jnp.exp(x - y), cum, lw)
    g_inv = each(lambda x: jnp.exp(-x), cum)
    g_tail = each(lambda x: jnp.exp(x[C - 1:C, :] - x), cum)

    at = each(lambda x, y: stack(-x * y), kk, g_ex)
    at_b = bf(at)
    rt = bf(each(lambda x, y: stack(x * y), r, g_in))
    bt = bf(each(lambda x, y: stack(x * y), b, g_inv))
    kt = bf(each(lambda x, y: stack(x * y), k, g_inv))
    bh = each(lambda x, y: stack(x * y), b, g_tail)
    kh = each(lambda x, y: stack(x * y), k, g_tail)
    vs = bf(each(stack, v))

    sc = each(lambda w, x, y, z: _dot_nt(jnp.concatenate([w, x], axis=0), jnp.concatenate([y, z], axis=0)),
              at_b, rt, bt, kt)
    a_ab = each(lambda x: jnp.where(strict, x[:R, :R], 0.0), sc)
    a_ak = bf(each(lambda x: jnp.where(strict, x[:R, R:], 0.0), sc))
    a_rb = bf(each(lambda x: jnp.where(incl, x[R:, :R], 0.0), sc))
    a_rk = bf(each(lambda x: jnp.where(incl, x[R:, R:], 0.0), sc))

    x = each(lambda w, y, z: jnp.concatenate([w, _dot(y, z)], axis=1), at, a_ak, vs)
    pw = a_ab
    n_sq = int(math.log2(C))
    for it in range(n_sq):
        pw_b = bf(pw)
        x = each(lambda y, z: y + _dot(z, y.astype(BF16)), x, pw_b)
        if it + 1 < n_sq:
            pw = each(lambda z: _dot(z, z), pw_b)
    a_hat = bf(each(lambda y: y[:, :LANES], x))
    u_bar = each(lambda y: y[:, LANES:], x)

    st = [st_scr[p] for p in range(n_pairs)]
    st_b = bf(st)
    u = each(lambda w, y, z: _dot(w, y) + z, a_hat, st_b, u_bar)
    u_b = bf(u)
    uv = each(lambda y, t: jnp.concatenate([y, t], axis=0), u_b, vs)
    o_st = each(lambda q, w, z, s, y: _dot(jnp.concatenate([q, w, z], axis=1), jnp.concatenate([s, y], axis=0)),
                rt, a_rb, a_rk, st_b, uv)
    o = each(lambda y: y[:C] + y[C:], o_st)

    def decay_col(x):
        x_t = jnp.concatenate([x, zeros_c], axis=0).T if C < LANES else x.T
        return jnp.exp(jnp.sum(x_t, axis=1, keepdims=True))

    dcol = each(decay_col, lw)
    st_new = each(lambda dc, s, w, z, y: dc * s + _dot(jnp.concatenate([w.T, z.T], axis=1).astype(BF16), y),
                  dcol, st, bh, kh, uv)
    for p in range(n_pairs):
        st_scr[p] = st_new[p]

    mean = each(lambda y: head_sum(y) * (1.0 / HEAD_A), o)
    dlt = each(lambda y, z: y - z, o, mean)
    var = each(lambda y: head_sum(y * y) * (1.0 / HEAD_A), dlt)
    on = each(lambda y, z, sl: y * lax.rsqrt(z + LNX_EPS) * lnw_ref[:, sl] + lnb_ref[:, sl], dlt, var, sls)
    bonus = each(lambda x, y, z, sl: head_sum(x * y * rk_ref[:, sl]) * z, r, k, v, sls)
    for p, sl in enumerate(sls):
        y_ref[0, :, sl] = ((on[p] + bonus[p]) * g_ref[0, :, sl]).astype(y_ref.dtype)

    @pl.when(c == pl.num_programs(2) - 1)
    def _():
        sout_ref[0] = st_scr[...]


def _scan(r, lw, kr, v, a, g, k_k, k_a, r_k, lnx_w, lnx_b, s0_bd):
    bsz, t, d = r.shape
    C = SCAN_CHUNK
    P = min(SCAN_PAIRS, d // LANES)
    w = P * LANES
    n_pg = d // w
    seq = pl.BlockSpec((1, C, w), lambda b, q, c: (b, c, q))
    vec = pl.BlockSpec((1, w), lambda b, q, c: (0, q))
    sbd = pl.BlockSpec((1, P, LANES, LANES), lambda b, q, c: (b, q, 0, 0))
    y, s_out = pl.pallas_call(
        _scan_kernel,
        grid=(bsz, n_pg, t // C),
        in_specs=[seq] * 6 + [vec] * 5 + [sbd],
        out_specs=[seq, sbd],
        out_shape=[jax.ShapeDtypeStruct((bsz, t, d), BF16),
                   jax.ShapeDtypeStruct(s0_bd.shape, F32)],
        scratch_shapes=[pltpu.VMEM((P, LANES, LANES), F32)],
        compiler_params=_cparams("parallel", "parallel", "arbitrary"),
        name="wkv7_scan",
    )(r, lw, kr, v, a, g, k_k.reshape(1, d), k_a.reshape(1, d), r_k.reshape(1, d),
      lnx_w.reshape(1, d), lnx_b.reshape(1, d), s0_bd)
    return y, s_out


def _state_to_bd(s):
    bsz, h = s.shape[:2]
    st = jnp.swapaxes(s, 2, 3).reshape(bsz, h // 2, 2, HEAD_A, HEAD_A)
    eye = jnp.eye(2, dtype=s.dtype)
    bd = st[:, :, :, :, None, :] * eye[None, None, :, None, :, None]
    return bd.reshape(bsz, h // 2, 2 * HEAD_A, 2 * HEAD_A)


def _bd_to_state(bd):
    bsz, hp = bd.shape[:2]
    x = bd.reshape(bsz, hp, 2, HEAD_A, 2, HEAD_A)
    st = jnp.stack([x[:, :, 0, :, 0, :], x[:, :, 1, :, 1, :]], axis=2)
    return jnp.swapaxes(st.reshape(bsz, hp * 2, HEAD_A, HEAD_A), 2, 3)


def _cumsum_kernel(x_ref, hi_ref, mid_ref, lo_ref, carry_scr):
    @pl.when(pl.program_id(1) == 0)
    def _():
        carry_scr[...] = jnp.zeros_like(carry_scr)

    tt = x_ref.shape[2]
    u = lax.broadcasted_iota(jnp.int32, (tt, tt), 0)
    s = lax.broadcasted_iota(jnp.int32, (tt, tt), 1)
    tri = (u <= s).astype(BF16)
    c = _dot_l(x_ref[0], tri, 3) + carry_scr[...]
    carry_scr[...] = c[:, tt - 1:tt]
    hi_ref[0], mid_ref[0], lo_ref[0] = _split(c * LOG2E, 3)


def _cumsum_rows_log2(x):
    bsz, h, t = x.shape
    tt = min(512, t)
    spec = pl.BlockSpec((1, h, tt), lambda b, i: (b, 0, i))
    return pl.pallas_call(
        _cumsum_kernel,
        grid=(bsz, t // tt),
        in_specs=[spec],
        out_specs=[spec] * 3,
        out_shape=[jax.ShapeDtypeStruct((bsz, h, t), BF16)] * 3,
        scratch_shapes=[pltpu.VMEM((h, 1), F32)],
        compiler_params=_cparams("parallel", "arbitrary"),
        name="logf_cumsum",
    )(x)


def _flash_kernel(q_ref, qa_ref, k_ref, ka_ref, v_ref, o_ref, m_scr, acc_scr):
    i, j = pl.program_id(1), pl.program_id(2)
    tq, tk = q_ref.shape[1], k_ref.shape[1]
    n_heads = q_ref.shape[2] // HEAD_B

    @pl.when(j == 0)
    def _():
        m_scr[...] = jnp.full_like(m_scr, NEG_BIG)
        acc_scr[...] = jnp.zeros_like(acc_scr)

    def step(masked):
        if masked:
            qpos = lax.broadcasted_iota(jnp.int32, (tq, tk), 0)
            kpos = lax.broadcasted_iota(jnp.int32, (tq, tk), 1)
            keep = kpos <= qpos
        ones = jnp.ones((tk, HEAD_B), BF16)
        for h in range(n_heads):
            hs = slice(h * HEAD_B, (h + 1) * HEAD_B)
            q_aug = jnp.concatenate([q_ref[0, :, hs], qa_ref[0, :, hs]], axis=1)
            k_aug = jnp.concatenate([k_ref[0, :, hs], ka_ref[0, :, hs]], axis=1)
            s = _dot_nt(q_aug, k_aug)
            if masked:
                s = jnp.where(keep, s, NEG_BIG)
            m_prev = m_scr[h]
            m_next = jnp.maximum(m_prev, jnp.max(s, axis=1, keepdims=True))
            alpha = jnp.exp2(m_prev - m_next)
            p = jnp.exp2(s - jnp.concatenate([m_next] * (tk // LANES), axis=1))
            v_aug = jnp.concatenate([v_ref[0, :, hs], ones], axis=1)
            acc_scr[h] = jnp.concatenate([alpha, alpha], axis=1) * acc_scr[h] + _dot(p.astype(BF16), v_aug)
            m_scr[h] = m_next

    @pl.when(j < i)
    def _():
        step(False)

    @pl.when(j == i)
    def _():
        step(True)
        for h in range(n_heads):
            hs = slice(h * HEAD_B, (h + 1) * HEAD_B)
            acc = acc_scr[h]
            o_ref[0, :, hs] = (acc[:, :HEAD_B] / acc[:, HEAD_B:]).astype(o_ref.dtype)


def _flash_attention(q, q_aug, k, k_aug, v):
    bsz, t, d = q.shape
    h = d // HEAD_B
    tq = min(512, t)
    nq = t // tq
    q_idx = lambda b, i, j: (b, i, 0)
    kv_idx = lambda b, i, j: (b, jnp.minimum(j, i), 0)
    return pl.pallas_call(
        _flash_kernel,
        grid=(bsz, nq, nq),
        in_specs=[pl.BlockSpec((1, tq, d), q_idx), pl.BlockSpec((1, tq, d), q_idx),
                  pl.BlockSpec((1, tq, d), kv_idx), pl.BlockSpec((1, tq, d), kv_idx),
                  pl.BlockSpec((1, tq, d), kv_idx)],
        out_specs=pl.BlockSpec((1, tq, d), q_idx),
        out_shape=jax.ShapeDtypeStruct((bsz, t, d), BF16),
        scratch_shapes=[pltpu.VMEM((h, tq, LANES), F32), pltpu.VMEM((h, tq, 2 * HEAD_B), F32)],
        compiler_params=_cparams("parallel", "parallel", "arbitrary"),
        name="fox_prompt_attention",
    )(q, q_aug, k, k_aug, v)


def _bias_lanes(hi, mid, lo):
    bsz, h, t = hi.shape
    one = jnp.ones_like(hi)

    def lanes(parts):
        x = jnp.stack(parts, axis=-1)
        x = jnp.pad(jnp.swapaxes(x, 1, 2), ((0, 0), (0, 0), (0, 0), (0, HEAD_B - len(parts))))
        return x.reshape(bsz, t, h * HEAD_B)

    return lanes([hi, mid, lo, one, one, one]), lanes([one, one, one, -hi, -mid, -lo])


PAGES_PER_STEP = 8
SLAB = 8


def _page_suffix_kernel(lf_ref, suf_ref, tot_ref):
    pb, h, page = lf_ref.shape
    u = lax.broadcasted_iota(jnp.int32, (page, page), 0)
    s = lax.broadcasted_iota(jnp.int32, (page, page), 1)
    lf = lf_ref[...].reshape(pb * h, page) * LOG2E
    suf_ref[...] = _dot_l(lf, (u > s).astype(BF16), 3).reshape(pb, h, page)
    tot_ref[...] = _dot_l(lf, jnp.ones((page, page), BF16), 3).reshape(pb, h, page)


def _page_suffix(lf_t):
    n_phys, h, page = lf_t.shape
    pb = math.gcd(n_phys, 64)
    spec = pl.BlockSpec((pb, h, page), lambda i: (i, 0, 0))
    return pl.pallas_call(
        _page_suffix_kernel,
        grid=(n_phys // pb,),
        in_specs=[spec],
        out_specs=[spec, spec],
        out_shape=[jax.ShapeDtypeStruct(lf_t.shape, F32)] * 2,
        compiler_params=_cparams("parallel"),
        name="page_logf_suffix",
    )(lf_t)


def _paged_kernel(pt_ref, q_ref, kn_ref, vn_ref, gn_ref, *refs, n_group, n_slab):
    del pt_ref
    n_kv = n_group * n_slab
    kp_refs, vp_refs = refs[:n_kv], refs[n_kv:2 * n_kv]
    suf_refs, tot_refs = refs[2 * n_kv:3 * n_kv], refs[3 * n_kv:4 * n_kv]
    o_ref, m_scr, l_scr, acc_scr, cq_scr, carry_scr = refs[4 * n_kv:]
    p = pl.program_id(1)
    rows = q_ref.shape[1]
    n_heads = kn_ref.shape[1]
    nq = rows // n_heads
    page = kn_ref.shape[2]
    srows = SLAB * nq
    flat = page * SLAB

    def update(sl, s, pv_of):
        m_old = m_scr[sl]
        m_new = jnp.maximum(m_old, jnp.max(s, axis=1, keepdims=True))
        alpha = jnp.exp2(m_old - m_new)
        pr = jnp.exp2(s - m_new)
        l_scr[sl] = alpha * l_scr[sl] + jnp.sum(pr, axis=1, keepdims=True)
        acc_scr[sl] = alpha * acc_scr[sl] + pv_of(pr)
        m_scr[sl] = m_new

    @pl.when(p == 0)
    def _():
        m_scr[...] = jnp.full_like(m_scr, NEG_BIG)
        l_scr[...] = jnp.zeros_like(l_scr)
        acc_scr[...] = jnp.zeros_like(acc_scr)
        carry_scr[...] = jnp.zeros_like(carry_scr)
        row = lax.broadcasted_iota(jnp.int32, (rows, page), 0)
        lane = lax.broadcasted_iota(jnp.int32, (rows, page), 1)
        u = lax.broadcasted_iota(jnp.int32, (page, page), 0)
        s_ = lax.broadcasted_iota(jnp.int32, (page, page), 1)
        gn = gn_ref[0] * LOG2E
        qi = row % nq
        c_new_col = jnp.sum(jnp.where(lane <= qi, gn, 0.0), axis=1, keepdims=True)
        hi, mid, lo = [x.astype(F32) for x in _split(c_new_col, 3)]
        lane_a = lax.broadcasted_iota(jnp.int32, (rows, LANES), 1)
        cq_scr[...] = jnp.where(lane_a == 0, hi, jnp.where(lane_a == 1, mid,
                                                           jnp.where(lane_a == 2, lo, 0.0))).astype(BF16)
        c_new_row = _dot_l(gn, (u <= s_).astype(BF16), 3)
        s = jnp.concatenate([_dot_nt(q_ref[0, h * nq:(h + 1) * nq, :], kn_ref[0, h])
                             for h in range(n_heads)], axis=0) + (c_new_col - c_new_row)
        s = jnp.where(lane <= qi, s, NEG_BIG)
        for c in range(n_slab):
            sl = slice(c * srows, (c + 1) * srows)
            update(sl, s[sl], lambda pr, c=c: jnp.concatenate(
                [_dot(pr[hh * nq:(hh + 1) * nq, :].astype(BF16), vn_ref[0, c * SLAB + hh])
                 for hh in range(SLAB)], axis=0))

    @pl.when(p > 0)
    def _():
        row_f = lax.broadcasted_iota(jnp.int32, (srows, flat), 0)
        lane_f = lax.broadcasted_iota(jnp.int32, (srows, flat), 1)
        same_head = (row_f // nq) == (lane_f % SLAB)
        lane_k = lax.broadcasted_iota(jnp.int32, (flat, LANES), 1)
        ones_k = (lane_k < 3).astype(BF16)
        slabs = range(n_slab)
        sls = [slice(c * srows, (c + 1) * srows) for c in slabs]
        q_aug = [jnp.concatenate([q_ref[0, sl, :], cq_scr[sl, :]], axis=1) for sl in sls]
        carry = [carry_scr[c:c + 1, :] for c in slabs]
        s = [[] for _ in slabs]
        for g in range(n_group):
            for c in slabs:
                k_rows = kp_refs[g * n_slab + c][...].reshape(flat, HEAD_B).astype(BF16)
                z = _dot_nt(q_aug[c], jnp.concatenate([k_rows, ones_k], axis=1))
                s[c].append(jnp.where(same_head, z + (suf_refs[g * n_slab + c][...] + carry[c]), NEG_BIG))
                carry[c] = carry[c] + tot_refs[g * n_slab + c][...]
        s = [jnp.concatenate(x, axis=1) for x in s]
        m_old = [m_scr[sl] for sl in sls]
        m_new = [jnp.maximum(mo, jnp.max(x, axis=1, keepdims=True)) for mo, x in zip(m_old, s)]
        alpha = [jnp.exp2(mo - mn) for mo, mn in zip(m_old, m_new)]
        pr = [jnp.exp2(x - mn) for x, mn in zip(s, m_new)]
        v_rows = [jnp.concatenate([vp_refs[g * n_slab + c][...].reshape(flat, HEAD_B)
                                   for g in range(n_group)], axis=0).astype(BF16) for c in slabs]
        pv = [_dot(x.astype(BF16), vr) for x, vr in zip(pr, v_rows)]
        for c, sl in enumerate(sls):
            carry_scr[c:c + 1, :] = carry[c]
            l_scr[sl] = alpha[c] * l_scr[sl] + jnp.sum(pr[c], axis=1, keepdims=True)
            acc_scr[sl] = alpha[c] * acc_scr[sl] + pv[c]
            m_scr[sl] = m_new[c]

    @pl.when(p == pl.num_programs(1) - 1)
    def _():
        o_ref[0] = acc_scr[...] / l_scr[...]


def _paged_attention(page_table, q, k_new, v_new, g_new, cache_k, cache_v, suffix, total):
    dbs, rows, _ = q.shape
    n_heads = k_new.shape[1]
    n_slab = n_heads // SLAB
    n_pages = page_table.shape[1]
    page = k_new.shape[2]
    flat = page * SLAB
    n_group = math.gcd(PAGES_PER_STEP, n_pages)
    per_b = lambda b, p, pt: (b, 0, 0)
    per_b4 = lambda b, p, pt: (b, 0, 0, 0)

    def slot_page(b, p, pt, g):
        return pt[b, n_pages - 1 - ((jnp.maximum(p, 1) - 1) * n_group + g)]

    slots = [(g, c) for g in range(n_group) for c in range(n_slab)]
    kv_specs = [pl.BlockSpec((None, page, None, SLAB, HEAD_B),
                             lambda b, p, pt, g=g, c=c: (slot_page(b, p, pt, g), 0, c, 0, 0)) for g, c in slots]
    row_specs = [pl.BlockSpec((None, None, 1, flat),
                              lambda b, p, pt, g=g, c=c: (slot_page(b, p, pt, g), c, 0, 0)) for g, c in slots]
    grid_spec = pltpu.PrefetchScalarGridSpec(
        num_scalar_prefetch=1,
        grid=(dbs, n_pages // n_group + 1),
        in_specs=[pl.BlockSpec((1, rows, HEAD_B), per_b),
                  pl.BlockSpec((1, n_heads, page, HEAD_B), per_b4),
                  pl.BlockSpec((1, n_heads, page, HEAD_B), per_b4),
                  pl.BlockSpec((1, rows, page), per_b)]
                 + kv_specs + kv_specs + row_specs + row_specs,
        out_specs=pl.BlockSpec((1, rows, HEAD_B), per_b),
        scratch_shapes=[pltpu.VMEM((rows, 1), F32), pltpu.VMEM((rows, 1), F32),
                        pltpu.VMEM((rows, HEAD_B), F32), pltpu.VMEM((rows, LANES), BF16),
                        pltpu.VMEM((8, flat), F32)],
    )
    n_kv = len(slots)
    return pl.pallas_call(
        functools.partial(_paged_kernel, n_group=n_group, n_slab=n_slab),
        grid_spec=grid_spec,
        out_shape=jax.ShapeDtypeStruct((dbs, rows, HEAD_B), F32),
        compiler_params=_cparams("parallel", "arbitrary"),
        name="fox_paged_attention",
    )(page_table, q, k_new, v_new, g_new, *([cache_k] * n_kv), *([cache_v] * n_kv),
      *([suffix] * n_kv), *([total] * n_kv))


def _trunk(x, shift0, wkv0, past, p):
    bsz, t, d = x.shape
    m = bsz * t
    n_a = p["tm_w_r"].shape[0]
    depth = p["ln1"].shape[0]
    h_b = d // HEAD_B
    h = x.reshape(m, d)
    v_first = None
    shifts, states = [], []
    t_scan = -(-t // SCAN_CHUNK) * SCAN_CHUNK
    heads_first = lambda z: jnp.swapaxes(z.reshape(bsz, t, h_b, HEAD_B), 1, 2)
    for l in range(depth):
        if l == n_a:
            k_new, k_bf, v_new, v_bf = _norm_matmul(h, p["kv_norm"], [p["fox_w_k"], p["fox_w_v"]], (F32, BF16))
            (lf_pad,) = _norm_matmul(h, p["kv_norm"], [p["fox_w_f"]], (F32,), epilogue="log_sigmoid",
                                     bias=p["fox_b_f"])
            logf_new = lf_pad[:, :h_b].reshape(bsz, t, h_b)
            if past is None:
                q_aug, k_aug = _bias_lanes(*_cumsum_rows_log2(jnp.swapaxes(logf_new, 1, 2)))
            else:
                page = past[0].shape[1]
                pad_keys = lambda z: jnp.pad(heads_first(z), ((0, 0), (0, 0), (0, page - t), (0, 0)))
                k_pad, v_pad = pad_keys(k_bf), pad_keys(v_bf)
                g_new = jnp.repeat(jnp.swapaxes(logf_new, 1, 2), t, axis=1)
                g_new = jnp.pad(g_new, ((0, 0), (0, 0), (0, page - t)))
        if l < n_a:
            shifts.append(_rmsnorm(h.reshape(bsz, t, d)[:, -1], p["ln1"][l]))
            has_v = l > 0
            lv = max(l - 1, 0)
            r, lw, kr, v, a, g = _mix(
                h, shift0[l], t, p["ln1"][l], v_first if has_v else h, p["mu8"][l],
                p["tm_w_r"], p["tm_w_k"], p["tm_w_v"],
                p["tm_w1"], p["tm_w2"], p["tm_a1"], p["tm_a2"],
                p["tm_v1"], p["tm_v2"], p["tm_g1"], p["tm_g2"],
                p["tm_w0"][l].reshape(1, d), p["tm_a0"][l].reshape(1, d), p["tm_v0"][lv].reshape(1, d),
                has_v, l, lv)
            if l == 0:
                v_first = v
            seq = lambda z: jnp.pad(z.reshape(bsz, t, d), ((0, 0), (0, t_scan - t), (0, 0)))
            y, s_bd = _scan(seq(r), seq(lw), seq(kr), seq(v), seq(a), seq(g),
                            p["tm_k_k"][l], p["tm_k_a"][l], p["tm_r_k"][l].reshape(d),
                            p["tm_lnx_w"][l], p["tm_lnx_b"][l], _state_to_bd(wkv0[l]))
            states.append(_bd_to_state(s_bd))
            h = _matmul_residual(y[:, :t].reshape(m, d), p["tm_w_o"], h, l)
        else:
            j = l - n_a
            (q,) = _norm_matmul(h, p["ln1"][l], [p["fox_w_q"]], (BF16,), epilogue="scale",
                                scale=HEAD_B ** -0.5 * LOG2E, layer=j)
            if past is None:
                att = _flash_attention(q.reshape(bsz, t, d), q_aug, k_bf.reshape(bsz, t, d), k_aug,
                                       v_bf.reshape(bsz, t, d)).reshape(m, d)
            else:
                cache_k, cache_v, suffix, total, page_table = past
                o = _paged_attention(page_table, heads_first(q).reshape(bsz, h_b * t, HEAD_B),
                                     k_pad, v_pad, g_new, cache_k, cache_v, suffix, total)
                att = jnp.swapaxes(o.reshape(bsz, h_b, t, HEAD_B), 1, 2).reshape(m, d).astype(BF16)
            h = _matmul_residual(att, p["fox_w_o"], h, j)
        h = _ffn(h, p["ln2"][l], p["ffn_w1"], p["ffn_w3"], p["ffn_w2"], l)
    y = _rmsnorm(h, p["ln_out"])
    h_sh = (bsz, t, h_b, HEAD_B)
    return (y.reshape(bsz, t, d), jnp.stack(shifts), jnp.stack(states),
            k_new.reshape(h_sh), v_new.reshape(h_sh), logf_new)


def kernel(x_prompt, x_sample, state_wkv, state_shift, cache_k, cache_v, cache_logf, page_table, ln1, ln2, ln_out, tm_mu, tm_w_r, tm_w_k, tm_w_v, tm_w_o, tm_w0, tm_w1, tm_w2, tm_a0, tm_a1, tm_a2, tm_v0, tm_v1, tm_v2, tm_g1, tm_g2, tm_k_k, tm_k_a, tm_r_k, tm_lnx_w, tm_lnx_b, kv_norm, fox_w_k, fox_w_v, fox_w_f, fox_b_f, fox_w_q, fox_w_o, ffn_w1, ffn_w3, ffn_w2):
    bf = lambda w: w.astype(BF16)
    n_a, d = tm_w0.shape
    h_b = d // HEAD_B
    lora = lambda w1, w2: (bf(jnp.pad(w1, ((0, 0), (0, 0), (0, -w1.shape[2] % LANES)))),
                           bf(jnp.pad(w2, ((0, 0), (0, -w2.shape[1] % LANES), (0, 0)))))
    w1, w2 = lora(tm_w1, tm_w2)
    a1, a2 = lora(tm_a1, tm_a2)
    if tm_v1.shape[0] == 0:
        tm_v0 = jnp.zeros((1, d), F32)
        tm_v1 = jnp.zeros((1, d, LANES), F32)
        tm_v2 = jnp.zeros((1, LANES, d), F32)
    v1, v2 = lora(tm_v1, tm_v2)
    g1, g2 = lora(tm_g1, tm_g2)
    p = dict(ln1=ln1, ln2=ln2, ln_out=ln_out,
             mu8=jnp.pad(tm_mu, ((0, 0), (0, 2), (0, 0))),
             tm_w_r=bf(tm_w_r), tm_w_k=bf(tm_w_k), tm_w_v=bf(tm_w_v), tm_w_o=bf(tm_w_o),
             tm_w0=tm_w0, tm_w1=w1, tm_w2=w2, tm_a0=tm_a0, tm_a1=a1, tm_a2=a2,
             tm_v0=tm_v0, tm_v1=v1, tm_v2=v2, tm_g1=g1, tm_g2=g2,
             tm_k_k=tm_k_k, tm_k_a=tm_k_a, tm_r_k=tm_r_k, tm_lnx_w=tm_lnx_w, tm_lnx_b=tm_lnx_b,
             kv_norm=kv_norm, fox_w_k=bf(fox_w_k), fox_w_v=bf(fox_w_v),
             fox_w_f=bf(jnp.pad(fox_w_f, ((0, 0), (0, LANES - h_b)))),
             fox_b_f=jnp.pad(fox_b_f, (0, LANES - h_b)).reshape(1, LANES),
             fox_w_q=bf(fox_w_q), fox_w_o=bf(fox_w_o),
             ffn_w1=bf(ffn_w1), ffn_w3=bf(ffn_w3), ffn_w2=bf(ffn_w2))

    bsz = x_prompt.shape[0]
    h_a = d // HEAD_A
    shift0 = jnp.zeros((n_a, bsz, d), x_prompt.dtype)
    wkv0 = jnp.zeros((n_a, bsz, h_a, HEAD_A, HEAD_A), F32)
    y_p, sh_p, wkv_p, k_p, v_p, lf_p = _trunk(x_prompt, shift0, wkv0, None, p)

    n_phys, page = cache_k.shape[:2]
    dec_t = x_sample.shape[1]
    n_slab = h_b // SLAB
    slabs = lambda c: c.reshape(n_phys, page, n_slab, SLAB, HEAD_B)
    key_head_rows = lambda z: jnp.transpose(z.reshape(n_phys, n_slab, SLAB, page), (0, 1, 3, 2)).reshape(
        n_phys, n_slab, 1, page * SLAB)
    suffix, total = _page_suffix(jnp.swapaxes(cache_logf, 1, 2))
    past = (slabs(cache_k), slabs(cache_v), key_head_rows(suffix), key_head_rows(total), page_table)
    y_s, sh_s, wkv_s, k_s, v_s, lf_s = _trunk(x_sample, state_shift, state_wkv, past, p)
    return (y_p, y_s, wkv_p, sh_p, k_p, v_p, lf_p, wkv_s, sh_s, k_s, v_s, lf_s)
```

```python
import functools
import math

import jax
import jax.numpy as jnp
from jax import lax
from jax.experimental import pallas as pl
from jax.experimental.pallas import tpu as pltpu

F32 = jnp.float32
BF16 = jnp.bfloat16

HEAD_A = 64
HEAD_B = 128
LANES = 128
RMS_EPS = 1e-6
LNX_EPS = 64e-5
NEG_BIG = -1e30
LOG2E = math.log2(math.e)
VMEM_LIMIT = 56 * 1024 * 1024


def _cparams(*sem):
    return pltpu.CompilerParams(dimension_semantics=sem, vmem_limit_bytes=VMEM_LIMIT)


def _dot(a, b):
    return jnp.dot(a, b, preferred_element_type=F32)


def _dot_nt(a, b):
    return lax.dot_general(a, b, (((1,), (1,)), ((), ())), preferred_element_type=F32)


def _split(x, n):
    pieces = []
    for _ in range(n - 1):
        hi = x.astype(BF16)
        pieces.append(hi)
        x = x - hi.astype(F32)
    pieces.append(x.astype(BF16))
    return pieces


def _dot_l(x, w, n):
    return sum(_dot(piece, w) for piece in _split(x, n))


def _dot_l_stacked(x, w_stacked, n):
    return _dot(jnp.concatenate(_split(x, n), axis=1), w_stacked)


def _rms(x, g):
    return x * lax.rsqrt(jnp.mean(x * x, axis=-1, keepdims=True) + RMS_EPS) * g


def _softplus(z):
    return jnp.maximum(z, 0.0) + jnp.log1p(jnp.exp(-jnp.abs(z)))


def _row_tile(m):
    return min(512, m)


def _layer_spec(w, layer, block, index, pipeline_mode=None):
    if w.ndim == 2:
        return pl.BlockSpec(block, index, pipeline_mode=pipeline_mode)
    return pl.BlockSpec((None,) + block, lambda *args: (layer,) + index(*args), pipeline_mode=pipeline_mode)


def _rmsnorm_kernel(x_ref, g_ref, o_ref):
    o_ref[...] = _rms(x_ref[...], g_ref[...])


def _rmsnorm(x, g):
    m, d = x.shape
    tm = _row_tile(m)
    return pl.pallas_call(
        _rmsnorm_kernel,
        grid=(m // tm,),
        in_specs=[pl.BlockSpec((tm, d), lambda i: (i, 0)), pl.BlockSpec((1, d), lambda i: (0, 0))],
        out_specs=pl.BlockSpec((tm, d), lambda i: (i, 0)),
        out_shape=jax.ShapeDtypeStruct((m, d), F32),
        compiler_params=_cparams("parallel"),
        name="rmsnorm",
    )(x, g.reshape(1, d))


def _norm_matmul_kernel(x_ref, g_ref, b_ref, *refs, n_w, n_out, epilogue, scale):
    w_refs, o_refs, xn_scr = refs[:n_w], refs[n_w:-1], refs[-1]

    @pl.when(pl.program_id(1) == 0)
    def _():
        xn_scr[...] = _rms(x_ref[...], g_ref[...]).astype(BF16)

    for wi, w_ref in enumerate(w_refs):
        acc = _dot(xn_scr[...], w_ref[...])
        if epilogue == "log_sigmoid":
            acc = -_softplus(-(acc + b_ref[...]))
        elif epilogue == "scale":
            acc = acc * scale
        for o_ref in o_refs[wi * n_out:(wi + 1) * n_out]:
            o_ref[...] = acc.astype(o_ref.dtype)


def _norm_matmul(x, g, ws, out_dtypes, epilogue="none", bias=None, scale=1.0, layer=None):
    m, d = x.shape
    n = ws[0].shape[-1]
    tm, tn = min(1024, m), min(512, n)
    if bias is None:
        bias = jnp.zeros((1, n), F32)
    n_res = len(ws) * len(out_dtypes)
    outs = pl.pallas_call(
        functools.partial(_norm_matmul_kernel, n_w=len(ws), n_out=len(out_dtypes), epilogue=epilogue,
                          scale=scale),
        grid=(m // tm, n // tn),
        in_specs=[pl.BlockSpec((tm, d), lambda i, j: (i, 0)),
                  pl.BlockSpec((1, d), lambda i, j: (0, 0)),
                  pl.BlockSpec((1, tn), lambda i, j: (0, j))]
                 + [_layer_spec(w, layer, (d, tn), lambda i, j: (0, j)) for w in ws],
        out_specs=[pl.BlockSpec((tm, tn), lambda i, j: (i, j)) for _ in range(n_res)],
        out_shape=[jax.ShapeDtypeStruct((m, n), dt) for _ in ws for dt in out_dtypes],
        scratch_shapes=[pltpu.VMEM((tm, d), BF16)],
        compiler_params=_cparams("parallel", "arbitrary"),
        name="norm_matmul_" + epilogue,
    )(x, g.reshape(1, d), bias, *ws)
    return outs


def _proj_ffn_kernel(h_ref, x_ref, wo_ref, g_ref, w1_ref, w3_ref, w2_ref, o_ref, xn_scr, acc_scr):
    j = pl.program_id(1)

    @pl.when(j == 0)
    def _():
        h1 = h_ref[...] + _dot(x_ref[...], wo_ref[...])
        xn_scr[...] = _rms(h1, g_ref[...]).astype(BF16)
        acc_scr[...] = h1

    x = xn_scr[...]
    a = _dot(x, w1_ref[...])
    b = _dot(x, w3_ref[...])
    hm = (a * jax.nn.sigmoid(a) * b).astype(BF16)
    acc_scr[...] += _dot(hm, w2_ref[...])

    @pl.when(j == pl.num_programs(1) - 1)
    def _():
        o_ref[...] = acc_scr[...]


def _proj_ffn(h, x, w_o, layer_o, g, w1, w3, w2, layer):
    m, d = h.shape
    k = x.shape[1]
    f = w1.shape[-1]
    tm, tf = _row_tile(m), 512
    return pl.pallas_call(
        _proj_ffn_kernel,
        grid=(m // tm, f // tf),
        in_specs=[pl.BlockSpec((tm, d), lambda i, j: (i, 0)),
                  pl.BlockSpec((tm, k), lambda i, j: (i, 0)),
                  _layer_spec(w_o, layer_o, (k, d), lambda i, j: (0, 0), pl.Buffered(1)),
                  pl.BlockSpec((1, d), lambda i, j: (0, 0)),
                  _layer_spec(w1, layer, (d, tf), lambda i, j: (0, j)),
                  _layer_spec(w3, layer, (d, tf), lambda i, j: (0, j)),
                  _layer_spec(w2, layer, (tf, d), lambda i, j: (j, 0))],
        out_specs=pl.BlockSpec((tm, d), lambda i, j: (i, 0)),
        out_shape=jax.ShapeDtypeStruct((m, d), F32),
        scratch_shapes=[pltpu.VMEM((tm, d), BF16), pltpu.VMEM((tm, d), F32)],
        compiler_params=_cparams("parallel", "arbitrary"),
        name="proj_ffn",
    )(h, x, w_o, g.reshape(1, d), w1, w3, w2)


def _mix_kernel(h_ref, hp_ref, st_ref, ln_ref, vf_ref, mu_ref, wr_ref, wk_ref, wv_ref,
                w1_ref, w2_ref, a1_ref, a2_ref, v1_ref, v2_ref, g1_ref, g2_ref,
                w0_ref, a0_ref, v0_ref,
                r_ref, lw_ref, k_ref, v_ref, a_ref, g_ref,
                xr_scr, xk_scr, xv_scr, hw_scr, ha_scr, hv_scr, hg_scr, *, has_v, seq_len):
    @pl.when(pl.program_id(1) == 0)
    def _():
        tm = h_ref.shape[0]
        xn = _rms(h_ref[...], ln_ref[...])
        row = lax.broadcasted_iota(jnp.int32, (tm, 1), 0)
        before_tile = _rms(hp_ref[7:8, :], ln_ref[...])
        xprev = jnp.where(row == 0, before_tile, pltpu.roll(xn, 1, axis=0))
        if seq_len >= tm:
            starts_here = pl.program_id(0) % (seq_len // tm) == 0
            xprev = jnp.where((row == 0) & starts_here, st_ref[0, 0:1, :], xprev)
        else:
            for s in range(tm // seq_len):
                xprev = jnp.where(row == s * seq_len, st_ref[0, s:s + 1, :], xprev)
        xx = xprev - xn
        mix = lambda i: (xn + xx * mu_ref[i:i + 1, :]).astype(BF16)
        xr_scr[...] = mix(0)
        hw_scr[...] = jnp.tanh(_dot(mix(1), w1_ref[...])).astype(BF16)
        xk_scr[...] = mix(2)
        xv = mix(3)
        xv_scr[...] = xv
        if has_v:
            hv_scr[...] = _dot(xv, v1_ref[...]).astype(BF16)
        ha_scr[...] = _dot(mix(4), a1_ref[...]).astype(BF16)
        hg_scr[...] = jax.nn.sigmoid(_dot(mix(5), g1_ref[...])).astype(BF16)

    r_ref[...] = _dot(xr_scr[...], wr_ref[...]).astype(r_ref.dtype)
    k_ref[...] = _dot(xk_scr[...], wk_ref[...]).astype(k_ref.dtype)
    v = _dot(xv_scr[...], wv_ref[...])
    if has_v:
        v = v + (vf_ref[...] - v) * jax.nn.sigmoid(v0_ref[...] + _dot(hv_scr[...], v2_ref[...]))
    v_ref[...] = v.astype(v_ref.dtype)
    w_log = -_softplus(-(w0_ref[...] + _dot(hw_scr[...], w2_ref[...]))) - 0.5
    lw_ref[...] = -jnp.exp(w_log)
    a_ref[...] = jax.nn.sigmoid(a0_ref[...] + _dot(ha_scr[...], a2_ref[...])).astype(a_ref.dtype)
    g_ref[...] = _dot(hg_scr[...], g2_ref[...]).astype(g_ref.dtype)


def _mix(h, shift_state, seq_len, ln, vfirst, mu, wr, wk, wv, w1, w2, a1, a2, v1, v2, g1, g2, w0, a0, v0,
         has_v, layer, layer_v):
    m, d = h.shape
    tm, tn = _row_tile(m), 256
    assert seq_len % tm == 0 or tm % seq_len == 0
    n_starts = max(1, tm // seq_len)
    seq_of = [[(i * tm + s * seq_len) // seq_len for s in range(n_starts)] for i in range(m // tm)]
    starts = shift_state[jnp.asarray(seq_of)]
    lw_, la_, lv_, lg_ = w1.shape[-1], a1.shape[-1], v1.shape[-1], g1.shape[-1]
    row = lambda i, j: (i, 0)
    col = lambda i, j: (0, j)
    tile = lambda i, j: (i, j)
    fixed = lambda i, j: (0, 0)
    down = lambda w, lyr: _layer_spec(w, lyr, (d, w.shape[-1]), fixed)
    up = lambda w, lyr: _layer_spec(w, lyr, (w.shape[-2], tn), col)
    outs = pl.pallas_call(
        functools.partial(_mix_kernel, has_v=has_v, seq_len=seq_len),
        grid=(m // tm, d // tn),
        in_specs=[pl.BlockSpec((tm, d), row),
                  pl.BlockSpec((8, d), lambda i, j: (jnp.maximum(i * (tm // 8) - 1, 0), 0)),
                  pl.BlockSpec((1, n_starts, d), lambda i, j: (i, 0, 0)),
                  pl.BlockSpec((1, d), fixed),
                  pl.BlockSpec((tm, tn), tile),
                  pl.BlockSpec((8, d), fixed),
                  up(wr, layer), up(wk, layer), up(wv, layer),
                  down(w1, layer), up(w2, layer), down(a1, layer), up(a2, layer),
                  down(v1, layer_v), up(v2, layer_v), down(g1, layer), up(g2, layer),
                  pl.BlockSpec((1, tn), col), pl.BlockSpec((1, tn), col), pl.BlockSpec((1, tn), col)],
        out_specs=[pl.BlockSpec((tm, tn), tile) for _ in range(6)],
        out_shape=[jax.ShapeDtypeStruct((m, d), dt) for dt in (BF16, F32, BF16, BF16, BF16, BF16)],
        scratch_shapes=[pltpu.VMEM((tm, d), BF16), pltpu.VMEM((tm, d), BF16), pltpu.VMEM((tm, d), BF16),
                        pltpu.VMEM((tm, lw_), BF16), pltpu.VMEM((tm, la_), BF16),
                        pltpu.VMEM((tm, lv_), BF16), pltpu.VMEM((tm, lg_), BF16)],
        compiler_params=_cparams("parallel", "arbitrary"),
        name="rwkv_mix_proj",
    )(h, h, starts, ln.reshape(1, d), vfirst, mu, wr, wk, wv, w1, w2, a1, a2, v1, v2, g1, g2, w0, a0, v0)
    return outs


SCAN_CHUNK = 64
SCAN_PAIRS = 16


def _scan_kernel(r_ref, lw_ref, kr_ref, v_ref, a_ref, g_ref,
                 kk_ref, ka_ref, rk_ref, lnw_ref, lnb_ref, s0_ref,
                 y_ref, sout_ref, st_scr):
    c, n_chunks = pl.program_id(2), pl.num_programs(2)
    C = SCAN_CHUNK
    R = 2 * C
    n_pairs = st_scr.shape[0]
    f32 = lambda ref, sl: ref[0, :, sl].astype(F32)

    @pl.when(c == 0)
    def _():
        st_scr[...] = s0_ref[0]

    lane_c = lax.broadcasted_iota(jnp.int32, (C, LANES), 1)
    head0 = lane_c < HEAD_A
    row = lax.broadcasted_iota(jnp.int32, (R, R), 0)
    col = lax.broadcasted_iota(jnp.int32, (R, R), 1)
    same = (row // C) == (col // C)
    strict = same & ((col % C) < (row % C))
    incl = same & ((col % C) <= (row % C))
    row2 = lax.broadcasted_iota(jnp.int32, (2 * LANES, LANES), 0)
    col2 = lax.broadcasted_iota(jnp.int32, (2 * LANES, LANES), 1)
    ones2 = (((row2 % LANES) // HEAD_A) == (col2 // HEAD_A)).astype(BF16)
    head_sum = lambda x: _dot_l_stacked(x, ones2, 2)
    ti = lax.broadcasted_iota(jnp.int32, (C, 3 * C), 0)
    tj = lax.broadcasted_iota(jnp.int32, (C, 3 * C), 1)
    tri3 = ((tj % C) <= ti).astype(BF16)
    zeros_c = jnp.zeros((C, LANES), F32)

    def stack(x):
        return jnp.concatenate([jnp.where(head0, x, 0.0), jnp.where(head0, 0.0, x)], axis=0)

    sls = [slice(p * LANES, (p + 1) * LANES) for p in range(n_pairs)]
    each = lambda f, *cols: [f(*xs) for xs in zip(*cols)]
    bf = lambda xs: [x.astype(BF16) for x in xs]

    r = [f32(r_ref, sl) for sl in sls]
    lw = [lw_ref[0, :, sl] for sl in sls]
    kr = [f32(kr_ref, sl) for sl in sls]
    v = [f32(v_ref, sl) for sl in sls]
    a = [f32(a_ref, sl) for sl in sls]

    kkr = each(lambda x, sl: x * kk_ref[:, sl], kr, sls)
    ss = each(lambda x: head_sum(x * x), kkr)
    kk = each(lambda x, s: x / jnp.maximum(jnp.sqrt(s), 1e-12), kkr, ss)
    k = each(lambda x, y, sl: x * (1.0 + (y - 1.0) * ka_ref[:, sl]), kr, a, sls)
    b = each(lambda x, y: x * y, kk, a)

    cum = each(lambda x: _dot(tri3, jnp.concatenate(_split(x, 3), axis=0)), lw)
    g_in = each(jnp.exp, cum)
    g_ex = each(lambda x, y: jnp.exp(x - y), cum, lw)
    g_inv = each(lambda x: jnp.exp(-x), cum)
    g_tail = each(lambda x: jnp.exp(x[C - 1:C, :] - x), cum)

    at = each(lambda x, y: stack(-x * y), kk, g_ex)
    at_b = bf(at)
    rt = bf(each(lambda x, y: stack(x * y), r, g_in))
    bt = bf(each(lambda x, y: stack(x * y), b, g_inv))
    kt = bf(each(lambda x, y: stack(x * y), k, g_inv))
    bh = each(lambda x, y: stack(x * y), b, g_tail)
    kh = each(lambda x, y: stack(x * y), k, g_tail)
    vs = bf(each(stack, v))

    sc = each(lambda w, x, y, z: _dot_nt(jnp.concatenate([w, x], axis=0), jnp.concatenate([y, z], axis=0)),
              at_b, rt, bt, kt)
    a_ab = each(lambda x: jnp.where(strict, x[:R, :R], 0.0), sc)
    a_ak = bf(each(lambda x: jnp.where(strict, x[:R, R:], 0.0), sc))
    a_rb = bf(each(lambda x: jnp.where(incl, x[R:, :R], 0.0), sc))
    a_rk = bf(each(lambda x: jnp.where(incl, x[R:, R:], 0.0), sc))

    x = each(lambda w, y, z: jnp.concatenate([w, _dot(y, z)], axis=1), at, a_ak, vs)
    pw = a_ab
    n_sq = int(math.log2(C))
    for it in range(n_sq):
        pw_b = bf(pw)
        x = each(lambda y, z: y + _dot(z, y.astype(BF16)), x, pw_b)
        if it + 1 < n_sq:
            pw = each(lambda z: _dot(z, z), pw_b)
    a_hat = bf(each(lambda y: y[:, :LANES], x))
    u_bar = each(lambda y: y[:, LANES:], x)

    st = [st_scr[p] for p in range(n_pairs)]
    st_b = bf(st)
    u = each(lambda w, y, z: _dot(w, y) + z, a_hat, st_b, u_bar)
    u_b = bf(u)
    uv = each(lambda y, t: jnp.concatenate([y, t], axis=0), u_b, vs)
    o_st = each(lambda q, w, z, s, y: _dot(jnp.concatenate([q, w, z], axis=1), jnp.concatenate([s, y], axis=0)),
                rt, a_rb, a_rk, st_b, uv)
    o = each(lambda y: y[:C] + y[C:], o_st)

    def decay_col(x):
        x_t = jnp.concatenate([x, zeros_c], axis=0).T if C < LANES else x.T
        return jnp.exp(jnp.sum(x_t, axis=1, keepdims=True))

    dcol = each(decay_col, lw)
    st_new = each(lambda dc, s, w, z, y: dc * s + _dot(jnp.concatenate([w.T, z.T], axis=1).astype(BF16), y),
                  dcol, st, bh, kh, uv)
    for p in range(n_pairs):
        st_scr[p] = st_new[p]

    mean = each(lambda y: head_sum(y) * (1.0 / HEAD_A), o)
    dlt = each(lambda y, z: y - z, o, mean)
    var = each(lambda y: head_sum(y * y) * (1.0 / HEAD_A), dlt)
    on = each(lambda y, z, sl: y * lax.rsqrt(z + LNX_EPS) * lnw_ref[:, sl] + lnb_ref[:, sl], dlt, var, sls)
    bonus = each(lambda x, y, z, sl: head_sum(x * y * rk_ref[:, sl]) * z, r, k, v, sls)
    for p, sl in enumerate(sls):
        y_ref[0, :, sl] = ((on[p] + bonus[p]) * f32(g_ref, sl)).astype(y_ref.dtype)

    @pl.when(c == n_chunks - 1)
    def _():
        sout_ref[0] = st_scr[...]


def _scan(r, lw, kr, v, a, g, k_k, k_a, r_k, lnx_w, lnx_b, s0_bd):
    bsz, t, d = r.shape
    C = SCAN_CHUNK
    P = min(SCAN_PAIRS, d // LANES)
    w = P * LANES
    n_pg = d // w
    seq = pl.BlockSpec((1, C, w), lambda b, q, c: (b, c, q))
    vec = pl.BlockSpec((1, w), lambda b, q, c: (0, q))
    sbd = pl.BlockSpec((1, P, LANES, LANES), lambda b, q, c: (b, q, 0, 0))
    y, s_out = pl.pallas_call(
        _scan_kernel,
        grid=(bsz, n_pg, t // C),
        in_specs=[seq] * 6 + [vec] * 5 + [sbd],
        out_specs=[seq, sbd],
        out_shape=[jax.ShapeDtypeStruct((bsz, t, d), BF16),
                   jax.ShapeDtypeStruct(s0_bd.shape, F32)],
        scratch_shapes=[pltpu.VMEM((P, LANES, LANES), F32)],
        compiler_params=_cparams("parallel", "parallel", "arbitrary"),
        name="wkv7_scan",
    )(r, lw, kr, v, a, g, k_k.reshape(1, d), k_a.reshape(1, d), r_k.reshape(1, d),
      lnx_w.reshape(1, d), lnx_b.reshape(1, d), s0_bd)
    return y, s_out


def _state_to_bd(s):
    bsz, h = s.shape[:2]
    st = jnp.swapaxes(s, 2, 3).reshape(bsz, h // 2, 2, HEAD_A, HEAD_A)
    eye = jnp.eye(2, dtype=s.dtype)
    bd = st[:, :, :, :, None, :] * eye[None, None, :, None, :, None]
    return bd.reshape(bsz, h // 2, 2 * HEAD_A, 2 * HEAD_A)


def _bd_to_state(bd):
    bsz, hp = bd.shape[:2]
    x = bd.reshape(bsz, hp, 2, HEAD_A, 2, HEAD_A)
    st = jnp.stack([x[:, :, 0, :, 0, :], x[:, :, 1, :, 1, :]], axis=2)
    return jnp.swapaxes(st.reshape(bsz, hp * 2, HEAD_A, HEAD_A), 2, 3)


def _cumsum_kernel(x_ref, hi_ref, mid_ref, lo_ref, carry_scr):
    @pl.when(pl.program_id(1) == 0)
    def _():
        carry_scr[...] = jnp.zeros_like(carry_scr)

    tt = x_ref.shape[2]
    u = lax.broadcasted_iota(jnp.int32, (tt, tt), 0)
    s = lax.broadcasted_iota(jnp.int32, (tt, tt), 1)
    tri = (u <= s).astype(BF16)
    c = _dot_l(x_ref[0], tri, 3) + carry_scr[...]
    carry_scr[...] = c[:, tt - 1:tt]
    hi_ref[0], mid_ref[0], lo_ref[0] = _split(c * LOG2E, 3)


def _cumsum_rows_log2(x):
    bsz, h, t = x.shape
    tt = min(512, t)
    spec = pl.BlockSpec((1, h, tt), lambda b, i: (b, 0, i))
    return pl.pallas_call(
        _cumsum_kernel,
        grid=(bsz, t // tt),
        in_specs=[spec],
        out_specs=[spec] * 3,
        out_shape=[jax.ShapeDtypeStruct((bsz, h, t), BF16)] * 3,
        scratch_shapes=[pltpu.VMEM((h, 1), F32)],
        compiler_params=_cparams("parallel", "arbitrary"),
        name="logf_cumsum",
    )(x)


def _flash_kernel(q_ref, qa_ref, k_ref, ka_ref, v_ref, o_ref, m_scr, acc_scr):
    i, j = pl.program_id(1), pl.program_id(2)
    tq, tk = q_ref.shape[1], k_ref.shape[1]
    n_heads = q_ref.shape[2] // HEAD_B

    @pl.when(j == 0)
    def _():
        m_scr[...] = jnp.full_like(m_scr, NEG_BIG)
        acc_scr[...] = jnp.zeros_like(acc_scr)

    def step(masked):
        if masked:
            qpos = lax.broadcasted_iota(jnp.int32, (tq, tk), 0)
            kpos = lax.broadcasted_iota(jnp.int32, (tq, tk), 1)
            keep = kpos <= qpos
        ones = jnp.ones((tk, HEAD_B), BF16)
        for h in range(n_heads):
            hs = slice(h * HEAD_B, (h + 1) * HEAD_B)
            q_aug = jnp.concatenate([q_ref[0, :, hs], qa_ref[0, :, hs]], axis=1)
            k_aug = jnp.concatenate([k_ref[0, :, hs], ka_ref[0, :, hs]], axis=1)
            s = _dot_nt(q_aug, k_aug)
            if masked:
                s = jnp.where(keep, s, NEG_BIG)
            m_prev = m_scr[h]
            m_next = jnp.maximum(m_prev, jnp.max(s, axis=1, keepdims=True))
            alpha = jnp.exp2(m_prev - m_next)
            p = jnp.exp2(s - jnp.concatenate([m_next] * (tk // LANES), axis=1))
            v_aug = jnp.concatenate([v_ref[0, :, hs], ones], axis=1)
            acc_scr[h] = jnp.concatenate([alpha, alpha], axis=1) * acc_scr[h] + _dot(p.astype(BF16), v_aug)
            m_scr[h] = m_next

    @pl.when(j < i)
    def _():
        step(False)

    @pl.when(j == i)
    def _():
        step(True)
        for h in range(n_heads):
            hs = slice(h * HEAD_B, (h + 1) * HEAD_B)
            acc = acc_scr[h]
            o_ref[0, :, hs] = (acc[:, :HEAD_B] / acc[:, HEAD_B:]).astype(o_ref.dtype)


def _flash_attention(q, q_aug, k, k_aug, v):
    bsz, t, d = q.shape
    h = d // HEAD_B
    tq = min(512, t)
    nq = t // tq
    q_idx = lambda b, i, j: (b, i, 0)
    kv_idx = lambda b, i, j: (b, jnp.minimum(j, i), 0)
    return pl.pallas_call(
        _flash_kernel,
        grid=(bsz, nq, nq),
        in_specs=[pl.BlockSpec((1, tq, d), q_idx), pl.BlockSpec((1, tq, d), q_idx),
                  pl.BlockSpec((1, tq, d), kv_idx), pl.BlockSpec((1, tq, d), kv_idx),
                  pl.BlockSpec((1, tq, d), kv_idx)],
        out_specs=pl.BlockSpec((1, tq, d), q_idx),
        out_shape=jax.ShapeDtypeStruct((bsz, t, d), BF16),
        scratch_shapes=[pltpu.VMEM((h, tq, LANES), F32), pltpu.VMEM((h, tq, 2 * HEAD_B), F32)],
        compiler_params=_cparams("parallel", "parallel", "arbitrary"),
        name="fox_prompt_attention",
    )(q, q_aug, k, k_aug, v)


def _bias_lanes(hi, mid, lo):
    bsz, h, t = hi.shape
    one = jnp.ones_like(hi)

    def lanes(parts):
        x = jnp.stack(parts, axis=-1)
        x = jnp.pad(jnp.swapaxes(x, 1, 2), ((0, 0), (0, 0), (0, 0), (0, HEAD_B - len(parts))))
        return x.reshape(bsz, t, h * HEAD_B)

    return lanes([hi, mid, lo, one, one, one]), lanes([one, one, one, -hi, -mid, -lo])


PAGES_PER_STEP = 8
SLAB = 8


def _page_suffix_kernel(lf_ref, suf_ref, tot_ref):
    pb, h, page = lf_ref.shape
    u = lax.broadcasted_iota(jnp.int32, (page, page), 0)
    s = lax.broadcasted_iota(jnp.int32, (page, page), 1)
    lf = lf_ref[...].reshape(pb * h, page) * LOG2E
    suf_ref[...] = _dot_l(lf, (u > s).astype(BF16), 3).reshape(pb, h, page)
    tot_ref[...] = _dot_l(lf, jnp.ones((page, page), BF16), 3).reshape(pb, h, page)


def _page_suffix(lf_t):
    n_phys, h, page = lf_t.shape
    pb = math.gcd(n_phys, 64)
    spec = pl.BlockSpec((pb, h, page), lambda i: (i, 0, 0))
    return pl.pallas_call(
        _page_suffix_kernel,
        grid=(n_phys // pb,),
        in_specs=[spec],
        out_specs=[spec, spec],
        out_shape=[jax.ShapeDtypeStruct(lf_t.shape, F32)] * 2,
        compiler_params=_cparams("parallel"),
        name="page_logf_suffix",
    )(lf_t)


def _paged_kernel(pt_ref, q_ref, kn_ref, vn_ref, gn_ref, *refs, n_group, n_slab):
    del pt_ref
    p, n_steps = pl.program_id(1), pl.num_programs(1)
    n_kv = n_group * n_slab
    kp_refs, vp_refs = refs[:n_kv], refs[n_kv:2 * n_kv]
    suf_refs, tot_refs = refs[2 * n_kv:3 * n_kv], refs[3 * n_kv:4 * n_kv]
    o_ref, m_scr, l_scr, acc_scr, cq_scr, carry_scr = refs[4 * n_kv:]
    rows = q_ref.shape[1]
    n_heads = kn_ref.shape[1]
    nq = rows // n_heads
    page = kn_ref.shape[2]
    srows = SLAB * nq
    flat = page * SLAB

    def update(sl, s, pv_of):
        m_old = m_scr[sl]
        m_new = jnp.maximum(m_old, jnp.max(s, axis=1, keepdims=True))
        alpha = jnp.exp2(m_old - m_new)
        pr = jnp.exp2(s - m_new)
        l_scr[sl] = alpha * l_scr[sl] + jnp.sum(pr, axis=1, keepdims=True)
        acc_scr[sl] = alpha * acc_scr[sl] + pv_of(pr)
        m_scr[sl] = m_new

    @pl.when(p == 0)
    def _():
        m_scr[...] = jnp.full_like(m_scr, NEG_BIG)
        l_scr[...] = jnp.zeros_like(l_scr)
        acc_scr[...] = jnp.zeros_like(acc_scr)
        carry_scr[...] = jnp.zeros_like(carry_scr)
        row = lax.broadcasted_iota(jnp.int32, (rows, page), 0)
        lane = lax.broadcasted_iota(jnp.int32, (rows, page), 1)
        u = lax.broadcasted_iota(jnp.int32, (page, page), 0)
        s_ = lax.broadcasted_iota(jnp.int32, (page, page), 1)
        gn = gn_ref[0] * LOG2E
        qi = row % nq
        c_new_col = jnp.sum(jnp.where(lane <= qi, gn, 0.0), axis=1, keepdims=True)
        hi, mid, lo = [x.astype(F32) for x in _split(c_new_col, 3)]
        lane_a = lax.broadcasted_iota(jnp.int32, (rows, LANES), 1)
        cq_scr[...] = jnp.where(lane_a == 0, hi, jnp.where(lane_a == 1, mid,
                                                           jnp.where(lane_a == 2, lo, 0.0))).astype(BF16)
        c_new_row = _dot_l(gn, (u <= s_).astype(BF16), 3)
        s = jnp.concatenate([_dot_nt(q_ref[0, h * nq:(h + 1) * nq, :], kn_ref[0, h])
                             for h in range(n_heads)], axis=0) + (c_new_col - c_new_row)
        s = jnp.where(lane <= qi, s, NEG_BIG)
        for c in range(n_slab):
            sl = slice(c * srows, (c + 1) * srows)
            update(sl, s[sl], lambda pr, c=c: jnp.concatenate(
                [_dot(pr[hh * nq:(hh + 1) * nq, :].astype(BF16), vn_ref[0, c * SLAB + hh])
                 for hh in range(SLAB)], axis=0))

    row_f = lax.broadcasted_iota(jnp.int32, (srows, flat), 0)
    lane_f = lax.broadcasted_iota(jnp.int32, (srows, flat), 1)
    same_head = (row_f // nq) == (lane_f % SLAB)
    lane_k = lax.broadcasted_iota(jnp.int32, (flat, LANES), 1)
    ones_k = (lane_k < 3).astype(BF16)
    slabs = range(n_slab)
    sls = [slice(c * srows, (c + 1) * srows) for c in slabs]
    q_aug = [jnp.concatenate([q_ref[0, sl, :], cq_scr[sl, :]], axis=1) for sl in sls]
    carry = [carry_scr[c:c + 1, :] for c in slabs]
    s = [[] for _ in slabs]
    for g in range(n_group):
        for c in slabs:
            k_rows = kp_refs[g * n_slab + c][...].reshape(flat, HEAD_B).astype(BF16)
            z = _dot_nt(q_aug[c], jnp.concatenate([k_rows, ones_k], axis=1))
            s[c].append(jnp.where(same_head, z + (suf_refs[g * n_slab + c][...] + carry[c]), NEG_BIG))
            carry[c] = carry[c] + tot_refs[g * n_slab + c][...]
    s = [jnp.concatenate(x, axis=1) for x in s]
    m_old = [m_scr[sl] for sl in sls]
    m_new = [jnp.maximum(mo, jnp.max(x, axis=1, keepdims=True)) for mo, x in zip(m_old, s)]
    alpha = [jnp.exp2(mo - mn) for mo, mn in zip(m_old, m_new)]
    pr = [jnp.exp2(x - mn) for x, mn in zip(s, m_new)]
    v_rows = [jnp.concatenate([vp_refs[g * n_slab + c][...].reshape(flat, HEAD_B)
                               for g in range(n_group)], axis=0).astype(BF16) for c in slabs]
    pv = [_dot(x.astype(BF16), vr) for x, vr in zip(pr, v_rows)]
    for c, sl in enumerate(sls):
        carry_scr[c:c + 1, :] = carry[c]
        l_scr[sl] = alpha[c] * l_scr[sl] + jnp.sum(pr[c], axis=1, keepdims=True)
        acc_scr[sl] = alpha[c] * acc_scr[sl] + pv[c]
        m_scr[sl] = m_new[c]

    @pl.when(p == n_steps - 1)
    def _():
        o_ref[0] = acc_scr[...] / l_scr[...]


def _paged_attention(page_table, q, k_new, v_new, g_new, cache_k, cache_v, suffix, total):
    dbs, rows, _ = q.shape
    n_heads = k_new.shape[1]
    n_slab = n_heads // SLAB
    n_pages = page_table.shape[1]
    page = k_new.shape[2]
    flat = page * SLAB
    n_group = math.gcd(PAGES_PER_STEP, n_pages)
    per_b = lambda b, p, pt: (b, 0, 0)
    per_b4 = lambda b, p, pt: (b, 0, 0, 0)
    slot_page = lambda b, p, pt, g: pt[b, n_pages - 1 - (p * n_group + g)]
    slots = [(g, c) for g in range(n_group) for c in range(n_slab)]
    kv_specs = [pl.BlockSpec((None, page, None, SLAB, HEAD_B),
                             lambda b, p, pt, g=g, c=c: (slot_page(b, p, pt, g), 0, c, 0, 0)) for g, c in slots]
    row_specs = [pl.BlockSpec((None, None, 1, flat),
                              lambda b, p, pt, g=g, c=c: (slot_page(b, p, pt, g), c, 0, 0)) for g, c in slots]
    n_kv = len(slots)
    grid_spec = pltpu.PrefetchScalarGridSpec(
        num_scalar_prefetch=1,
        grid=(dbs, n_pages // n_group),
        in_specs=[pl.BlockSpec((1, rows, HEAD_B), per_b),
                  pl.BlockSpec((1, n_heads, page, HEAD_B), per_b4),
                  pl.BlockSpec((1, n_heads, page, HEAD_B), per_b4),
                  pl.BlockSpec((1, rows, page), per_b)] + kv_specs + kv_specs + row_specs + row_specs,
        out_specs=pl.BlockSpec((1, rows, HEAD_B), per_b),
        scratch_shapes=[pltpu.VMEM((rows, 1), F32), pltpu.VMEM((rows, 1), F32),
                        pltpu.VMEM((rows, HEAD_B), F32), pltpu.VMEM((rows, LANES), BF16),
                        pltpu.VMEM((8, flat), F32)])
    return pl.pallas_call(
        functools.partial(_paged_kernel, n_group=n_group, n_slab=n_slab),
        grid_spec=grid_spec,
        out_shape=jax.ShapeDtypeStruct((dbs, rows, HEAD_B), F32),
        compiler_params=_cparams("parallel", "arbitrary"),
        name="fox_paged_attention",
    )(page_table, q, k_new, v_new, g_new, *([cache_k] * n_kv), *([cache_v] * n_kv),
      *([suffix] * n_kv), *([total] * n_kv))


def _trunk(x, shift0, wkv0, past, p):
    bsz, t, d = x.shape
    m = bsz * t
    n_a = p["tm_w_r"].shape[0]
    depth = p["ln1"].shape[0]
    h_b = d // HEAD_B
    h = x.reshape(m, d)
    v_first = None
    shifts, states = [], []
    t_scan = -(-t // SCAN_CHUNK) * SCAN_CHUNK
    heads_first = lambda z: jnp.swapaxes(z.reshape(bsz, t, h_b, HEAD_B), 1, 2)
    for l in range(depth):
        if l == n_a:
            k_new, k_bf, v_new, v_bf = _norm_matmul(h, p["kv_norm"], [p["fox_w_k"], p["fox_w_v"]], (F32, BF16))
            (lf_pad,) = _norm_matmul(h, p["kv_norm"], [p["fox_w_f"]], (F32,), epilogue="log_sigmoid",
                                     bias=p["fox_b_f"])
            logf_new = lf_pad[:, :h_b].reshape(bsz, t, h_b)
            if past is None:
                q_aug, k_aug = _bias_lanes(*_cumsum_rows_log2(jnp.swapaxes(logf_new, 1, 2)))
            else:
                page = past[0].shape[1]
                pad_keys = lambda z: jnp.pad(heads_first(z), ((0, 0), (0, 0), (0, page - t), (0, 0)))
                k_pad, v_pad = pad_keys(k_bf), pad_keys(v_bf)
                g_new = jnp.repeat(jnp.swapaxes(logf_new, 1, 2), t, axis=1)
                g_new = jnp.pad(g_new, ((0, 0), (0, 0), (0, page - t)))
        if l < n_a:
            shifts.append(_rmsnorm(h.reshape(bsz, t, d)[:, -1], p["ln1"][l]))
            has_v = l > 0
            lv = max(l - 1, 0)
            r, lw, kr, v, a, g = _mix(
                h, shift0[l], t, p["ln1"][l], v_first if has_v else h, p["mu8"][l],
                p["tm_w_r"], p["tm_w_k"], p["tm_w_v"],
                p["tm_w1"], p["tm_w2"], p["tm_a1"], p["tm_a2"],
                p["tm_v1"], p["tm_v2"], p["tm_g1"], p["tm_g2"],
                p["tm_w0"][l].reshape(1, d), p["tm_a0"][l].reshape(1, d), p["tm_v0"][lv].reshape(1, d),
                has_v, l, lv)
            if l == 0:
                v_first = v
            seq = lambda z: jnp.pad(z.reshape(bsz, t, d), ((0, 0), (0, t_scan - t), (0, 0)))
            y, s_bd = _scan(seq(r), seq(lw), seq(kr), seq(v), seq(a), seq(g),
                            p["tm_k_k"][l], p["tm_k_a"][l], p["tm_r_k"][l].reshape(d),
                            p["tm_lnx_w"][l], p["tm_lnx_b"][l], _state_to_bd(wkv0[l]))
            states.append(_bd_to_state(s_bd))
            mixed, w_o, layer_o = y[:, :t].reshape(m, d), p["tm_w_o"], l
        else:
            j = l - n_a
            (q,) = _norm_matmul(h, p["ln1"][l], [p["fox_w_q"]], (BF16,), epilogue="scale",
                                scale=HEAD_B ** -0.5 * LOG2E, layer=j)
            if past is None:
                att = _flash_attention(q.reshape(bsz, t, d), q_aug, k_bf.reshape(bsz, t, d), k_aug,
                                       v_bf.reshape(bsz, t, d)).reshape(m, d)
            else:
                cache_k, cache_v, suffix, total, page_table = past
                o = _paged_attention(page_table, heads_first(q).reshape(bsz, h_b * t, HEAD_B),
                                     k_pad, v_pad, g_new, cache_k, cache_v, suffix, total)
                att = jnp.swapaxes(o.reshape(bsz, h_b, t, HEAD_B), 1, 2).reshape(m, d).astype(BF16)
            mixed, w_o, layer_o = att, p["fox_w_o"], j
        h = _proj_ffn(h, mixed, w_o, layer_o, p["ln2"][l], p["ffn_w1"], p["ffn_w3"], p["ffn_w2"], l)
    y = _rmsnorm(h, p["ln_out"])
    h_sh = (bsz, t, h_b, HEAD_B)
    return (y.reshape(bsz, t, d), jnp.stack(shifts), jnp.stack(states),
            k_new.reshape(h_sh), v_new.reshape(h_sh), logf_new)


def kernel(x_prompt, x_sample, state_wkv, state_shift, cache_k, cache_v, cache_logf, page_table, ln1, ln2, ln_out, tm_mu, tm_w_r, tm_w_k, tm_w_v, tm_w_o, tm_w0, tm_w1, tm_w2, tm_a0, tm_a1, tm_a2, tm_v0, tm_v1, tm_v2, tm_g1, tm_g2, tm_k_k, tm_k_a, tm_r_k, tm_lnx_w, tm_lnx_b, kv_norm, fox_w_k, fox_w_v, fox_w_f, fox_b_f, fox_w_q, fox_w_o, ffn_w1, ffn_w3, ffn_w2):
    bf = lambda w: w.astype(BF16)
    n_a, d = tm_w0.shape
    h_b = d // HEAD_B
    lora = lambda w1, w2: (bf(jnp.pad(w1, ((0, 0), (0, 0), (0, -w1.shape[2] % LANES)))),
                           bf(jnp.pad(w2, ((0, 0), (0, -w2.shape[1] % LANES), (0, 0)))))
    w1, w2 = lora(tm_w1, tm_w2)
    a1, a2 = lora(tm_a1, tm_a2)
    if tm_v1.shape[0] == 0:
        tm_v0 = jnp.zeros((1, d), F32)
        tm_v1 = jnp.zeros((1, d, LANES), F32)
        tm_v2 = jnp.zeros((1, LANES, d), F32)
    v1, v2 = lora(tm_v1, tm_v2)
    g1, g2 = lora(tm_g1, tm_g2)
    p = dict(ln1=ln1, ln2=ln2, ln_out=ln_out,
             mu8=jnp.pad(tm_mu, ((0, 0), (0, 2), (0, 0))),
             tm_w_r=bf(tm_w_r), tm_w_k=bf(tm_w_k), tm_w_v=bf(tm_w_v), tm_w_o=bf(tm_w_o),
             tm_w0=tm_w0, tm_w1=w1, tm_w2=w2, tm_a0=tm_a0, tm_a1=a1, tm_a2=a2,
             tm_v0=tm_v0, tm_v1=v1, tm_v2=v2, tm_g1=g1, tm_g2=g2,
             tm_k_k=tm_k_k, tm_k_a=tm_k_a, tm_r_k=tm_r_k, tm_lnx_w=tm_lnx_w, tm_lnx_b=tm_lnx_b,
             kv_norm=kv_norm, fox_w_k=bf(fox_w_k), fox_w_v=bf(fox_w_v),
             fox_w_f=bf(jnp.pad(fox_w_f, ((0, 0), (0, LANES - h_b)))),
             fox_b_f=jnp.pad(fox_b_f, (0, LANES - h_b)).reshape(1, LANES),
             fox_w_q=bf(fox_w_q), fox_w_o=bf(fox_w_o),
             ffn_w1=bf(ffn_w1), ffn_w3=bf(ffn_w3), ffn_w2=bf(ffn_w2))

    bsz = x_prompt.shape[0]
    h_a = d // HEAD_A
    shift0 = jnp.zeros((n_a, bsz, d), x_prompt.dtype)
    wkv0 = jnp.zeros((n_a, bsz, h_a, HEAD_A, HEAD_A), F32)
    n_phys, page = cache_k.shape[:2]
    n_slab = h_b // SLAB
    slabs = lambda c: c.reshape(n_phys, page, n_slab, SLAB, HEAD_B)
    key_head_rows = lambda z: jnp.transpose(z.reshape(n_phys, n_slab, SLAB, page), (0, 1, 3, 2)).reshape(
        n_phys, n_slab, 1, page * SLAB)
    suffix, total = _page_suffix(jnp.swapaxes(cache_logf, 1, 2))
    past = (slabs(cache_k), slabs(cache_v), key_head_rows(suffix), key_head_rows(total), page_table)
    y_p, sh_p, wkv_p, k_p, v_p, lf_p = _trunk(x_prompt, shift0, wkv0, None, p)
    y_s, sh_s, wkv_s, k_s, v_s, lf_s = _trunk(x_sample, state_shift, state_wkv, past, p)
    return (y_p, y_s, wkv_p, sh_p, k_p, v_p, lf_p, wkv_s, sh_s, k_s, v_s, lf_s)
```

```python
import functools
import math

import jax
import jax.numpy as jnp
from jax import lax
from jax.experimental import pallas as pl
from jax.experimental.pallas import tpu as pltpu

F32 = jnp.float32
BF16 = jnp.bfloat16

HEAD_A = 64
HEAD_B = 128
LANES = 128
RMS_EPS = 1e-6
LNX_EPS = 64e-5
NEG_BIG = -1e30
LOG2E = math.log2(math.e)
VMEM_LIMIT = 56 * 1024 * 1024


def _cparams(*sem):
    return pltpu.CompilerParams(dimension_semantics=sem, vmem_limit_bytes=VMEM_LIMIT)


def _dot(a, b):
    return jnp.dot(a, b, preferred_element_type=F32)


def _dot_nt(a, b):
    return lax.dot_general(a, b, (((1,), (1,)), ((), ())), preferred_element_type=F32)


def _split(x, n):
    pieces = []
    for _ in range(n - 1):
        hi = x.astype(BF16)
        pieces.append(hi)
        x = x - hi.astype(F32)
    pieces.append(x.astype(BF16))
    return pieces


def _dot_l(x, w, n):
    return sum(_dot(piece, w) for piece in _split(x, n))


def _dot_l_stacked(x, w_stacked, n):
    return _dot(jnp.concatenate(_split(x, n), axis=1), w_stacked)


def _rms(x, g):
    return x * lax.rsqrt(jnp.mean(x * x, axis=-1, keepdims=True) + RMS_EPS) * g


def _softplus(z):
    return jnp.maximum(z, 0.0) + jnp.log1p(jnp.exp(-jnp.abs(z)))


def _row_tile(m):
    return min(512, m)


def _layer_spec(w, layer, block, index, pipeline_mode=None):
    if w.ndim == 2:
        return pl.BlockSpec(block, index, pipeline_mode=pipeline_mode)
    return pl.BlockSpec((None,) + block, lambda *args: (layer,) + index(*args), pipeline_mode=pipeline_mode)


def _rmsnorm_kernel(x_ref, g_ref, o_ref):
    o_ref[...] = _rms(x_ref[...], g_ref[...])


def _rmsnorm(x, g):
    m, d = x.shape
    tm = _row_tile(m)
    return pl.pallas_call(
        _rmsnorm_kernel,
        grid=(m // tm,),
        in_specs=[pl.BlockSpec((tm, d), lambda i: (i, 0)), pl.BlockSpec((1, d), lambda i: (0, 0))],
        out_specs=pl.BlockSpec((tm, d), lambda i: (i, 0)),
        out_shape=jax.ShapeDtypeStruct((m, d), F32),
        compiler_params=_cparams("parallel"),
        name="rmsnorm",
    )(x, g.reshape(1, d))


def _norm_matmul_kernel(x_ref, g_ref, b_ref, *refs, n_w, n_out, epilogue, scale):
    w_refs, o_refs, xn_scr = refs[:n_w], refs[n_w:-1], refs[-1]

    @pl.when(pl.program_id(1) == 0)
    def _():
        xn_scr[...] = _rms(x_ref[...], g_ref[...]).astype(BF16)

    for wi, w_ref in enumerate(w_refs):
        acc = _dot(xn_scr[...], w_ref[...])
        if epilogue == "log_sigmoid":
            acc = -_softplus(-(acc + b_ref[...]))
        elif epilogue == "scale":
            acc = acc * scale
        for o_ref in o_refs[wi * n_out:(wi + 1) * n_out]:
            o_ref[...] = acc.astype(o_ref.dtype)


def _norm_matmul(x, g, ws, out_dtypes, epilogue="none", bias=None, scale=1.0, layer=None):
    m, d = x.shape
    n = ws[0].shape[-1]
    tm, tn = min(1024, m), min(512, n)
    if bias is None:
        bias = jnp.zeros((1, n), F32)
    n_res = len(ws) * len(out_dtypes)
    outs = pl.pallas_call(
        functools.partial(_norm_matmul_kernel, n_w=len(ws), n_out=len(out_dtypes), epilogue=epilogue,
                          scale=scale),
        grid=(m // tm, n // tn),
        in_specs=[pl.BlockSpec((tm, d), lambda i, j: (i, 0)),
                  pl.BlockSpec((1, d), lambda i, j: (0, 0)),
                  pl.BlockSpec((1, tn), lambda i, j: (0, j))]
                 + [_layer_spec(w, layer, (d, tn), lambda i, j: (0, j)) for w in ws],
        out_specs=[pl.BlockSpec((tm, tn), lambda i, j: (i, j)) for _ in range(n_res)],
        out_shape=[jax.ShapeDtypeStruct((m, n), dt) for _ in ws for dt in out_dtypes],
        scratch_shapes=[pltpu.VMEM((tm, d), BF16)],
        compiler_params=_cparams("parallel", "arbitrary"),
        name="norm_matmul_" + epilogue,
    )(x, g.reshape(1, d), bias, *ws)
    return outs


def _proj_ffn_kernel(h_ref, x_ref, wo_ref, g_ref, w1_ref, w3_ref, w2_ref, gy_ref, o_ref, xn_scr, acc_scr, *,
                     final_norm):
    j = pl.program_id(1)

    @pl.when(j == 0)
    def _():
        h1 = h_ref[...] + _dot(x_ref[...], wo_ref[...])
        xn_scr[...] = _rms(h1, g_ref[...]).astype(BF16)
        acc_scr[...] = h1

    x = xn_scr[...]
    a = _dot(x, w1_ref[...])
    b = _dot(x, w3_ref[...])
    hm = (a * jax.nn.sigmoid(a) * b).astype(BF16)
    acc_scr[...] += _dot(hm, w2_ref[...])

    @pl.when(j == pl.num_programs(1) - 1)
    def _():
        o_ref[...] = _rms(acc_scr[...], gy_ref[...]) if final_norm else acc_scr[...]


def _proj_ffn(h, x, w_o, layer_o, g, w1, w3, w2, layer, final_gain=None):
    m, d = h.shape
    k = x.shape[1]
    f = w1.shape[-1]
    tm, tf = _row_tile(m), 512
    final_norm = final_gain is not None
    return pl.pallas_call(
        functools.partial(_proj_ffn_kernel, final_norm=final_norm),
        grid=(m // tm, f // tf),
        in_specs=[pl.BlockSpec((tm, d), lambda i, j: (i, 0)),
                  pl.BlockSpec((tm, k), lambda i, j: (i, 0)),
                  _layer_spec(w_o, layer_o, (k, d), lambda i, j: (0, 0), pl.Buffered(1)),
                  pl.BlockSpec((1, d), lambda i, j: (0, 0)),
                  _layer_spec(w1, layer, (d, tf), lambda i, j: (0, j)),
                  _layer_spec(w3, layer, (d, tf), lambda i, j: (0, j)),
                  _layer_spec(w2, layer, (tf, d), lambda i, j: (j, 0)),
                  pl.BlockSpec((1, d), lambda i, j: (0, 0))],
        out_specs=pl.BlockSpec((tm, d), lambda i, j: (i, 0)),
        out_shape=jax.ShapeDtypeStruct((m, d), F32),
        scratch_shapes=[pltpu.VMEM((tm, d), BF16), pltpu.VMEM((tm, d), F32)],
        compiler_params=_cparams("parallel", "arbitrary"),
        name="proj_ffn",
    )(h, x, w_o, g.reshape(1, d), w1, w3, w2, (final_gain if final_norm else g).reshape(1, d))


def _mix_kernel(h_ref, hp_ref, st_ref, ln_ref, vf_ref, mu_ref, wr_ref, wk_ref, wv_ref,
                w1_ref, w2_ref, a1_ref, a2_ref, v1_ref, v2_ref, g1_ref, g2_ref,
                w0_ref, a0_ref, v0_ref,
                r_ref, lw_ref, k_ref, v_ref, a_ref, g_ref,
                xr_scr, xk_scr, xv_scr, hw_scr, ha_scr, hv_scr, hg_scr, *, has_v, seq_len):
    @pl.when(pl.program_id(1) == 0)
    def _():
        tm = h_ref.shape[0]
        xn = _rms(h_ref[...], ln_ref[...])
        row = lax.broadcasted_iota(jnp.int32, (tm, 1), 0)
        before_tile = _rms(hp_ref[7:8, :], ln_ref[...])
        xprev = jnp.where(row == 0, before_tile, pltpu.roll(xn, 1, axis=0))
        if seq_len >= tm:
            starts_here = pl.program_id(0) % (seq_len // tm) == 0
            xprev = jnp.where((row == 0) & starts_here, st_ref[0, 0:1, :], xprev)
        else:
            for s in range(tm // seq_len):
                xprev = jnp.where(row == s * seq_len, st_ref[0, s:s + 1, :], xprev)
        xx = xprev - xn
        mix = lambda i: (xn + xx * mu_ref[i:i + 1, :]).astype(BF16)
        xr_scr[...] = mix(0)
        hw_scr[...] = jnp.tanh(_dot(mix(1), w1_ref[...])).astype(BF16)
        xk_scr[...] = mix(2)
        xv = mix(3)
        xv_scr[...] = xv
        if has_v:
            hv_scr[...] = _dot(xv, v1_ref[...]).astype(BF16)
        ha_scr[...] = _dot(mix(4), a1_ref[...]).astype(BF16)
        hg_scr[...] = jax.nn.sigmoid(_dot(mix(5), g1_ref[...])).astype(BF16)

    r_ref[...] = _dot(xr_scr[...], wr_ref[...]).astype(r_ref.dtype)
    k_ref[...] = _dot(xk_scr[...], wk_ref[...]).astype(k_ref.dtype)
    v = _dot(xv_scr[...], wv_ref[...])
    if has_v:
        v = v + (vf_ref[...] - v) * jax.nn.sigmoid(v0_ref[...] + _dot(hv_scr[...], v2_ref[...]))
    v_ref[...] = v.astype(v_ref.dtype)
    w_log = -_softplus(-(w0_ref[...] + _dot(hw_scr[...], w2_ref[...]))) - 0.5
    lw_ref[...] = -jnp.exp(w_log)
    a_ref[...] = jax.nn.sigmoid(a0_ref[...] + _dot(ha_scr[...], a2_ref[...])).astype(a_ref.dtype)
    g_ref[...] = _dot(hg_scr[...], g2_ref[...]).astype(g_ref.dtype)


def _mix(h, shift_state, seq_len, ln, vfirst, mu, wr, wk, wv, w1, w2, a1, a2, v1, v2, g1, g2, w0, a0, v0,
         has_v, layer, layer_v):
    m, d = h.shape
    tm, tn = _row_tile(m), 512
    assert seq_len % tm == 0 or tm % seq_len == 0
    n_starts = max(1, tm // seq_len)
    seq_of = [[(i * tm + s * seq_len) // seq_len for s in range(n_starts)] for i in range(m // tm)]
    starts = shift_state[jnp.asarray(seq_of)]
    lw_, la_, lv_, lg_ = w1.shape[-1], a1.shape[-1], v1.shape[-1], g1.shape[-1]
    row = lambda i, j: (i, 0)
    col = lambda i, j: (0, j)
    tile = lambda i, j: (i, j)
    fixed = lambda i, j: (0, 0)
    down = lambda w, lyr: _layer_spec(w, lyr, (d, w.shape[-1]), fixed)
    up = lambda w, lyr: _layer_spec(w, lyr, (w.shape[-2], tn), col)
    outs = pl.pallas_call(
        functools.partial(_mix_kernel, has_v=has_v, seq_len=seq_len),
        grid=(m // tm, d // tn),
        in_specs=[pl.BlockSpec((tm, d), row),
                  pl.BlockSpec((8, d), lambda i, j: (jnp.maximum(i * (tm // 8) - 1, 0), 0)),
                  pl.BlockSpec((1, n_starts, d), lambda i, j: (i, 0, 0)),
                  pl.BlockSpec((1, d), fixed),
                  pl.BlockSpec((tm, tn), tile),
                  pl.BlockSpec((8, d), fixed),
                  up(wr, layer), up(wk, layer), up(wv, layer),
                  down(w1, layer), up(w2, layer), down(a1, layer), up(a2, layer),
                  down(v1, layer_v), up(v2, layer_v), down(g1, layer), up(g2, layer),
                  pl.BlockSpec((1, tn), col), pl.BlockSpec((1, tn), col), pl.BlockSpec((1, tn), col)],
        out_specs=[pl.BlockSpec((tm, tn), tile) for _ in range(6)],
        out_shape=[jax.ShapeDtypeStruct((m, d), dt) for dt in (BF16, F32, BF16, BF16, BF16, BF16)],
        scratch_shapes=[pltpu.VMEM((tm, d), BF16), pltpu.VMEM((tm, d), BF16), pltpu.VMEM((tm, d), BF16),
                        pltpu.VMEM((tm, lw_), BF16), pltpu.VMEM((tm, la_), BF16),
                        pltpu.VMEM((tm, lv_), BF16), pltpu.VMEM((tm, lg_), BF16)],
        compiler_params=_cparams("parallel", "arbitrary"),
        name="rwkv_mix_proj",
    )(h, h, starts, ln.reshape(1, d), vfirst, mu, wr, wk, wv, w1, w2, a1, a2, v1, v2, g1, g2, w0, a0, v0)
    return outs


SCAN_CHUNK = 64
SCAN_PAIRS = 16


def _scan_kernel(r_ref, lw_ref, kr_ref, v_ref, a_ref, g_ref,
                 kk_ref, ka_ref, rk_ref, lnw_ref, lnb_ref, s0_ref,
                 y_ref, sout_ref, st_scr):
    c, n_chunks = pl.program_id(2), pl.num_programs(2)
    C = SCAN_CHUNK
    R = 2 * C
    n_pairs = st_scr.shape[0]
    f32 = lambda ref, sl: ref[0, :, sl].astype(F32)

    @pl.when(c == 0)
    def _():
        st_scr[...] = s0_ref[0]

    lane_c = lax.broadcasted_iota(jnp.int32, (C, LANES), 1)
    head0 = lane_c < HEAD_A
    row = lax.broadcasted_iota(jnp.int32, (R, R), 0)
    col = lax.broadcasted_iota(jnp.int32, (R, R), 1)
    same = (row // C) == (col // C)
    strict = same & ((col % C) < (row % C))
    incl = same & ((col % C) <= (row % C))
    row2 = lax.broadcasted_iota(jnp.int32, (2 * LANES, LANES), 0)
    col2 = lax.broadcasted_iota(jnp.int32, (2 * LANES, LANES), 1)
    ones2 = (((row2 % LANES) // HEAD_A) == (col2 // HEAD_A)).astype(BF16)
    head_sum = lambda x: _dot_l_stacked(x, ones2, 2)
    ti = lax.broadcasted_iota(jnp.int32, (C, 3 * C), 0)
    tj = lax.broadcasted_iota(jnp.int32, (C, 3 * C), 1)
    tri3 = ((tj % C) <= ti).astype(BF16)
    zeros_c = jnp.zeros((C, LANES), F32)

    def stack(x):
        return jnp.concatenate([jnp.where(head0, x, 0.0), jnp.where(head0, 0.0, x)], axis=0)

    sls = [slice(p * LANES, (p + 1) * LANES) for p in range(n_pairs)]
    each = lambda f, *cols: [f(*xs) for xs in zip(*cols)]
    bf = lambda xs: [x.astype(BF16) for x in xs]

    r = [f32(r_ref, sl) for sl in sls]
    lw = [lw_ref[0, :, sl] for sl in sls]
    kr = [f32(kr_ref, sl) for sl in sls]
    v = [f32(v_ref, sl) for sl in sls]
    a = [f32(a_ref, sl) for sl in sls]

    kkr = each(lambda x, sl: x * kk_ref[:, sl], kr, sls)
    ss = each(lambda x: head_sum(x * x), kkr)
    kk = each(lambda x, s: x / jnp.maximum(jnp.sqrt(s), 1e-12), kkr, ss)
    k = each(lambda x, y, sl: x * (1.0 + (y - 1.0) * ka_ref[:, sl]), kr, a, sls)
    b = each(lambda x, y: x * y, kk, a)

    cum = each(lambda x: _dot(tri3, jnp.concatenate(_split(x, 3), axis=0)), lw)
    g_in = each(jnp.exp, cum)
    g_ex = each(lambda x, y: jnp.exp(x - y), cum, lw)
    g_inv = each(lambda x: jnp.exp(-x), cum)
    g_tail = each(lambda x: jnp.exp(x[C - 1:C, :] - x), cum)

    at = each(lambda x, y: stack(-x * y), kk, g_ex)
    at_b = bf(at)
    rt = bf(each(lambda x, y: stack(x * y), r, g_in))
    bt = bf(each(lambda x, y: stack(x * y), b, g_inv))
    kt = bf(each(lambda x, y: stack(x * y), k, g_inv))
    bh = each(lambda x, y: stack(x * y), b, g_tail)
    kh = each(lambda x, y: stack(x * y), k, g_tail)
    vs = bf(each(stack, v))

    sc = each(lambda w, x, y, z: _dot_nt(jnp.concatenate([w, x], axis=0), jnp.concatenate([y, z], axis=0)),
              at_b, rt, bt, kt)
    a_ab = each(lambda x: jnp.where(strict, x[:R, :R], 0.0), sc)
    a_ak = bf(each(lambda x: jnp.where(strict, x[:R, R:], 0.0), sc))
    a_rb = bf(each(lambda x: jnp.where(incl, x[R:, :R], 0.0), sc))
    a_rk = bf(each(lambda x: jnp.where(incl, x[R:, R:], 0.0), sc))

    x = each(lambda w, y, z: jnp.concatenate([w, _dot(y, z)], axis=1), at, a_ak, vs)
    pw = a_ab
    n_sq = int(math.log2(C))
    for it in range(n_sq):
        pw_b = bf(pw)
        x = each(lambda y, z: y + _dot(z, y.astype(BF16)), x, pw_b)
        if it + 1 < n_sq:
            pw = each(lambda z: _dot(z, z), pw_b)
    a_hat = bf(each(lambda y: y[:, :LANES], x))
    u_bar = each(lambda y: y[:, LANES:], x)

    st = [st_scr[p] for p in range(n_pairs)]
    st_b = bf(st)
    u = each(lambda w, y, z: _dot(w, y) + z, a_hat, st_b, u_bar)
    u_b = bf(u)
    uv = each(lambda y, t: jnp.concatenate([y, t], axis=0), u_b, vs)
    o_st = each(lambda q, w, z, s, y: _dot(jnp.concatenate([q, w, z], axis=1), jnp.concatenate([s, y], axis=0)),
                rt, a_rb, a_rk, st_b, uv)
    o = each(lambda y: y[:C] + y[C:], o_st)

    def decay_col(x):
        x_t = jnp.concatenate([x, zeros_c], axis=0).T if C < LANES else x.T
        return jnp.exp(jnp.sum(x_t, axis=1, keepdims=True))

    dcol = each(decay_col, lw)
    st_new = each(lambda dc, s, w, z, y: dc * s + _dot(jnp.concatenate([w.T, z.T], axis=1).astype(BF16), y),
                  dcol, st, bh, kh, uv)
    for p in range(n_pairs):
        st_scr[p] = st_new[p]

    mean = each(lambda y: head_sum(y) * (1.0 / HEAD_A), o)
    dlt = each(lambda y, z: y - z, o, mean)
    var = each(lambda y: head_sum(y * y) * (1.0 / HEAD_A), dlt)
    on = each(lambda y, z, sl: y * lax.rsqrt(z + LNX_EPS) * lnw_ref[:, sl] + lnb_ref[:, sl], dlt, var, sls)
    bonus = each(lambda x, y, z, sl: head_sum(x * y * rk_ref[:, sl]) * z, r, k, v, sls)
    for p, sl in enumerate(sls):
        y_ref[0, :, sl] = ((on[p] + bonus[p]) * f32(g_ref, sl)).astype(y_ref.dtype)

    @pl.when(c == n_chunks - 1)
    def _():
        sout_ref[0] = st_scr[...]


def _scan(r, lw, kr, v, a, g, k_k, k_a, r_k, lnx_w, lnx_b, s0_bd):
    bsz, t, d = r.shape
    C = SCAN_CHUNK
    P = min(SCAN_PAIRS, d // LANES)
    w = P * LANES
    n_pg = d // w
    seq = pl.BlockSpec((1, C, w), lambda b, q, c: (b, c, q))
    vec = pl.BlockSpec((1, w), lambda b, q, c: (0, q))
    sbd = pl.BlockSpec((1, P, LANES, LANES), lambda b, q, c: (b, q, 0, 0))
    y, s_out = pl.pallas_call(
        _scan_kernel,
        grid=(bsz, n_pg, t // C),
        in_specs=[seq] * 6 + [vec] * 5 + [sbd],
        out_specs=[seq, sbd],
        out_shape=[jax.ShapeDtypeStruct((bsz, t, d), BF16),
                   jax.ShapeDtypeStruct(s0_bd.shape, F32)],
        scratch_shapes=[pltpu.VMEM((P, LANES, LANES), F32)],
        compiler_params=_cparams("parallel", "parallel", "arbitrary"),
        name="wkv7_scan",
    )(r, lw, kr, v, a, g, k_k.reshape(1, d), k_a.reshape(1, d), r_k.reshape(1, d),
      lnx_w.reshape(1, d), lnx_b.reshape(1, d), s0_bd)
    return y, s_out


def _state_to_bd(s):
    bsz, h = s.shape[:2]
    st = jnp.swapaxes(s, 2, 3).reshape(bsz, h // 2, 2, HEAD_A, HEAD_A)
    eye = jnp.eye(2, dtype=s.dtype)
    bd = st[:, :, :, :, None, :] * eye[None, None, :, None, :, None]
    return bd.reshape(bsz, h // 2, 2 * HEAD_A, 2 * HEAD_A)


def _bd_to_state(bd):
    bsz, hp = bd.shape[:2]
    x = bd.reshape(bsz, hp, 2, HEAD_A, 2, HEAD_A)
    st = jnp.stack([x[:, :, 0, :, 0, :], x[:, :, 1, :, 1, :]], axis=2)
    return jnp.swapaxes(st.reshape(bsz, hp * 2, HEAD_A, HEAD_A), 2, 3)


N_PIECES = 3


def _forget_bias_kernel(x_ref, eq_ref, ek_ref, qa_ref, ka_ref, carry_scr):
    @pl.when(pl.program_id(1) == 0)
    def _():
        carry_scr[...] = jnp.zeros_like(carry_scr)

    tt = x_ref.shape[1]
    ti = lax.broadcasted_iota(jnp.int32, (tt, N_PIECES * tt), 0)
    tj = lax.broadcasted_iota(jnp.int32, (tt, N_PIECES * tt), 1)
    tri = ((tj % tt) <= ti).astype(BF16)
    c = _dot(tri, jnp.concatenate(_split(x_ref[0], N_PIECES), axis=0)) + carry_scr[...]
    carry_scr[...] = c[tt - 1:tt, :]
    pieces = jnp.concatenate(_split(c * LOG2E, N_PIECES), axis=1)
    lane = lax.broadcasted_iota(jnp.int32, qa_ref.shape[1:], 1) % HEAD_B
    qa = _dot(pieces, eq_ref[...]) + jnp.where((lane >= N_PIECES) & (lane < 2 * N_PIECES), 1.0, 0.0)
    ka = _dot(pieces, ek_ref[...]) + jnp.where(lane < N_PIECES, 1.0, 0.0)
    qa_ref[0] = qa.astype(BF16)
    ka_ref[0] = ka.astype(BF16)


def _forget_bias_lanes(logf_lanes, n_heads):
    bsz, t, _ = logf_lanes.shape
    tt = min(512, t)
    d = n_heads * HEAD_B
    src = jnp.arange(N_PIECES * LANES)
    piece, head = src // LANES, src % LANES
    place = lambda lane0: (jnp.arange(d)[None, :] == (head * HEAD_B + lane0 + piece)[:, None]) & (
        head < n_heads)[:, None]
    eq = place(0).astype(BF16)
    ek = -place(N_PIECES).astype(BF16)
    out = pl.BlockSpec((1, tt, d), lambda b, i: (b, i, 0))
    return pl.pallas_call(
        _forget_bias_kernel,
        grid=(bsz, t // tt),
        in_specs=[pl.BlockSpec((1, tt, LANES), lambda b, i: (b, i, 0)),
                  pl.BlockSpec(eq.shape, lambda b, i: (0, 0)), pl.BlockSpec(ek.shape, lambda b, i: (0, 0))],
        out_specs=[out, out],
        out_shape=[jax.ShapeDtypeStruct((bsz, t, d), BF16)] * 2,
        scratch_shapes=[pltpu.VMEM((1, LANES), F32)],
        compiler_params=_cparams("parallel", "arbitrary"),
        name="forget_bias_lanes",
    )(logf_lanes, eq, ek)


def _flash_kernel(q_ref, qa_ref, k_ref, ka_ref, v_ref, o_ref, m_scr, acc_scr):
    i, j = pl.program_id(1), pl.program_id(2)
    tq, tk = q_ref.shape[1], k_ref.shape[1]
    n_heads = q_ref.shape[2] // HEAD_B

    @pl.when(j == 0)
    def _():
        m_scr[...] = jnp.full_like(m_scr, NEG_BIG)
        acc_scr[...] = jnp.zeros_like(acc_scr)

    def step(masked):
        if masked:
            qpos = lax.broadcasted_iota(jnp.int32, (tq, tk), 0)
            kpos = lax.broadcasted_iota(jnp.int32, (tq, tk), 1)
            keep = kpos <= qpos
        ones = jnp.ones((tk, HEAD_B), BF16)
        for h in range(n_heads):
            hs = slice(h * HEAD_B, (h + 1) * HEAD_B)
            q_aug = jnp.concatenate([q_ref[0, :, hs], qa_ref[0, :, hs]], axis=1)
            k_aug = jnp.concatenate([k_ref[0, :, hs], ka_ref[0, :, hs]], axis=1)
            s = _dot_nt(q_aug, k_aug)
            if masked:
                s = jnp.where(keep, s, NEG_BIG)
            m_prev = m_scr[h]
            m_next = jnp.maximum(m_prev, jnp.max(s, axis=1, keepdims=True))
            alpha = jnp.exp2(m_prev - m_next)
            p = jnp.exp2(s - jnp.concatenate([m_next] * (tk // LANES), axis=1))
            v_aug = jnp.concatenate([v_ref[0, :, hs], ones], axis=1)
            acc_scr[h] = jnp.concatenate([alpha, alpha], axis=1) * acc_scr[h] + _dot(p.astype(BF16), v_aug)
            m_scr[h] = m_next

    @pl.when(j < i)
    def _():
        step(False)

    @pl.when(j == i)
    def _():
        step(True)
        for h in range(n_heads):
            hs = slice(h * HEAD_B, (h + 1) * HEAD_B)
            acc = acc_scr[h]
            o_ref[0, :, hs] = (acc[:, :HEAD_B] / acc[:, HEAD_B:]).astype(o_ref.dtype)


def _flash_attention(q, q_aug, k, k_aug, v):
    bsz, t, d = q.shape
    h = d // HEAD_B
    tq = min(512, t)
    nq = t // tq
    q_idx = lambda b, i, j: (b, i, 0)
    kv_idx = lambda b, i, j: (b, jnp.minimum(j, i), 0)
    return pl.pallas_call(
        _flash_kernel,
        grid=(bsz, nq, nq),
        in_specs=[pl.BlockSpec((1, tq, d), q_idx), pl.BlockSpec((1, tq, d), q_idx),
                  pl.BlockSpec((1, tq, d), kv_idx), pl.BlockSpec((1, tq, d), kv_idx),
                  pl.BlockSpec((1, tq, d), kv_idx)],
        out_specs=pl.BlockSpec((1, tq, d), q_idx),
        out_shape=jax.ShapeDtypeStruct((bsz, t, d), BF16),
        scratch_shapes=[pltpu.VMEM((h, tq, LANES), F32), pltpu.VMEM((h, tq, 2 * HEAD_B), F32)],
        compiler_params=_cparams("parallel", "parallel", "arbitrary"),
        name="fox_prompt_attention",
    )(q, q_aug, k, k_aug, v)


PAGES_PER_STEP = 8
SLAB = 8


def _page_suffix_kernel(lf_ref, suf_ref, tot_ref):
    pb, h, page = lf_ref.shape
    u = lax.broadcasted_iota(jnp.int32, (page, page), 0)
    s = lax.broadcasted_iota(jnp.int32, (page, page), 1)
    lf = lf_ref[...].reshape(pb * h, page) * LOG2E
    suf_ref[...] = _dot_l(lf, (u > s).astype(BF16), 3).reshape(pb, h, page)
    tot_ref[...] = _dot_l(lf, jnp.ones((page, page), BF16), 3).reshape(pb, h, page)


def _page_suffix(lf_t):
    n_phys, h, page = lf_t.shape
    pb = math.gcd(n_phys, 64)
    spec = pl.BlockSpec((pb, h, page), lambda i: (i, 0, 0))
    return pl.pallas_call(
        _page_suffix_kernel,
        grid=(n_phys // pb,),
        in_specs=[spec],
        out_specs=[spec, spec],
        out_shape=[jax.ShapeDtypeStruct(lf_t.shape, F32)] * 2,
        compiler_params=_cparams("parallel"),
        name="page_logf_suffix",
    )(lf_t)


def _paged_kernel(pt_ref, q_ref, kn_ref, vn_ref, gn_ref, *refs, n_group, n_slab):
    del pt_ref
    p, n_steps = pl.program_id(1), pl.num_programs(1)
    n_kv = n_group * n_slab
    kp_refs, vp_refs = refs[:n_kv], refs[n_kv:2 * n_kv]
    suf_refs, tot_refs = refs[2 * n_kv:3 * n_kv], refs[3 * n_kv:4 * n_kv]
    o_ref, m_scr, l_scr, acc_scr, cq_scr, carry_scr = refs[4 * n_kv:]
    rows = q_ref.shape[1]
    n_heads = kn_ref.shape[1]
    nq = rows // n_heads
    page = kn_ref.shape[2]
    srows = SLAB * nq
    flat = page * SLAB

    def update(sl, s, pv_of):
        m_old = m_scr[sl]
        m_new = jnp.maximum(m_old, jnp.max(s, axis=1, keepdims=True))
        alpha = jnp.exp2(m_old - m_new)
        pr = jnp.exp2(s - m_new)
        l_scr[sl] = alpha * l_scr[sl] + jnp.sum(pr, axis=1, keepdims=True)
        acc_scr[sl] = alpha * acc_scr[sl] + pv_of(pr)
        m_scr[sl] = m_new

    @pl.when(p == 0)
    def _():
        m_scr[...] = jnp.full_like(m_scr, NEG_BIG)
        l_scr[...] = jnp.zeros_like(l_scr)
        acc_scr[...] = jnp.zeros_like(acc_scr)
        carry_scr[...] = jnp.zeros_like(carry_scr)
        row = lax.broadcasted_iota(jnp.int32, (rows, page), 0)
        lane = lax.broadcasted_iota(jnp.int32, (rows, page), 1)
        u = lax.broadcasted_iota(jnp.int32, (page, page), 0)
        s_ = lax.broadcasted_iota(jnp.int32, (page, page), 1)
        gn = gn_ref[0] * LOG2E
        qi = row % nq
        c_new_col = jnp.sum(jnp.where(lane <= qi, gn, 0.0), axis=1, keepdims=True)
        hi, mid, lo = [x.astype(F32) for x in _split(c_new_col, 3)]
        lane_a = lax.broadcasted_iota(jnp.int32, (rows, LANES), 1)
        cq_scr[...] = jnp.where(lane_a == 0, hi, jnp.where(lane_a == 1, mid,
                                                           jnp.where(lane_a == 2, lo, 0.0))).astype(BF16)
        c_new_row = _dot_l(gn, (u <= s_).astype(BF16), 3)
        s = jnp.concatenate([_dot_nt(q_ref[0, h * nq:(h + 1) * nq, :], kn_ref[0, h])
                             for h in range(n_heads)], axis=0) + (c_new_col - c_new_row)
        s = jnp.where(lane <= qi, s, NEG_BIG)
        for c in range(n_slab):
            sl = slice(c * srows, (c + 1) * srows)
            update(sl, s[sl], lambda pr, c=c: jnp.concatenate(
                [_dot(pr[hh * nq:(hh + 1) * nq, :].astype(BF16), vn_ref[0, c * SLAB + hh])
                 for hh in range(SLAB)], axis=0))

    row_f = lax.broadcasted_iota(jnp.int32, (srows, flat), 0)
    lane_f = lax.broadcasted_iota(jnp.int32, (srows, flat), 1)
    same_head = (row_f // nq) == (lane_f % SLAB)
    lane_k = lax.broadcasted_iota(jnp.int32, (flat, LANES), 1)
    ones_k = (lane_k < 3).astype(BF16)
    slabs = range(n_slab)
    sls = [slice(c * srows, (c + 1) * srows) for c in slabs]
    q_aug = [jnp.concatenate([q_ref[0, sl, :], cq_scr[sl, :]], axis=1) for sl in sls]
    carry = [carry_scr[c:c + 1, :] for c in slabs]
    s = [[] for _ in slabs]
    for g in range(n_group):
        for c in slabs:
            k_rows = kp_refs[g * n_slab + c][...].reshape(flat, HEAD_B).astype(BF16)
            z = _dot_nt(q_aug[c], jnp.concatenate([k_rows, ones_k], axis=1))
            s[c].append(jnp.where(same_head, z + (suf_refs[g * n_slab + c][...] + carry[c]), NEG_BIG))
            carry[c] = carry[c] + tot_refs[g * n_slab + c][...]
    s = [jnp.concatenate(x, axis=1) for x in s]
    m_old = [m_scr[sl] for sl in sls]
    m_new = [jnp.maximum(mo, jnp.max(x, axis=1, keepdims=True)) for mo, x in zip(m_old, s)]
    alpha = [jnp.exp2(mo - mn) for mo, mn in zip(m_old, m_new)]
    pr = [jnp.exp2(x - mn) for x, mn in zip(s, m_new)]
    v_rows = [jnp.concatenate([vp_refs[g * n_slab + c][...].reshape(flat, HEAD_B)
                               for g in range(n_group)], axis=0).astype(BF16) for c in slabs]
    pv = [_dot(x.astype(BF16), vr) for x, vr in zip(pr, v_rows)]
    for c, sl in enumerate(sls):
        carry_scr[c:c + 1, :] = carry[c]
        l_scr[sl] = alpha[c] * l_scr[sl] + jnp.sum(pr[c], axis=1, keepdims=True)
        acc_scr[sl] = alpha[c] * acc_scr[sl] + pv[c]
        m_scr[sl] = m_new[c]

    @pl.when(p == n_steps - 1)
    def _():
        o_ref[0] = acc_scr[...] / l_scr[...]


def _paged_attention(page_table, q, k_new, v_new, g_new, cache_k, cache_v, suffix, total):
    dbs, rows, _ = q.shape
    n_heads = k_new.shape[1]
    n_slab = n_heads // SLAB
    n_pages = page_table.shape[1]
    page = k_new.shape[2]
    flat = page * SLAB
    n_group = math.gcd(PAGES_PER_STEP, n_pages)
    per_b = lambda b, p, pt: (b, 0, 0)
    per_b4 = lambda b, p, pt: (b, 0, 0, 0)
    slot_page = lambda b, p, pt, g: pt[b, n_pages - 1 - (p * n_group + g)]
    slots = [(g, c) for g in range(n_group) for c in range(n_slab)]
    kv_specs = [pl.BlockSpec((None, page, None, SLAB, HEAD_B),
                             lambda b, p, pt, g=g, c=c: (slot_page(b, p, pt, g), 0, c, 0, 0)) for g, c in slots]
    row_specs = [pl.BlockSpec((None, None, 1, flat),
                              lambda b, p, pt, g=g, c=c: (slot_page(b, p, pt, g), c, 0, 0)) for g, c in slots]
    n_kv = len(slots)
    grid_spec = pltpu.PrefetchScalarGridSpec(
        num_scalar_prefetch=1,
        grid=(dbs, n_pages // n_group),
        in_specs=[pl.BlockSpec((1, rows, HEAD_B), per_b),
                  pl.BlockSpec((1, n_heads, page, HEAD_B), per_b4),
                  pl.BlockSpec((1, n_heads, page, HEAD_B), per_b4),
                  pl.BlockSpec((1, rows, page), per_b)] + kv_specs + kv_specs + row_specs + row_specs,
        out_specs=pl.BlockSpec((1, rows, HEAD_B), per_b),
        scratch_shapes=[pltpu.VMEM((rows, 1), F32), pltpu.VMEM((rows, 1), F32),
                        pltpu.VMEM((rows, HEAD_B), F32), pltpu.VMEM((rows, LANES), BF16),
                        pltpu.VMEM((8, flat), F32)])
    return pl.pallas_call(
        functools.partial(_paged_kernel, n_group=n_group, n_slab=n_slab),
        grid_spec=grid_spec,
        out_shape=jax.ShapeDtypeStruct((dbs, rows, HEAD_B), F32),
        compiler_params=_cparams("parallel", "arbitrary"),
        name="fox_paged_attention",
    )(page_table, q, k_new, v_new, g_new, *([cache_k] * n_kv), *([cache_v] * n_kv),
      *([suffix] * n_kv), *([total] * n_kv))


def _trunk(x, shift0, wkv0, past, p):
    bsz, t, d = x.shape
    m = bsz * t
    n_a = p["tm_w_r"].shape[0]
    depth = p["ln1"].shape[0]
    h_b = d // HEAD_B
    h = x.reshape(m, d)
    v_first = None
    shifts, states = [], []
    t_scan = -(-t // SCAN_CHUNK) * SCAN_CHUNK
    heads_first = lambda z: jnp.swapaxes(z.reshape(bsz, t, h_b, HEAD_B), 1, 2)
    for l in range(depth):
        if l == n_a:
            k_new, k_bf, v_new, v_bf = _norm_matmul(h, p["kv_norm"], [p["fox_w_k"], p["fox_w_v"]], (F32, BF16))
            (lf_pad,) = _norm_matmul(h, p["kv_norm"], [p["fox_w_f"]], (F32,), epilogue="log_sigmoid",
                                     bias=p["fox_b_f"])
            logf_new = lf_pad[:, :h_b].reshape(bsz, t, h_b)
            if past is None:
                q_aug, k_aug = _forget_bias_lanes(lf_pad.reshape(bsz, t, LANES), h_b)
            else:
                page = past[0].shape[1]
                pad_keys = lambda z: jnp.pad(heads_first(z), ((0, 0), (0, 0), (0, page - t), (0, 0)))
                k_pad, v_pad = pad_keys(k_bf), pad_keys(v_bf)
                g_new = jnp.repeat(jnp.swapaxes(logf_new, 1, 2), t, axis=1)
                g_new = jnp.pad(g_new, ((0, 0), (0, 0), (0, page - t)))
        if l < n_a:
            shifts.append(_rmsnorm(h.reshape(bsz, t, d)[:, -1], p["ln1"][l]))
            has_v = l > 0
            lv = max(l - 1, 0)
            r, lw, kr, v, a, g = _mix(
                h, shift0[l], t, p["ln1"][l], v_first if has_v else h, p["mu8"][l],
                p["tm_w_r"], p["tm_w_k"], p["tm_w_v"],
                p["tm_w1"], p["tm_w2"], p["tm_a1"], p["tm_a2"],
                p["tm_v1"], p["tm_v2"], p["tm_g1"], p["tm_g2"],
                p["tm_w0"][l].reshape(1, d), p["tm_a0"][l].reshape(1, d), p["tm_v0"][lv].reshape(1, d),
                has_v, l, lv)
            if l == 0:
                v_first = v
            seq = lambda z: jnp.pad(z.reshape(bsz, t, d), ((0, 0), (0, t_scan - t), (0, 0)))
            y, s_bd = _scan(seq(r), seq(lw), seq(kr), seq(v), seq(a), seq(g),
                            p["tm_k_k"][l], p["tm_k_a"][l], p["tm_r_k"][l].reshape(d),
                            p["tm_lnx_w"][l], p["tm_lnx_b"][l], _state_to_bd(wkv0[l]))
            states.append(_bd_to_state(s_bd))
            mixed, w_o, layer_o = y[:, :t].reshape(m, d), p["tm_w_o"], l
        else:
            j = l - n_a
            (q,) = _norm_matmul(h, p["ln1"][l], [p["fox_w_q"]], (BF16,), epilogue="scale",
                                scale=HEAD_B ** -0.5 * LOG2E, layer=j)
            if past is None:
                att = _flash_attention(q.reshape(bsz, t, d), q_aug, k_bf.reshape(bsz, t, d), k_aug,
                                       v_bf.reshape(bsz, t, d)).reshape(m, d)
            else:
                cache_k, cache_v, suffix, total, page_table = past
                o = _paged_attention(page_table, heads_first(q).reshape(bsz, h_b * t, HEAD_B),
                                     k_pad, v_pad, g_new, cache_k, cache_v, suffix, total)
                att = jnp.swapaxes(o.reshape(bsz, h_b, t, HEAD_B), 1, 2).reshape(m, d).astype(BF16)
            mixed, w_o, layer_o = att, p["fox_w_o"], j
        h = _proj_ffn(h, mixed, w_o, layer_o, p["ln2"][l], p["ffn_w1"], p["ffn_w3"], p["ffn_w2"], l,
                      final_gain=p["ln_out"] if l == depth - 1 else None)
    h_sh = (bsz, t, h_b, HEAD_B)
    return (h.reshape(bsz, t, d), jnp.stack(shifts), jnp.stack(states),
            k_new.reshape(h_sh), v_new.reshape(h_sh), logf_new)


def kernel(x_prompt, x_sample, state_wkv, state_shift, cache_k, cache_v, cache_logf, page_table, ln1, ln2, ln_out, tm_mu, tm_w_r, tm_w_k, tm_w_v, tm_w_o, tm_w0, tm_w1, tm_w2, tm_a0, tm_a1, tm_a2, tm_v0, tm_v1, tm_v2, tm_g1, tm_g2, tm_k_k, tm_k_a, tm_r_k, tm_lnx_w, tm_lnx_b, kv_norm, fox_w_k, fox_w_v, fox_w_f, fox_b_f, fox_w_q, fox_w_o, ffn_w1, ffn_w3, ffn_w2):
    bf = lambda w: w.astype(BF16)
    n_a, d = tm_w0.shape
    h_b = d // HEAD_B
    lora = lambda w1, w2: (bf(jnp.pad(w1, ((0, 0), (0, 0), (0, -w1.shape[2] % LANES)))),
                           bf(jnp.pad(w2, ((0, 0), (0, -w2.shape[1] % LANES), (0, 0)))))
    w1, w2 = lora(tm_w1, tm_w2)
    a1, a2 = lora(tm_a1, tm_a2)
    if tm_v1.shape[0] == 0:
        tm_v0 = jnp.zeros((1, d), F32)
        tm_v1 = jnp.zeros((1, d, LANES), F32)
        tm_v2 = jnp.zeros((1, LANES, d), F32)
    v1, v2 = lora(tm_v1, tm_v2)
    g1, g2 = lora(tm_g1, tm_g2)
    p = dict(ln1=ln1, ln2=ln2, ln_out=ln_out,
             mu8=jnp.pad(tm_mu, ((0, 0), (0, 2), (0, 0))),
             tm_w_r=bf(tm_w_r), tm_w_k=bf(tm_w_k), tm_w_v=bf(tm_w_v), tm_w_o=bf(tm_w_o),
             tm_w0=tm_w0, tm_w1=w1, tm_w2=w2, tm_a0=tm_a0, tm_a1=a1, tm_a2=a2,
             tm_v0=tm_v0, tm_v1=v1, tm_v2=v2, tm_g1=g1, tm_g2=g2,
             tm_k_k=tm_k_k, tm_k_a=tm_k_a, tm_r_k=tm_r_k, tm_lnx_w=tm_lnx_w, tm_lnx_b=tm_lnx_b,
             kv_norm=kv_norm, fox_w_k=bf(fox_w_k), fox_w_v=bf(fox_w_v),
             fox_w_f=bf(jnp.pad(fox_w_f, ((0, 0), (0, LANES - h_b)))),
             fox_b_f=jnp.pad(fox_b_f, (0, LANES - h_b)).reshape(1, LANES),
             fox_w_q=bf(fox_w_q), fox_w_o=bf(fox_w_o),
             ffn_w1=bf(ffn_w1), ffn_w3=bf(ffn_w3), ffn_w2=bf(ffn_w2))

    bsz = x_prompt.shape[0]
    h_a = d // HEAD_A
    shift0 = jnp.zeros((n_a, bsz, d), x_prompt.dtype)
    wkv0 = jnp.zeros((n_a, bsz, h_a, HEAD_A, HEAD_A), F32)
    n_phys, page = cache_k.shape[:2]
    n_slab = h_b // SLAB
    slabs = lambda c: c.reshape(n_phys, page, n_slab, SLAB, HEAD_B)
    key_head_rows = lambda z: jnp.transpose(z.reshape(n_phys, n_slab, SLAB, page), (0, 1, 3, 2)).reshape(
        n_phys, n_slab, 1, page * SLAB)
    suffix, total = _page_suffix(jnp.swapaxes(cache_logf, 1, 2))
    past = (slabs(cache_k), slabs(cache_v), key_head_rows(suffix), key_head_rows(total), page_table)
    y_p, sh_p, wkv_p, k_p, v_p, lf_p = _trunk(x_prompt, shift0, wkv0, None, p)
    y_s, sh_s, wkv_s, k_s, v_s, lf_s = _trunk(x_sample, state_shift, state_wkv, past, p)
    return (y_p, y_s, wkv_p, sh_p, k_p, v_p, lf_p, wkv_s, sh_s, k_s, v_s, lf_s)
```

```python
import functools
import math

import jax
import jax.numpy as jnp
from jax import lax
from jax.experimental import pallas as pl
from jax.experimental.pallas import tpu as pltpu

F32 = jnp.float32
BF16 = jnp.bfloat16

HEAD_A = 64
HEAD_B = 128
LANES = 128
RMS_EPS = 1e-6
LNX_EPS = 64e-5
NEG_BIG = -1e30
LOG2E = math.log2(math.e)
VMEM_LIMIT = 56 * 1024 * 1024

ROW_TILE = 512
WIDE_ROW_TILE = 1024
COL_TILE = 512


def _cparams(*sem):
    return pltpu.CompilerParams(dimension_semantics=sem, vmem_limit_bytes=VMEM_LIMIT)


def _dot(a, b):
    return jnp.dot(a, b, preferred_element_type=F32)


def _dot_nt(a, b):
    return lax.dot_general(a, b, (((1,), (1,)), ((), ())), preferred_element_type=F32)


def _split(x, n):
    pieces = []
    for _ in range(n - 1):
        hi = x.astype(BF16)
        pieces.append(hi)
        x = x - hi.astype(F32)
    pieces.append(x.astype(BF16))
    return pieces


def _dot_l(x, w, n):
    return sum(_dot(piece, w) for piece in _split(x, n))


def _dot_l_stacked(x, w_stacked, n):
    return _dot(jnp.concatenate(_split(x, n), axis=1), w_stacked)


def _rms(x, g):
    return x * lax.rsqrt(jnp.mean(x * x, axis=-1, keepdims=True) + RMS_EPS) * g


def _softplus(z):
    return jnp.maximum(z, 0.0) + jnp.log1p(jnp.exp(-jnp.abs(z)))


def _row_tile(m):
    return min(ROW_TILE, m)


def _layer_spec(w, layer, block, index, pipeline_mode=None):
    if w.ndim == 2:
        return pl.BlockSpec(block, index, pipeline_mode=pipeline_mode)
    return pl.BlockSpec((None,) + block, lambda *args: (layer,) + index(*args), pipeline_mode=pipeline_mode)


def _rmsnorm_kernel(x_ref, g_ref, o_ref):
    o_ref[...] = _rms(x_ref[...], g_ref[...])


def _rmsnorm(x, g):
    m, d = x.shape
    tm = _row_tile(m)
    return pl.pallas_call(
        _rmsnorm_kernel,
        grid=(m // tm,),
        in_specs=[pl.BlockSpec((tm, d), lambda i: (i, 0)), pl.BlockSpec((1, d), lambda i: (0, 0))],
        out_specs=pl.BlockSpec((tm, d), lambda i: (i, 0)),
        out_shape=jax.ShapeDtypeStruct((m, d), F32),
        compiler_params=_cparams("parallel"),
        name="rmsnorm",
    )(x, g.reshape(1, d))


def _norm_matmul_kernel(x_ref, g_ref, b_ref, *refs, n_w, n_out, epilogue, scale):
    w_refs, o_refs, xn_scr = refs[:n_w], refs[n_w:-1], refs[-1]

    @pl.when(pl.program_id(1) == 0)
    def _():
        xn_scr[...] = _rms(x_ref[...], g_ref[...]).astype(BF16)

    for wi, w_ref in enumerate(w_refs):
        acc = _dot(xn_scr[...], w_ref[...])
        if epilogue == "log_sigmoid":
            acc = -_softplus(-(acc + b_ref[...]))
        elif epilogue == "scale":
            acc = acc * scale
        for o_ref in o_refs[wi * n_out:(wi + 1) * n_out]:
            o_ref[...] = acc.astype(o_ref.dtype)


def _norm_matmul(x, g, ws, out_dtypes, epilogue="none", bias=None, scale=1.0, layer=None):
    m, d = x.shape
    n = ws[0].shape[-1]
    tm, tn = min(WIDE_ROW_TILE, m), min(COL_TILE, n)
    if bias is None:
        bias = jnp.zeros((1, n), F32)
    n_res = len(ws) * len(out_dtypes)
    outs = pl.pallas_call(
        functools.partial(_norm_matmul_kernel, n_w=len(ws), n_out=len(out_dtypes), epilogue=epilogue,
                          scale=scale),
        grid=(m // tm, n // tn),
        in_specs=[pl.BlockSpec((tm, d), lambda i, j: (i, 0)),
                  pl.BlockSpec((1, d), lambda i, j: (0, 0)),
                  pl.BlockSpec((1, tn), lambda i, j: (0, j))]
                 + [_layer_spec(w, layer, (d, tn), lambda i, j: (0, j)) for w in ws],
        out_specs=[pl.BlockSpec((tm, tn), lambda i, j: (i, j)) for _ in range(n_res)],
        out_shape=[jax.ShapeDtypeStruct((m, n), dt) for _ in ws for dt in out_dtypes],
        scratch_shapes=[pltpu.VMEM((tm, d), BF16)],
        compiler_params=_cparams("parallel", "arbitrary"),
        name="norm_matmul_" + epilogue,
    )(x, g.reshape(1, d), bias, *ws)
    return outs


def _proj_ffn_kernel(h_ref, x_ref, wo_ref, g_ref, w1_ref, w3_ref, w2_ref, gy_ref, o_ref, xn_scr, acc_scr, *,
                     final_norm):
    j = pl.program_id(1)

    @pl.when(j == 0)
    def _():
        h1 = h_ref[...] + _dot(x_ref[...], wo_ref[...])
        xn_scr[...] = _rms(h1, g_ref[...]).astype(BF16)
        acc_scr[...] = h1

    x = xn_scr[...]
    a = _dot(x, w1_ref[...])
    b = _dot(x, w3_ref[...])
    hm = (a * jax.nn.sigmoid(a) * b).astype(BF16)
    acc_scr[...] += _dot(hm, w2_ref[...])

    @pl.when(j == pl.num_programs(1) - 1)
    def _():
        o_ref[...] = _rms(acc_scr[...], gy_ref[...]) if final_norm else acc_scr[...]


def _proj_ffn(h, x, w_o, layer_o, g, w1, w3, w2, layer, final_gain=None):
    m, d = h.shape
    k = x.shape[1]
    f = w1.shape[-1]
    tm, tf = _row_tile(m), COL_TILE
    final_norm = final_gain is not None
    return pl.pallas_call(
        functools.partial(_proj_ffn_kernel, final_norm=final_norm),
        grid=(m // tm, f // tf),
        in_specs=[pl.BlockSpec((tm, d), lambda i, j: (i, 0)),
                  pl.BlockSpec((tm, k), lambda i, j: (i, 0)),
                  _layer_spec(w_o, layer_o, (k, d), lambda i, j: (0, 0), pl.Buffered(1)),
                  pl.BlockSpec((1, d), lambda i, j: (0, 0)),
                  _layer_spec(w1, layer, (d, tf), lambda i, j: (0, j)),
                  _layer_spec(w3, layer, (d, tf), lambda i, j: (0, j)),
                  _layer_spec(w2, layer, (tf, d), lambda i, j: (j, 0)),
                  pl.BlockSpec((1, d), lambda i, j: (0, 0))],
        out_specs=pl.BlockSpec((tm, d), lambda i, j: (i, 0)),
        out_shape=jax.ShapeDtypeStruct((m, d), F32),
        scratch_shapes=[pltpu.VMEM((tm, d), BF16), pltpu.VMEM((tm, d), F32)],
        compiler_params=_cparams("parallel", "arbitrary"),
        name="proj_ffn",
    )(h, x, w_o, g.reshape(1, d), w1, w3, w2, (final_gain if final_norm else g).reshape(1, d))


def _mix_kernel(h_ref, hp_ref, st_ref, ln_ref, vf_ref, mu_ref, wr_ref, wk_ref, wv_ref,
                w1_ref, w2_ref, a1_ref, a2_ref, v1_ref, v2_ref, g1_ref, g2_ref,
                w0_ref, a0_ref, v0_ref,
                r_ref, lw_ref, k_ref, v_ref, a_ref, g_ref,
                xr_scr, xk_scr, xv_scr, hw_scr, ha_scr, hv_scr, hg_scr, *, has_v, seq_len):
    @pl.when(pl.program_id(1) == 0)
    def _():
        tm = h_ref.shape[0]
        xn = _rms(h_ref[...], ln_ref[...])
        row = lax.broadcasted_iota(jnp.int32, (tm, 1), 0)
        before_tile = _rms(hp_ref[7:8, :], ln_ref[...])
        xprev = jnp.where(row == 0, before_tile, pltpu.roll(xn, 1, axis=0))
        if seq_len >= tm:
            starts_here = pl.program_id(0) % (seq_len // tm) == 0
            xprev = jnp.where((row == 0) & starts_here, st_ref[0, 0:1, :], xprev)
        else:
            for s in range(tm // seq_len):
                xprev = jnp.where(row == s * seq_len, st_ref[0, s:s + 1, :], xprev)
        xx = xprev - xn
        mix = lambda i: (xn + xx * mu_ref[i:i + 1, :]).astype(BF16)
        xr_scr[...] = mix(0)
        hw_scr[...] = jnp.tanh(_dot(mix(1), w1_ref[...])).astype(BF16)
        xk_scr[...] = mix(2)
        xv = mix(3)
        xv_scr[...] = xv
        if has_v:
            hv_scr[...] = _dot(xv, v1_ref[...]).astype(BF16)
        ha_scr[...] = _dot(mix(4), a1_ref[...]).astype(BF16)
        hg_scr[...] = jax.nn.sigmoid(_dot(mix(5), g1_ref[...])).astype(BF16)

    r_ref[...] = _dot(xr_scr[...], wr_ref[...]).astype(r_ref.dtype)
    k_ref[...] = _dot(xk_scr[...], wk_ref[...]).astype(k_ref.dtype)
    v = _dot(xv_scr[...], wv_ref[...])
    if has_v:
        v = v + (vf_ref[...] - v) * jax.nn.sigmoid(v0_ref[...] + _dot(hv_scr[...], v2_ref[...]))
    v_ref[...] = v.astype(v_ref.dtype)
    w_log = -_softplus(-(w0_ref[...] + _dot(hw_scr[...], w2_ref[...]))) - 0.5
    lw_ref[...] = -jnp.exp(w_log)
    a_ref[...] = jax.nn.sigmoid(a0_ref[...] + _dot(ha_scr[...], a2_ref[...])).astype(a_ref.dtype)
    g_ref[...] = _dot(hg_scr[...], g2_ref[...]).astype(g_ref.dtype)


def _mix(h, shift_state, seq_len, ln, vfirst, mu, wr, wk, wv, w1, w2, a1, a2, v1, v2, g1, g2, w0, a0, v0,
         has_v, layer, layer_v):
    m, d = h.shape
    tm, tn = _row_tile(m), COL_TILE
    assert seq_len % tm == 0 or tm % seq_len == 0
    n_starts = max(1, tm // seq_len)
    seq_of = [[(i * tm + s * seq_len) // seq_len for s in range(n_starts)] for i in range(m // tm)]
    starts = shift_state[jnp.asarray(seq_of)]
    lw_, la_, lv_, lg_ = w1.shape[-1], a1.shape[-1], v1.shape[-1], g1.shape[-1]
    row = lambda i, j: (i, 0)
    col = lambda i, j: (0, j)
    tile = lambda i, j: (i, j)
    fixed = lambda i, j: (0, 0)
    down = lambda w, lyr: _layer_spec(w, lyr, (d, w.shape[-1]), fixed)
    up = lambda w, lyr: _layer_spec(w, lyr, (w.shape[-2], tn), col)
    outs = pl.pallas_call(
        functools.partial(_mix_kernel, has_v=has_v, seq_len=seq_len),
        grid=(m // tm, d // tn),
        in_specs=[pl.BlockSpec((tm, d), row),
                  pl.BlockSpec((8, d), lambda i, j: (jnp.maximum(i * (tm // 8) - 1, 0), 0)),
                  pl.BlockSpec((1, n_starts, d), lambda i, j: (i, 0, 0)),
                  pl.BlockSpec((1, d), fixed),
                  pl.BlockSpec((tm, tn), tile),
                  pl.BlockSpec((8, d), fixed),
                  up(wr, layer), up(wk, layer), up(wv, layer),
                  down(w1, layer), up(w2, layer), down(a1, layer), up(a2, layer),
                  down(v1, layer_v), up(v2, layer_v), down(g1, layer), up(g2, layer),
                  pl.BlockSpec((1, tn), col), pl.BlockSpec((1, tn), col), pl.BlockSpec((1, tn), col)],
        out_specs=[pl.BlockSpec((tm, tn), tile) for _ in range(6)],
        out_shape=[jax.ShapeDtypeStruct((m, d), dt) for dt in (BF16, F32, BF16, BF16, BF16, BF16)],
        scratch_shapes=[pltpu.VMEM((tm, d), BF16), pltpu.VMEM((tm, d), BF16), pltpu.VMEM((tm, d), BF16),
                        pltpu.VMEM((tm, lw_), BF16), pltpu.VMEM((tm, la_), BF16),
                        pltpu.VMEM((tm, lv_), BF16), pltpu.VMEM((tm, lg_), BF16)],
        compiler_params=_cparams("parallel", "arbitrary"),
        name="rwkv_mix_proj",
    )(h, h, starts, ln.reshape(1, d), vfirst, mu, wr, wk, wv, w1, w2, a1, a2, v1, v2, g1, g2, w0, a0, v0)
    return outs


SCAN_CHUNK = 64
SCAN_PAIRS = 16


def _scan_kernel(r_ref, lw_ref, kr_ref, v_ref, a_ref, g_ref,
                 kk_ref, ka_ref, rk_ref, lnw_ref, lnb_ref, s0_ref,
                 y_ref, sout_ref, st_scr):
    c, n_chunks = pl.program_id(2), pl.num_programs(2)
    C = SCAN_CHUNK
    R = 2 * C
    n_pairs = st_scr.shape[0]
    f32 = lambda ref, sl: ref[0, :, sl].astype(F32)

    @pl.when(c == 0)
    def _():
        st_scr[...] = s0_ref[0]

    lane_c = lax.broadcasted_iota(jnp.int32, (C, LANES), 1)
    head0 = lane_c < HEAD_A
    row = lax.broadcasted_iota(jnp.int32, (R, R), 0)
    col = lax.broadcasted_iota(jnp.int32, (R, R), 1)
    same = (row // C) == (col // C)
    strict = same & ((col % C) < (row % C))
    incl = same & ((col % C) <= (row % C))
    row2 = lax.broadcasted_iota(jnp.int32, (2 * LANES, LANES), 0)
    col2 = lax.broadcasted_iota(jnp.int32, (2 * LANES, LANES), 1)
    ones2 = (((row2 % LANES) // HEAD_A) == (col2 // HEAD_A)).astype(BF16)
    head_sum = lambda x: _dot_l_stacked(x, ones2, 2)
    ti = lax.broadcasted_iota(jnp.int32, (C, 3 * C), 0)
    tj = lax.broadcasted_iota(jnp.int32, (C, 3 * C), 1)
    tri3 = ((tj % C) <= ti).astype(BF16)
    zeros_c = jnp.zeros((C, LANES), F32)

    def stack(x):
        return jnp.concatenate([jnp.where(head0, x, 0.0), jnp.where(head0, 0.0, x)], axis=0)

    sls = [slice(p * LANES, (p + 1) * LANES) for p in range(n_pairs)]
    each = lambda f, *cols: [f(*xs) for xs in zip(*cols)]
    bf = lambda xs: [x.astype(BF16) for x in xs]

    r = [f32(r_ref, sl) for sl in sls]
    lw = [lw_ref[0, :, sl] for sl in sls]
    kr = [f32(kr_ref, sl) for sl in sls]
    v = [f32(v_ref, sl) for sl in sls]
    a = [f32(a_ref, sl) for sl in sls]

    kkr = each(lambda x, sl: x * kk_ref[:, sl], kr, sls)
    ss = each(lambda x: head_sum(x * x), kkr)
    kk = each(lambda x, s: x / jnp.maximum(jnp.sqrt(s), 1e-12), kkr, ss)
    k = each(lambda x, y, sl: x * (1.0 + (y - 1.0) * ka_ref[:, sl]), kr, a, sls)
    b = each(lambda x, y: x * y, kk, a)

    cum = each(lambda x: _dot(tri3, jnp.concatenate(_split(x, 3), axis=0)), lw)
    g_in = each(jnp.exp, cum)
    g_ex = each(lambda x, y: jnp.exp(x - y), cum, lw)
    g_inv = each(lambda x: jnp.exp(-x), cum)
    g_tail = each(lambda x: jnp.exp(x[C - 1:C, :] - x), cum)

    at = each(lambda x, y: stack(-x * y), kk, g_ex)
    at_b = bf(at)
    rt = bf(each(lambda x, y: stack(x * y), r, g_in))
    bt = bf(each(lambda x, y: stack(x * y), b, g_inv))
    kt = bf(each(lambda x, y: stack(x * y), k, g_inv))
    bh = each(lambda x, y: stack(x * y), b, g_tail)
    kh = each(lambda x, y: stack(x * y), k, g_tail)
    vs = bf(each(stack, v))

    sc = each(lambda w, x, y, z: _dot_nt(jnp.concatenate([w, x], axis=0), jnp.concatenate([y, z], axis=0)),
              at_b, rt, bt, kt)
    a_ab = each(lambda x: jnp.where(strict, x[:R, :R], 0.0), sc)
    a_ak = bf(each(lambda x: jnp.where(strict, x[:R, R:], 0.0), sc))
    a_rb = bf(each(lambda x: jnp.where(incl, x[R:, :R], 0.0), sc))
    a_rk = bf(each(lambda x: jnp.where(incl, x[R:, R:], 0.0), sc))

    st = [st_scr[p] for p in range(n_pairs)]
    st_b = bf(st)
    u = each(lambda w, y, s, z: _dot(jnp.concatenate([w, y], axis=1), jnp.concatenate([s, z], axis=0)),
             at_b, a_ak, st_b, vs)
    pw = a_ab
    n_sq = int(math.log2(C))
    for it in range(n_sq):
        pw_b = bf(pw)
        u = each(lambda y, z: y + _dot(z, y.astype(BF16)), u, pw_b)
        if it + 1 < n_sq:
            pw = each(lambda z: _dot(z, z), pw_b)
    u_b = bf(u)
    uv = each(lambda y, t: jnp.concatenate([y, t], axis=0), u_b, vs)
    o_st = each(lambda q, w, z, s, y: _dot(jnp.concatenate([q, w, z], axis=1), jnp.concatenate([s, y], axis=0)),
                rt, a_rb, a_rk, st_b, uv)
    o = each(lambda y: y[:C] + y[C:], o_st)

    def decay_col(x):
        x_t = jnp.concatenate([x, zeros_c], axis=0).T if C < LANES else x.T
        return jnp.exp(jnp.sum(x_t, axis=1, keepdims=True))

    dcol = each(decay_col, lw)
    st_new = each(lambda dc, s, w, z, y: dc * s + _dot(jnp.concatenate([w.T, z.T], axis=1).astype(BF16), y),
                  dcol, st, bh, kh, uv)
    for p in range(n_pairs):
        st_scr[p] = st_new[p]

    mean = each(lambda y: head_sum(y) * (1.0 / HEAD_A), o)
    dlt = each(lambda y, z: y - z, o, mean)
    var = each(lambda y: head_sum(y * y) * (1.0 / HEAD_A), dlt)
    on = each(lambda y, z, sl: y * lax.rsqrt(z + LNX_EPS) * lnw_ref[:, sl] + lnb_ref[:, sl], dlt, var, sls)
    bonus = each(lambda x, y, z, sl: head_sum(x * y * rk_ref[:, sl]) * z, r, k, v, sls)
    for p, sl in enumerate(sls):
        y_ref[0, :, sl] = ((on[p] + bonus[p]) * f32(g_ref, sl)).astype(y_ref.dtype)

    @pl.when(c == n_chunks - 1)
    def _():
        sout_ref[0] = st_scr[...]


def _scan(r, lw, kr, v, a, g, k_k, k_a, r_k, lnx_w, lnx_b, s0_bd):
    bsz, t, d = r.shape
    C = SCAN_CHUNK
    P = min(SCAN_PAIRS, d // LANES)
    w = P * LANES
    n_pg = d // w
    seq = pl.BlockSpec((1, C, w), lambda b, q, c: (b, c, q))
    vec = pl.BlockSpec((1, w), lambda b, q, c: (0, q))
    sbd = pl.BlockSpec((1, P, LANES, LANES), lambda b, q, c: (b, q, 0, 0))
    y, s_out = pl.pallas_call(
        _scan_kernel,
        grid=(bsz, n_pg, t // C),
        in_specs=[seq] * 6 + [vec] * 5 + [sbd],
        out_specs=[seq, sbd],
        out_shape=[jax.ShapeDtypeStruct((bsz, t, d), BF16),
                   jax.ShapeDtypeStruct(s0_bd.shape, F32)],
        scratch_shapes=[pltpu.VMEM((P, LANES, LANES), F32)],
        compiler_params=_cparams("parallel", "parallel", "arbitrary"),
        name="wkv7_scan",
    )(r, lw, kr, v, a, g, k_k.reshape(1, d), k_a.reshape(1, d), r_k.reshape(1, d),
      lnx_w.reshape(1, d), lnx_b.reshape(1, d), s0_bd)
    return y, s_out


def _state_to_bd(s):
    bsz, h = s.shape[:2]
    st = jnp.swapaxes(s, 2, 3).reshape(bsz, h // 2, 2, HEAD_A, HEAD_A)
    eye = jnp.eye(2, dtype=s.dtype)
    bd = st[:, :, :, :, None, :] * eye[None, None, :, None, :, None]
    return bd.reshape(bsz, h // 2, 2 * HEAD_A, 2 * HEAD_A)


def _bd_to_state(bd):
    bsz, hp = bd.shape[:2]
    x = bd.reshape(bsz, hp, 2, HEAD_A, 2, HEAD_A)
    st = jnp.stack([x[:, :, 0, :, 0, :], x[:, :, 1, :, 1, :]], axis=2)
    return jnp.swapaxes(st.reshape(bsz, hp * 2, HEAD_A, HEAD_A), 2, 3)


N_PIECES = 3


def _forget_bias_kernel(x_ref, eq_ref, ek_ref, qa_ref, ka_ref, carry_scr):
    @pl.when(pl.program_id(1) == 0)
    def _():
        carry_scr[...] = jnp.zeros_like(carry_scr)

    tt = x_ref.shape[1]
    ti = lax.broadcasted_iota(jnp.int32, (tt, N_PIECES * tt), 0)
    tj = lax.broadcasted_iota(jnp.int32, (tt, N_PIECES * tt), 1)
    tri = ((tj % tt) <= ti).astype(BF16)
    c = _dot(tri, jnp.concatenate(_split(x_ref[0], N_PIECES), axis=0)) + carry_scr[...]
    carry_scr[...] = c[tt - 1:tt, :]
    pieces = jnp.concatenate(_split(c * LOG2E, N_PIECES), axis=1)
    lane = lax.broadcasted_iota(jnp.int32, qa_ref.shape[1:], 1) % HEAD_B
    qa = _dot(pieces, eq_ref[...]) + jnp.where((lane >= N_PIECES) & (lane < 2 * N_PIECES), 1.0, 0.0)
    ka = _dot(pieces, ek_ref[...]) + jnp.where(lane < N_PIECES, 1.0, 0.0)
    qa_ref[0] = qa.astype(BF16)
    ka_ref[0] = ka.astype(BF16)


def _forget_bias_lanes(logf_lanes, n_heads):
    bsz, t, _ = logf_lanes.shape
    tt = _row_tile(t)
    d = n_heads * HEAD_B
    src = jnp.arange(N_PIECES * LANES)
    piece, head = src // LANES, src % LANES
    place = lambda lane0: (jnp.arange(d)[None, :] == (head * HEAD_B + lane0 + piece)[:, None]) & (
        head < n_heads)[:, None]
    eq = place(0).astype(BF16)
    ek = -place(N_PIECES).astype(BF16)
    out = pl.BlockSpec((1, tt, d), lambda b, i: (b, i, 0))
    return pl.pallas_call(
        _forget_bias_kernel,
        grid=(bsz, t // tt),
        in_specs=[pl.BlockSpec((1, tt, LANES), lambda b, i: (b, i, 0)),
                  pl.BlockSpec(eq.shape, lambda b, i: (0, 0)), pl.BlockSpec(ek.shape, lambda b, i: (0, 0))],
        out_specs=[out, out],
        out_shape=[jax.ShapeDtypeStruct((bsz, t, d), BF16)] * 2,
        scratch_shapes=[pltpu.VMEM((1, LANES), F32)],
        compiler_params=_cparams("parallel", "arbitrary"),
        name="forget_bias_lanes",
    )(logf_lanes, eq, ek)


def _flash_kernel(q_ref, qa_ref, k_ref, ka_ref, v_ref, o_ref, m_scr, acc_scr):
    i, j = pl.program_id(1), pl.program_id(2)
    tq, tk = q_ref.shape[1], k_ref.shape[1]
    n_heads = q_ref.shape[2] // HEAD_B

    @pl.when(j == 0)
    def _():
        m_scr[...] = jnp.full_like(m_scr, NEG_BIG)
        acc_scr[...] = jnp.zeros_like(acc_scr)

    def block(rows, n_keys, masked):
        n_rows = rows.stop - rows.start
        keys = slice(0, n_keys)
        if masked:
            qpos = lax.broadcasted_iota(jnp.int32, (n_rows, n_keys), 0) + rows.start
            kpos = lax.broadcasted_iota(jnp.int32, (n_rows, n_keys), 1)
            keep = kpos <= qpos
        ones = jnp.ones((n_keys, HEAD_B), BF16)
        for h in range(n_heads):
            hs = slice(h * HEAD_B, (h + 1) * HEAD_B)
            q_aug = jnp.concatenate([q_ref[0, rows, hs], qa_ref[0, rows, hs]], axis=1)
            k_aug = jnp.concatenate([k_ref[0, keys, hs], ka_ref[0, keys, hs]], axis=1)
            s = _dot_nt(q_aug, k_aug)
            if masked:
                s = jnp.where(keep, s, NEG_BIG)
            m_prev = m_scr[h, rows]
            m_next = jnp.maximum(m_prev, jnp.max(s, axis=1, keepdims=True))
            alpha = jnp.exp2(m_prev - m_next)
            p = jnp.exp2(s - jnp.concatenate([m_next] * (n_keys // LANES), axis=1))
            v_aug = jnp.concatenate([v_ref[0, keys, hs], ones], axis=1)
            acc_scr[h, rows] = (jnp.concatenate([alpha, alpha], axis=1) * acc_scr[h, rows]
                                + _dot(p.astype(BF16), v_aug))
            m_scr[h, rows] = m_next

    @pl.when(j < i)
    def _():
        block(slice(0, tq), tk, False)

    @pl.when(j == i)
    def _():
        half = tq // 2
        if half % LANES == 0:
            block(slice(0, half), half, True)
            block(slice(half, tq), tk, True)
        else:
            block(slice(0, tq), tk, True)
        for h in range(n_heads):
            hs = slice(h * HEAD_B, (h + 1) * HEAD_B)
            acc = acc_scr[h]
            o_ref[0, :, hs] = (acc[:, :HEAD_B] / acc[:, HEAD_B:]).astype(o_ref.dtype)


def _flash_attention(q, q_aug, k, k_aug, v):
    bsz, t, d = q.shape
    h = d // HEAD_B
    tq = _row_tile(t)
    nq = t // tq
    q_idx = lambda b, i, j: (b, i, 0)
    kv_idx = lambda b, i, j: (b, jnp.minimum(j, i), 0)
    return pl.pallas_call(
        _flash_kernel,
        grid=(bsz, nq, nq),
        in_specs=[pl.BlockSpec((1, tq, d), q_idx), pl.BlockSpec((1, tq, d), q_idx),
                  pl.BlockSpec((1, tq, d), kv_idx), pl.BlockSpec((1, tq, d), kv_idx),
                  pl.BlockSpec((1, tq, d), kv_idx)],
        out_specs=pl.BlockSpec((1, tq, d), q_idx),
        out_shape=jax.ShapeDtypeStruct((bsz, t, d), BF16),
        scratch_shapes=[pltpu.VMEM((h, tq, LANES), F32), pltpu.VMEM((h, tq, 2 * HEAD_B), F32)],
        compiler_params=_cparams("parallel", "parallel", "arbitrary"),
        name="fox_prompt_attention",
    )(q, q_aug, k, k_aug, v)


PAGES_PER_STEP = 8
SLAB = 8


def _page_suffix_kernel(lf_ref, suf_ref, tot_ref):
    pb, h, page = lf_ref.shape
    u = lax.broadcasted_iota(jnp.int32, (page, page), 0)
    s = lax.broadcasted_iota(jnp.int32, (page, page), 1)
    lf = lf_ref[...].reshape(pb * h, page) * LOG2E
    suf_ref[...] = _dot_l(lf, (u > s).astype(BF16), 3).reshape(pb, h, page)
    tot_ref[...] = _dot_l(lf, jnp.ones((page, page), BF16), 3).reshape(pb, h, page)


def _page_suffix(lf_t):
    n_phys, h, page = lf_t.shape
    pb = math.gcd(n_phys, 64)
    spec = pl.BlockSpec((pb, h, page), lambda i: (i, 0, 0))
    return pl.pallas_call(
        _page_suffix_kernel,
        grid=(n_phys // pb,),
        in_specs=[spec],
        out_specs=[spec, spec],
        out_shape=[jax.ShapeDtypeStruct(lf_t.shape, F32)] * 2,
        compiler_params=_cparams("parallel"),
        name="page_logf_suffix",
    )(lf_t)


def _paged_kernel(pt_ref, q_ref, kn_ref, vn_ref, gn_ref, *refs, n_group, n_slab):
    del pt_ref
    p, n_steps = pl.program_id(1), pl.num_programs(1)
    n_kv = n_group * n_slab
    kp_refs, vp_refs = refs[:n_kv], refs[n_kv:2 * n_kv]
    suf_refs, tot_refs = refs[2 * n_kv:3 * n_kv], refs[3 * n_kv:4 * n_kv]
    o_ref, m_scr, l_scr, acc_scr, cq_scr, carry_scr = refs[4 * n_kv:]
    rows = q_ref.shape[1]
    n_heads = kn_ref.shape[1]
    nq = rows // n_heads
    page = kn_ref.shape[2]
    srows = SLAB * nq
    flat = page * SLAB

    def update(sl, s, pv_of):
        m_old = m_scr[sl]
        m_new = jnp.maximum(m_old, jnp.max(s, axis=1, keepdims=True))
        alpha = jnp.exp2(m_old - m_new)
        pr = jnp.exp2(s - m_new)
        l_scr[sl] = alpha * l_scr[sl] + jnp.sum(pr, axis=1, keepdims=True)
        acc_scr[sl] = alpha * acc_scr[sl] + pv_of(pr)
        m_scr[sl] = m_new

    @pl.when(p == 0)
    def _():
        m_scr[...] = jnp.full_like(m_scr, NEG_BIG)
        l_scr[...] = jnp.zeros_like(l_scr)
        acc_scr[...] = jnp.zeros_like(acc_scr)
        carry_scr[...] = jnp.zeros_like(carry_scr)
        row = lax.broadcasted_iota(jnp.int32, (rows, page), 0)
        lane = lax.broadcasted_iota(jnp.int32, (rows, page), 1)
        u = lax.broadcasted_iota(jnp.int32, (page, page), 0)
        s_ = lax.broadcasted_iota(jnp.int32, (page, page), 1)
        gn = gn_ref[0] * LOG2E
        qi = row % nq
        c_new_col = jnp.sum(jnp.where(lane <= qi, gn, 0.0), axis=1, keepdims=True)
        hi, mid, lo = [x.astype(F32) for x in _split(c_new_col, 3)]
        lane_a = lax.broadcasted_iota(jnp.int32, (rows, LANES), 1)
        cq_scr[...] = jnp.where(lane_a == 0, hi, jnp.where(lane_a == 1, mid,
                                                           jnp.where(lane_a == 2, lo, 0.0))).astype(BF16)
        c_new_row = _dot_l(gn, (u <= s_).astype(BF16), 3)
        s = jnp.concatenate([_dot_nt(q_ref[0, h * nq:(h + 1) * nq, :], kn_ref[0, h])
                             for h in range(n_heads)], axis=0) + (c_new_col - c_new_row)
        s = jnp.where(lane <= qi, s, NEG_BIG)
        for c in range(n_slab):
            sl = slice(c * srows, (c + 1) * srows)
            update(sl, s[sl], lambda pr, c=c: jnp.concatenate(
                [_dot(pr[hh * nq:(hh + 1) * nq, :].astype(BF16), vn_ref[0, c * SLAB + hh])
                 for hh in range(SLAB)], axis=0))

    row_f = lax.broadcasted_iota(jnp.int32, (srows, flat), 0)
    lane_f = lax.broadcasted_iota(jnp.int32, (srows, flat), 1)
    same_head = (row_f // nq) == (lane_f % SLAB)
    lane_k = lax.broadcasted_iota(jnp.int32, (flat, LANES), 1)
    ones_k = (lane_k < 3).astype(BF16)
    slabs = range(n_slab)
    sls = [slice(c * srows, (c + 1) * srows) for c in slabs]
    q_aug = [jnp.concatenate([q_ref[0, sl, :], cq_scr[sl, :]], axis=1) for sl in sls]
    carry = [carry_scr[c:c + 1, :] for c in slabs]
    s = [[] for _ in slabs]
    for g in range(n_group):
        for c in slabs:
            k_rows = kp_refs[g * n_slab + c][...].reshape(flat, HEAD_B).astype(BF16)
            z = _dot_nt(q_aug[c], jnp.concatenate([k_rows, ones_k], axis=1))
            s[c].append(jnp.where(same_head, z + (suf_refs[g * n_slab + c][...] + carry[c]), NEG_BIG))
            carry[c] = carry[c] + tot_refs[g * n_slab + c][...]
    s = [jnp.concatenate(x, axis=1) for x in s]
    m_old = [m_scr[sl] for sl in sls]
    m_new = [jnp.maximum(mo, jnp.max(x, axis=1, keepdims=True)) for mo, x in zip(m_old, s)]
    alpha = [jnp.exp2(mo - mn) for mo, mn in zip(m_old, m_new)]
    pr = [jnp.exp2(x - mn) for x, mn in zip(s, m_new)]
    v_rows = [jnp.concatenate([vp_refs[g * n_slab + c][...].reshape(flat, HEAD_B)
                               for g in range(n_group)], axis=0).astype(BF16) for c in slabs]
    pv = [_dot(x.astype(BF16), vr) for x, vr in zip(pr, v_rows)]
    for c, sl in enumerate(sls):
        carry_scr[c:c + 1, :] = carry[c]
        l_scr[sl] = alpha[c] * l_scr[sl] + jnp.sum(pr[c], axis=1, keepdims=True)
        acc_scr[sl] = alpha[c] * acc_scr[sl] + pv[c]
        m_scr[sl] = m_new[c]

    @pl.when(p == n_steps - 1)
    def _():
        o_ref[0] = acc_scr[...] / l_scr[...]


def _paged_attention(page_table, q, k_new, v_new, g_new, cache_k, cache_v, suffix, total):
    dbs, rows, _ = q.shape
    n_heads = k_new.shape[1]
    n_slab = n_heads // SLAB
    n_pages = page_table.shape[1]
    page = k_new.shape[2]
    flat = page * SLAB
    n_group = math.gcd(PAGES_PER_STEP, n_pages)
    per_b = lambda b, p, pt: (b, 0, 0)
    per_b4 = lambda b, p, pt: (b, 0, 0, 0)
    slot_page = lambda b, p, pt, g: pt[b, n_pages - 1 - (p * n_group + g)]
    slots = [(g, c) for g in range(n_group) for c in range(n_slab)]
    kv_specs = [pl.BlockSpec((None, page, None, SLAB, HEAD_B),
                             lambda b, p, pt, g=g, c=c: (slot_page(b, p, pt, g), 0, c, 0, 0)) for g, c in slots]
    row_specs = [pl.BlockSpec((None, None, 1, flat),
                              lambda b, p, pt, g=g, c=c: (slot_page(b, p, pt, g), c, 0, 0)) for g, c in slots]
    n_kv = len(slots)
    grid_spec = pltpu.PrefetchScalarGridSpec(
        num_scalar_prefetch=1,
        grid=(dbs, n_pages // n_group),
        in_specs=[pl.BlockSpec((1, rows, HEAD_B), per_b),
                  pl.BlockSpec((1, n_heads, page, HEAD_B), per_b4),
                  pl.BlockSpec((1, n_heads, page, HEAD_B), per_b4),
                  pl.BlockSpec((1, rows, page), per_b)] + kv_specs + kv_specs + row_specs + row_specs,
        out_specs=pl.BlockSpec((1, rows, HEAD_B), per_b),
        scratch_shapes=[pltpu.VMEM((rows, 1), F32), pltpu.VMEM((rows, 1), F32),
                        pltpu.VMEM((rows, HEAD_B), F32), pltpu.VMEM((rows, LANES), BF16),
                        pltpu.VMEM((8, flat), F32)])
    return pl.pallas_call(
        functools.partial(_paged_kernel, n_group=n_group, n_slab=n_slab),
        grid_spec=grid_spec,
        out_shape=jax.ShapeDtypeStruct((dbs, rows, HEAD_B), F32),
        compiler_params=_cparams("parallel", "arbitrary"),
        name="fox_paged_attention",
    )(page_table, q, k_new, v_new, g_new, *([cache_k] * n_kv), *([cache_v] * n_kv),
      *([suffix] * n_kv), *([total] * n_kv))


def _trunk(x, shift0, wkv0, past, p):
    bsz, t, d = x.shape
    m = bsz * t
    n_a = p["tm_w_r"].shape[0]
    depth = p["ln1"].shape[0]
    h_b = d // HEAD_B
    h = x.reshape(m, d)
    v_first = None
    shifts, states = [], []
    t_scan = -(-t // SCAN_CHUNK) * SCAN_CHUNK
    heads_first = lambda z: jnp.swapaxes(z.reshape(bsz, t, h_b, HEAD_B), 1, 2)
    for l in range(depth):
        if l == n_a:
            k_new, k_bf, v_new, v_bf = _norm_matmul(h, p["kv_norm"], [p["fox_w_k"], p["fox_w_v"]], (F32, BF16))
            (lf_pad,) = _norm_matmul(h, p["kv_norm"], [p["fox_w_f"]], (F32,), epilogue="log_sigmoid",
                                     bias=p["fox_b_f"])
            logf_new = lf_pad[:, :h_b].reshape(bsz, t, h_b)
            if past is None:
                q_aug, k_aug = _forget_bias_lanes(lf_pad.reshape(bsz, t, LANES), h_b)
            else:
                page = past[0].shape[1]
                pad_keys = lambda z: jnp.pad(heads_first(z), ((0, 0), (0, 0), (0, page - t), (0, 0)))
                k_pad, v_pad = pad_keys(k_bf), pad_keys(v_bf)
                g_new = jnp.repeat(jnp.swapaxes(logf_new, 1, 2), t, axis=1)
                g_new = jnp.pad(g_new, ((0, 0), (0, 0), (0, page - t)))
        if l < n_a:
            shifts.append(_rmsnorm(h.reshape(bsz, t, d)[:, -1], p["ln1"][l]))
            has_v = l > 0
            lv = max(l - 1, 0)
            r, lw, kr, v, a, g = _mix(
                h, shift0[l], t, p["ln1"][l], v_first if has_v else h, p["mu8"][l],
                p["tm_w_r"], p["tm_w_k"], p["tm_w_v"],
                p["tm_w1"], p["tm_w2"], p["tm_a1"], p["tm_a2"],
                p["tm_v1"], p["tm_v2"], p["tm_g1"], p["tm_g2"],
                p["tm_w0"][l].reshape(1, d), p["tm_a0"][l].reshape(1, d), p["tm_v0"][lv].reshape(1, d),
                has_v, l, lv)
            if l == 0:
                v_first = v
            seq = lambda z: jnp.pad(z.reshape(bsz, t, d), ((0, 0), (0, t_scan - t), (0, 0)))
            y, s_bd = _scan(seq(r), seq(lw), seq(kr), seq(v), seq(a), seq(g),
                            p["tm_k_k"][l], p["tm_k_a"][l], p["tm_r_k"][l].reshape(d),
                            p["tm_lnx_w"][l], p["tm_lnx_b"][l], _state_to_bd(wkv0[l]))
            states.append(_bd_to_state(s_bd))
            mixed, w_o, layer_o = y[:, :t].reshape(m, d), p["tm_w_o"], l
        else:
            j = l - n_a
            (q,) = _norm_matmul(h, p["ln1"][l], [p["fox_w_q"]], (BF16,), epilogue="scale",
                                scale=HEAD_B ** -0.5 * LOG2E, layer=j)
            if past is None:
                att = _flash_attention(q.reshape(bsz, t, d), q_aug, k_bf.reshape(bsz, t, d), k_aug,
                                       v_bf.reshape(bsz, t, d)).reshape(m, d)
            else:
                cache_k, cache_v, suffix, total, page_table = past
                o = _paged_attention(page_table, heads_first(q).reshape(bsz, h_b * t, HEAD_B),
                                     k_pad, v_pad, g_new, cache_k, cache_v, suffix, total)
                att = jnp.swapaxes(o.reshape(bsz, h_b, t, HEAD_B), 1, 2).reshape(m, d).astype(BF16)
            mixed, w_o, layer_o = att, p["fox_w_o"], j
        h = _proj_ffn(h, mixed, w_o, layer_o, p["ln2"][l], p["ffn_w1"], p["ffn_w3"], p["ffn_w2"], l,
                      final_gain=p["ln_out"] if l == depth - 1 else None)
    h_sh = (bsz, t, h_b, HEAD_B)
    return (h.reshape(bsz, t, d), jnp.stack(shifts), jnp.stack(states),
            k_new.reshape(h_sh), v_new.reshape(h_sh), logf_new)


def kernel(x_prompt, x_sample, state_wkv, state_shift, cache_k, cache_v, cache_logf, page_table, ln1, ln2, ln_out, tm_mu, tm_w_r, tm_w_k, tm_w_v, tm_w_o, tm_w0, tm_w1, tm_w2, tm_a0, tm_a1, tm_a2, tm_v0, tm_v1, tm_v2, tm_g1, tm_g2, tm_k_k, tm_k_a, tm_r_k, tm_lnx_w, tm_lnx_b, kv_norm, fox_w_k, fox_w_v, fox_w_f, fox_b_f, fox_w_q, fox_w_o, ffn_w1, ffn_w3, ffn_w2):
    bf = lambda w: w.astype(BF16)
    n_a, d = tm_w0.shape
    h_b = d // HEAD_B
    lora = lambda w1, w2: (bf(jnp.pad(w1, ((0, 0), (0, 0), (0, -w1.shape[2] % LANES)))),
                           bf(jnp.pad(w2, ((0, 0), (0, -w2.shape[1] % LANES), (0, 0)))))
    w1, w2 = lora(tm_w1, tm_w2)
    a1, a2 = lora(tm_a1, tm_a2)
    if tm_v1.shape[0] == 0:
        tm_v0 = jnp.zeros((1, d), F32)
        tm_v1 = jnp.zeros((1, d, LANES), F32)
        tm_v2 = jnp.zeros((1, LANES, d), F32)
    v1, v2 = lora(tm_v1, tm_v2)
    g1, g2 = lora(tm_g1, tm_g2)
    p = dict(ln1=ln1, ln2=ln2, ln_out=ln_out,
             mu8=jnp.pad(tm_mu, ((0, 0), (0, 2), (0, 0))),
             tm_w_r=bf(tm_w_r), tm_w_k=bf(tm_w_k), tm_w_v=bf(tm_w_v), tm_w_o=bf(tm_w_o),
             tm_w0=tm_w0, tm_w1=w1, tm_w2=w2, tm_a0=tm_a0, tm_a1=a1, tm_a2=a2,
             tm_v0=tm_v0, tm_v1=v1, tm_v2=v2, tm_g1=g1, tm_g2=g2,
             tm_k_k=tm_k_k, tm_k_a=tm_k_a, tm_r_k=tm_r_k, tm_lnx_w=tm_lnx_w, tm_lnx_b=tm_lnx_b,
             kv_norm=kv_norm, fox_w_k=bf(fox_w_k), fox_w_v=bf(fox_w_v),
             fox_w_f=bf(jnp.pad(fox_w_f, ((0, 0), (0, LANES - h_b)))),
             fox_b_f=jnp.pad(fox_b_f, (0, LANES - h_b)).reshape(1, LANES),
             fox_w_q=bf(fox_w_q), fox_w_o=bf(fox_w_o),
             ffn_w1=bf(ffn_w1), ffn_w3=bf(ffn_w3), ffn_w2=bf(ffn_w2))

    bsz = x_prompt.shape[0]
    h_a = d // HEAD_A
    shift0 = jnp.zeros((n_a, bsz, d), x_prompt.dtype)
    wkv0 = jnp.zeros((n_a, bsz, h_a, HEAD_A, HEAD_A), F32)
    n_phys, page = cache_k.shape[:2]
    n_slab = h_b // SLAB
    slabs = lambda c: c.reshape(n_phys, page, n_slab, SLAB, HEAD_B)
    key_head_rows = lambda z: jnp.transpose(z.reshape(n_phys, n_slab, SLAB, page), (0, 1, 3, 2)).reshape(
        n_phys, n_slab, 1, page * SLAB)
    suffix, total = _page_suffix(jnp.swapaxes(cache_logf, 1, 2))
    past = (slabs(cache_k), slabs(cache_v), key_head_rows(suffix), key_head_rows(total), page_table)
    y_p, sh_p, wkv_p, k_p, v_p, lf_p = _trunk(x_prompt, shift0, wkv0, None, p)
    y_s, sh_s, wkv_s, k_s, v_s, lf_s = _trunk(x_sample, state_shift, state_wkv, past, p)
    return (y_p, y_s, wkv_p, sh_p, k_p, v_p, lf_p, wkv_s, sh_s, k_s, v_s, lf_s)
```

```python
import functools
import math

import jax
import jax.numpy as jnp
from jax import lax
from jax.experimental import pallas as pl
from jax.experimental.pallas import tpu as pltpu

F32 = jnp.float32
BF16 = jnp.bfloat16

HEAD_A = 64
HEAD_B = 128
LANES = 128
RMS_EPS = 1e-6
LNX_EPS = 64e-5
NEG_BIG = -1e30
LOG2E = math.log2(math.e)
VMEM_LIMIT = 56 * 1024 * 1024

ROW_TILE = 512
WIDE_ROW_TILE = 1024
COL_TILE = 512


def _cparams(*sem):
    return pltpu.CompilerParams(dimension_semantics=sem, vmem_limit_bytes=VMEM_LIMIT)


def _dot(a, b):
    return jnp.dot(a, b, preferred_element_type=F32)


def _dot_nt(a, b):
    return lax.dot_general(a, b, (((1,), (1,)), ((), ())), preferred_element_type=F32)


def _split(x, n):
    pieces = []
    for _ in range(n - 1):
        hi = x.astype(BF16)
        pieces.append(hi)
        x = x - hi.astype(F32)
    pieces.append(x.astype(BF16))
    return pieces


def _dot_l(x, w, n):
    return sum(_dot(piece, w) for piece in _split(x, n))


def _dot_l_stacked(x, w_stacked, n):
    return _dot(jnp.concatenate(_split(x, n), axis=1), w_stacked)


def _rms(x, g):
    return x * lax.rsqrt(jnp.mean(x * x, axis=-1, keepdims=True) + RMS_EPS) * g


def _softplus(z):
    return jnp.maximum(z, 0.0) + jnp.log1p(jnp.exp(-jnp.abs(z)))


def _row_tile(m):
    return min(ROW_TILE, m)


def _layer_spec(w, layer, block, index, pipeline_mode=None):
    if w.ndim == 2:
        return pl.BlockSpec(block, index, pipeline_mode=pipeline_mode)
    return pl.BlockSpec((None,) + block, lambda *args: (layer,) + index(*args), pipeline_mode=pipeline_mode)


def _rmsnorm_kernel(x_ref, g_ref, o_ref):
    o_ref[...] = _rms(x_ref[...], g_ref[...])


def _rmsnorm(x, g):
    m, d = x.shape
    tm = _row_tile(m)
    return pl.pallas_call(
        _rmsnorm_kernel,
        grid=(m // tm,),
        in_specs=[pl.BlockSpec((tm, d), lambda i: (i, 0)), pl.BlockSpec((1, d), lambda i: (0, 0))],
        out_specs=pl.BlockSpec((tm, d), lambda i: (i, 0)),
        out_shape=jax.ShapeDtypeStruct((m, d), F32),
        compiler_params=_cparams("parallel"),
        name="rmsnorm",
    )(x, g.reshape(1, d))


def _norm_matmul_kernel(x_ref, g_ref, b_ref, *refs, n_w, n_out, epilogue, scale):
    w_refs, o_refs, xn_scr = refs[:n_w], refs[n_w:-1], refs[-1]

    @pl.when(pl.program_id(1) == 0)
    def _():
        xn_scr[...] = _rms(x_ref[...], g_ref[...]).astype(BF16)

    for wi, w_ref in enumerate(w_refs):
        acc = _dot(xn_scr[...], w_ref[...])
        if epilogue == "log_sigmoid":
            acc = -_softplus(-(acc + b_ref[...]))
        elif epilogue == "scale":
            acc = acc * scale
        for o_ref in o_refs[wi * n_out:(wi + 1) * n_out]:
            o_ref[...] = acc.astype(o_ref.dtype)


def _norm_matmul(x, g, ws, out_dtypes, epilogue="none", bias=None, scale=1.0, layer=None):
    m, d = x.shape
    n = ws[0].shape[-1]
    tm, tn = min(WIDE_ROW_TILE, m), min(COL_TILE, n)
    if bias is None:
        bias = jnp.zeros((1, n), F32)
    n_res = len(ws) * len(out_dtypes)
    outs = pl.pallas_call(
        functools.partial(_norm_matmul_kernel, n_w=len(ws), n_out=len(out_dtypes), epilogue=epilogue,
                          scale=scale),
        grid=(m // tm, n // tn),
        in_specs=[pl.BlockSpec((tm, d), lambda i, j: (i, 0)),
                  pl.BlockSpec((1, d), lambda i, j: (0, 0)),
                  pl.BlockSpec((1, tn), lambda i, j: (0, j))]
                 + [_layer_spec(w, layer, (d, tn), lambda i, j: (0, j)) for w in ws],
        out_specs=[pl.BlockSpec((tm, tn), lambda i, j: (i, j)) for _ in range(n_res)],
        out_shape=[jax.ShapeDtypeStruct((m, n), dt) for _ in ws for dt in out_dtypes],
        scratch_shapes=[pltpu.VMEM((tm, d), BF16)],
        compiler_params=_cparams("parallel", "arbitrary"),
        name="norm_matmul_" + epilogue,
    )(x, g.reshape(1, d), bias, *ws)
    return outs


def _proj_ffn_kernel(h_ref, x_ref, wo_ref, g_ref, w1_ref, w3_ref, w2_ref, gy_ref, o_ref, xn_scr, acc_scr, *,
                     final_norm):
    j = pl.program_id(1)

    @pl.when(j == 0)
    def _():
        h1 = h_ref[...] + _dot(x_ref[...], wo_ref[...])
        xn_scr[...] = _rms(h1, g_ref[...]).astype(BF16)
        acc_scr[...] = h1

    x = xn_scr[...]
    a = _dot(x, w1_ref[...])
    b = _dot(x, w3_ref[...])
    hm = (a * jax.nn.sigmoid(a) * b).astype(BF16)
    acc_scr[...] += _dot(hm, w2_ref[...].astype(BF16))

    @pl.when(j == pl.num_programs(1) - 1)
    def _():
        o_ref[...] = _rms(acc_scr[...], gy_ref[...]) if final_norm else acc_scr[...]


def _proj_ffn(h, x, w_o, layer_o, g, w1, w3, w2, layer, final_gain=None):
    m, d = h.shape
    k = x.shape[1]
    f = w1.shape[-1]
    tm, tf = _row_tile(m), COL_TILE
    final_norm = final_gain is not None
    return pl.pallas_call(
        functools.partial(_proj_ffn_kernel, final_norm=final_norm),
        grid=(m // tm, f // tf),
        in_specs=[pl.BlockSpec((tm, d), lambda i, j: (i, 0)),
                  pl.BlockSpec((tm, k), lambda i, j: (i, 0)),
                  _layer_spec(w_o, layer_o, (k, d), lambda i, j: (0, 0), pl.Buffered(1)),
                  pl.BlockSpec((1, d), lambda i, j: (0, 0)),
                  _layer_spec(w1, layer, (d, tf), lambda i, j: (0, j)),
                  _layer_spec(w3, layer, (d, tf), lambda i, j: (0, j)),
                  _layer_spec(w2, layer, (tf, d), lambda i, j: (j, 0)),
                  pl.BlockSpec((1, d), lambda i, j: (0, 0))],
        out_specs=pl.BlockSpec((tm, d), lambda i, j: (i, 0)),
        out_shape=jax.ShapeDtypeStruct((m, d), F32),
        scratch_shapes=[pltpu.VMEM((tm, d), BF16), pltpu.VMEM((tm, d), F32)],
        compiler_params=_cparams("parallel", "arbitrary"),
        name="proj_ffn",
    )(h, x, w_o, g.reshape(1, d), w1, w3, w2, (final_gain if final_norm else g).reshape(1, d))


def _mix_kernel(h_ref, hp_ref, st_ref, ln_ref, vf_ref, mu_ref, wr_ref, wk_ref, wv_ref,
                w1_ref, w2_ref, a1_ref, a2_ref, v1_ref, v2_ref, g1_ref, g2_ref,
                w0_ref, a0_ref, v0_ref,
                r_ref, lw_ref, k_ref, v_ref, a_ref, g_ref,
                xr_scr, xk_scr, xv_scr, hw_scr, ha_scr, hv_scr, hg_scr, *, has_v, seq_len):
    @pl.when(pl.program_id(1) == 0)
    def _():
        tm = h_ref.shape[0]
        xn = _rms(h_ref[...], ln_ref[...])
        row = lax.broadcasted_iota(jnp.int32, (tm, 1), 0)
        before_tile = _rms(hp_ref[7:8, :], ln_ref[...])
        xprev = jnp.where(row == 0, before_tile, pltpu.roll(xn, 1, axis=0))
        if seq_len >= tm:
            starts_here = pl.program_id(0) % (seq_len // tm) == 0
            xprev = jnp.where((row == 0) & starts_here, st_ref[0, 0:1, :], xprev)
        else:
            for s in range(tm // seq_len):
                xprev = jnp.where(row == s * seq_len, st_ref[0, s:s + 1, :], xprev)
        xx = xprev - xn
        mix = lambda i: (xn + xx * mu_ref[i:i + 1, :]).astype(BF16)
        xr_scr[...] = mix(0)
        hw_scr[...] = jnp.tanh(_dot(mix(1), w1_ref[...])).astype(BF16)
        xk_scr[...] = mix(2)
        xv = mix(3)
        xv_scr[...] = xv
        if has_v:
            hv_scr[...] = _dot(xv, v1_ref[...]).astype(BF16)
        ha_scr[...] = _dot(mix(4), a1_ref[...]).astype(BF16)
        hg_scr[...] = jax.nn.sigmoid(_dot(mix(5), g1_ref[...])).astype(BF16)

    r_ref[...] = _dot(xr_scr[...], wr_ref[...]).astype(r_ref.dtype)
    k_ref[...] = _dot(xk_scr[...], wk_ref[...]).astype(k_ref.dtype)
    v = _dot(xv_scr[...], wv_ref[...])
    if has_v:
        v = v + (vf_ref[...] - v) * jax.nn.sigmoid(v0_ref[...] + _dot(hv_scr[...], v2_ref[...]))
    v_ref[...] = v.astype(v_ref.dtype)
    w_log = -_softplus(-(w0_ref[...] + _dot(hw_scr[...], w2_ref[...]))) - 0.5
    lw_ref[...] = -jnp.exp(w_log)
    a_ref[...] = jax.nn.sigmoid(a0_ref[...] + _dot(ha_scr[...], a2_ref[...])).astype(a_ref.dtype)
    g_ref[...] = _dot(hg_scr[...], g2_ref[...]).astype(g_ref.dtype)


def _mix(h, shift_state, seq_len, ln, vfirst, mu, wr, wk, wv, w1, w2, a1, a2, v1, v2, g1, g2, w0, a0, v0,
         has_v, layer, layer_v):
    m, d = h.shape
    tm, tn = _row_tile(m), COL_TILE
    assert seq_len % tm == 0 or tm % seq_len == 0
    n_starts = max(1, tm // seq_len)
    seq_of = [[(i * tm + s * seq_len) // seq_len for s in range(n_starts)] for i in range(m // tm)]
    starts = shift_state[jnp.asarray(seq_of)]
    lw_, la_, lv_, lg_ = w1.shape[-1], a1.shape[-1], v1.shape[-1], g1.shape[-1]
    row = lambda i, j: (i, 0)
    col = lambda i, j: (0, j)
    tile = lambda i, j: (i, j)
    fixed = lambda i, j: (0, 0)
    down = lambda w, lyr: _layer_spec(w, lyr, (d, w.shape[-1]), fixed)
    up = lambda w, lyr: _layer_spec(w, lyr, (w.shape[-2], tn), col)
    outs = pl.pallas_call(
        functools.partial(_mix_kernel, has_v=has_v, seq_len=seq_len),
        grid=(m // tm, d // tn),
        in_specs=[pl.BlockSpec((tm, d), row),
                  pl.BlockSpec((8, d), lambda i, j: (jnp.maximum(i * (tm // 8) - 1, 0), 0)),
                  pl.BlockSpec((1, n_starts, d), lambda i, j: (i, 0, 0)),
                  pl.BlockSpec((1, d), fixed),
                  pl.BlockSpec((tm, tn), tile),
                  pl.BlockSpec((8, d), fixed),
                  up(wr, layer), up(wk, layer), up(wv, layer),
                  down(w1, layer), up(w2, layer), down(a1, layer), up(a2, layer),
                  down(v1, layer_v), up(v2, layer_v), down(g1, layer), up(g2, layer),
                  pl.BlockSpec((1, tn), col), pl.BlockSpec((1, tn), col), pl.BlockSpec((1, tn), col)],
        out_specs=[pl.BlockSpec((tm, tn), tile) for _ in range(6)],
        out_shape=[jax.ShapeDtypeStruct((m, d), dt) for dt in (BF16, F32, BF16, BF16, BF16, BF16)],
        scratch_shapes=[pltpu.VMEM((tm, d), BF16), pltpu.VMEM((tm, d), BF16), pltpu.VMEM((tm, d), BF16),
                        pltpu.VMEM((tm, lw_), BF16), pltpu.VMEM((tm, la_), BF16),
                        pltpu.VMEM((tm, lv_), BF16), pltpu.VMEM((tm, lg_), BF16)],
        compiler_params=_cparams("parallel", "arbitrary"),
        name="rwkv_mix_proj",
    )(h, h, starts, ln.reshape(1, d), vfirst, mu, wr, wk, wv, w1, w2, a1, a2, v1, v2, g1, g2, w0, a0, v0)
    return outs


SCAN_CHUNK = 64
SCAN_PAIRS = 16


def _scan_kernel(r_ref, lw_ref, kr_ref, v_ref, a_ref, g_ref,
                 kk_ref, ka_ref, rk_ref, lnw_ref, lnb_ref, s0_ref,
                 y_ref, sout_ref, st_scr):
    c, n_chunks = pl.program_id(2), pl.num_programs(2)
    C = SCAN_CHUNK
    R = 2 * C
    n_pairs = st_scr.shape[0]
    f32 = lambda ref, sl: ref[0, :, sl].astype(F32)

    @pl.when(c == 0)
    def _():
        st_scr[...] = s0_ref[0]

    lane_c = lax.broadcasted_iota(jnp.int32, (C, LANES), 1)
    head0 = lane_c < HEAD_A
    row = lax.broadcasted_iota(jnp.int32, (R, R), 0)
    col = lax.broadcasted_iota(jnp.int32, (R, R), 1)
    same = (row // C) == (col // C)
    strict = same & ((col % C) < (row % C))
    incl = same & ((col % C) <= (row % C))
    row2 = lax.broadcasted_iota(jnp.int32, (2 * LANES, LANES), 0)
    col2 = lax.broadcasted_iota(jnp.int32, (2 * LANES, LANES), 1)
    ones2 = (((row2 % LANES) // HEAD_A) == (col2 // HEAD_A)).astype(BF16)
    head_sum = lambda x: _dot_l_stacked(x, ones2, 2)
    ti = lax.broadcasted_iota(jnp.int32, (C, 3 * C), 0)
    tj = lax.broadcasted_iota(jnp.int32, (C, 3 * C), 1)
    tri3 = ((tj % C) <= ti).astype(BF16)
    zeros_c = jnp.zeros((C, LANES), F32)

    def stack(x):
        return jnp.concatenate([jnp.where(head0, x, 0.0), jnp.where(head0, 0.0, x)], axis=0)

    sls = [slice(p * LANES, (p + 1) * LANES) for p in range(n_pairs)]
    each = lambda f, *cols: [f(*xs) for xs in zip(*cols)]
    bf = lambda xs: [x.astype(BF16) for x in xs]

    r = [f32(r_ref, sl) for sl in sls]
    lw = [lw_ref[0, :, sl] for sl in sls]
    kr = [f32(kr_ref, sl) for sl in sls]
    v = [f32(v_ref, sl) for sl in sls]
    a = [f32(a_ref, sl) for sl in sls]

    kkr = each(lambda x, sl: x * kk_ref[:, sl], kr, sls)
    ss = each(lambda x: head_sum(x * x), kkr)
    kk = each(lambda x, s: x / jnp.maximum(jnp.sqrt(s), 1e-12), kkr, ss)
    k = each(lambda x, y, sl: x * (1.0 + (y - 1.0) * ka_ref[:, sl]), kr, a, sls)
    b = each(lambda x, y: x * y, kk, a)

    cum = each(lambda x: _dot(tri3, jnp.concatenate(_split(x, 3), axis=0)), lw)
    g_in = each(jnp.exp, cum)
    g_ex = each(lambda x, y: jnp.exp(x - y), cum, lw)
    g_inv = each(lambda x: jnp.exp(-x), cum)
    g_tail = each(lambda x: jnp.exp(x[C - 1:C, :] - x), cum)

    at = each(lambda x, y: stack(-x * y), kk, g_ex)
    at_b = bf(at)
    rt = bf(each(lambda x, y: stack(x * y), r, g_in))
    bt = bf(each(lambda x, y: stack(x * y), b, g_inv))
    kt = bf(each(lambda x, y: stack(x * y), k, g_inv))
    bh = each(lambda x, y: stack(x * y), b, g_tail)
    kh = each(lambda x, y: stack(x * y), k, g_tail)
    vs = bf(each(stack, v))

    sc = each(lambda w, x, y, z: _dot_nt(jnp.concatenate([w, x], axis=0), jnp.concatenate([y, z], axis=0)),
              at_b, rt, bt, kt)
    a_ab = each(lambda x: jnp.where(strict, x[:R, :R], 0.0), sc)
    a_ak = bf(each(lambda x: jnp.where(strict, x[:R, R:], 0.0), sc))
    a_rb = bf(each(lambda x: jnp.where(incl, x[R:, :R], 0.0), sc))
    a_rk = bf(each(lambda x: jnp.where(incl, x[R:, R:], 0.0), sc))

    st = [st_scr[p] for p in range(n_pairs)]
    st_b = bf(st)
    u = each(lambda w, y, s, z: _dot(jnp.concatenate([w, y], axis=1), jnp.concatenate([s, z], axis=0)),
             at_b, a_ak, st_b, vs)
    pw = a_ab
    n_sq = int(math.log2(C))
    for it in range(n_sq):
        pw_b = bf(pw)
        u = each(lambda y, z: y + _dot(z, y.astype(BF16)), u, pw_b)
        if it + 1 < n_sq:
            pw = each(lambda z: _dot(z, z), pw_b)
    u_b = bf(u)
    uv = each(lambda y, t: jnp.concatenate([y, t], axis=0), u_b, vs)
    o_st = each(lambda q, w, z, s, y: _dot(jnp.concatenate([q, w, z], axis=1), jnp.concatenate([s, y], axis=0)),
                rt, a_rb, a_rk, st_b, uv)
    o = each(lambda y: y[:C] + y[C:], o_st)

    def decay_col(x):
        x_t = jnp.concatenate([x, zeros_c], axis=0).T if C < LANES else x.T
        return jnp.exp(jnp.sum(x_t, axis=1, keepdims=True))

    dcol = each(decay_col, lw)
    st_new = each(lambda dc, s, w, z, y: dc * s + _dot(jnp.concatenate([w.T, z.T], axis=1).astype(BF16), y),
                  dcol, st, bh, kh, uv)
    for p in range(n_pairs):
        st_scr[p] = st_new[p]

    mean = each(lambda y: head_sum(y) * (1.0 / HEAD_A), o)
    dlt = each(lambda y, z: y - z, o, mean)
    var = each(lambda y: head_sum(y * y) * (1.0 / HEAD_A), dlt)
    on = each(lambda y, z, sl: y * lax.rsqrt(z + LNX_EPS) * lnw_ref[:, sl] + lnb_ref[:, sl], dlt, var, sls)
    bonus = each(lambda x, y, z, sl: head_sum(x * y * rk_ref[:, sl]) * z, r, k, v, sls)
    for p, sl in enumerate(sls):
        y_ref[0, :, sl] = ((on[p] + bonus[p]) * f32(g_ref, sl)).astype(y_ref.dtype)

    @pl.when(c == n_chunks - 1)
    def _():
        sout_ref[0] = st_scr[...]


def _scan(r, lw, kr, v, a, g, k_k, k_a, r_k, lnx_w, lnx_b, s0_bd):
    bsz, t, d = r.shape
    C = SCAN_CHUNK
    P = min(SCAN_PAIRS, d // LANES)
    w = P * LANES
    n_pg = d // w
    seq = pl.BlockSpec((1, C, w), lambda b, q, c: (b, c, q))
    vec = pl.BlockSpec((1, w), lambda b, q, c: (0, q))
    sbd = pl.BlockSpec((1, P, LANES, LANES), lambda b, q, c: (b, q, 0, 0))
    y, s_out = pl.pallas_call(
        _scan_kernel,
        grid=(bsz, n_pg, t // C),
        in_specs=[seq] * 6 + [vec] * 5 + [sbd],
        out_specs=[seq, sbd],
        out_shape=[jax.ShapeDtypeStruct((bsz, t, d), BF16),
                   jax.ShapeDtypeStruct(s0_bd.shape, F32)],
        scratch_shapes=[pltpu.VMEM((P, LANES, LANES), F32)],
        compiler_params=_cparams("parallel", "parallel", "arbitrary"),
        name="wkv7_scan",
    )(r, lw, kr, v, a, g, k_k.reshape(1, d), k_a.reshape(1, d), r_k.reshape(1, d),
      lnx_w.reshape(1, d), lnx_b.reshape(1, d), s0_bd)
    return y, s_out


def _state_to_bd(s):
    bsz, h = s.shape[:2]
    st = jnp.swapaxes(s, 2, 3).reshape(bsz, h // 2, 2, HEAD_A, HEAD_A)
    eye = jnp.eye(2, dtype=s.dtype)
    bd = st[:, :, :, :, None, :] * eye[None, None, :, None, :, None]
    return bd.reshape(bsz, h // 2, 2 * HEAD_A, 2 * HEAD_A)


def _bd_to_state(bd):
    bsz, hp = bd.shape[:2]
    x = bd.reshape(bsz, hp, 2, HEAD_A, 2, HEAD_A)
    st = jnp.stack([x[:, :, 0, :, 0, :], x[:, :, 1, :, 1, :]], axis=2)
    return jnp.swapaxes(st.reshape(bsz, hp * 2, HEAD_A, HEAD_A), 2, 3)


N_PIECES = 3


def _forget_bias_kernel(x_ref, eq_ref, ek_ref, qa_ref, ka_ref, carry_scr):
    @pl.when(pl.program_id(1) == 0)
    def _():
        carry_scr[...] = jnp.zeros_like(carry_scr)

    tt = x_ref.shape[1]
    ti = lax.broadcasted_iota(jnp.int32, (tt, N_PIECES * tt), 0)
    tj = lax.broadcasted_iota(jnp.int32, (tt, N_PIECES * tt), 1)
    tri = ((tj % tt) <= ti).astype(BF16)
    c = _dot(tri, jnp.concatenate(_split(x_ref[0], N_PIECES), axis=0)) + carry_scr[...]
    carry_scr[...] = c[tt - 1:tt, :]
    pieces = jnp.concatenate(_split(c * LOG2E, N_PIECES), axis=1)
    lane = lax.broadcasted_iota(jnp.int32, qa_ref.shape[1:], 1) % HEAD_B
    qa = _dot(pieces, eq_ref[...]) + jnp.where((lane >= N_PIECES) & (lane < 2 * N_PIECES), 1.0, 0.0)
    ka = _dot(pieces, ek_ref[...]) + jnp.where(lane < N_PIECES, 1.0, 0.0)
    qa_ref[0] = qa.astype(BF16)
    ka_ref[0] = ka.astype(BF16)


def _forget_bias_lanes(logf_lanes, n_heads):
    bsz, t, _ = logf_lanes.shape
    tt = _row_tile(t)
    d = n_heads * HEAD_B
    src = jnp.arange(N_PIECES * LANES)
    piece, head = src // LANES, src % LANES
    place = lambda lane0: (jnp.arange(d)[None, :] == (head * HEAD_B + lane0 + piece)[:, None]) & (
        head < n_heads)[:, None]
    eq = place(0).astype(BF16)
    ek = -place(N_PIECES).astype(BF16)
    out = pl.BlockSpec((1, tt, d), lambda b, i: (b, i, 0))
    return pl.pallas_call(
        _forget_bias_kernel,
        grid=(bsz, t // tt),
        in_specs=[pl.BlockSpec((1, tt, LANES), lambda b, i: (b, i, 0)),
                  pl.BlockSpec(eq.shape, lambda b, i: (0, 0)), pl.BlockSpec(ek.shape, lambda b, i: (0, 0))],
        out_specs=[out, out],
        out_shape=[jax.ShapeDtypeStruct((bsz, t, d), BF16)] * 2,
        scratch_shapes=[pltpu.VMEM((1, LANES), F32)],
        compiler_params=_cparams("parallel", "arbitrary"),
        name="forget_bias_lanes",
    )(logf_lanes, eq, ek)


def _flash_kernel(q_ref, qa_ref, k_ref, ka_ref, v_ref, o_ref, m_scr, acc_scr):
    i, j = pl.program_id(1), pl.program_id(2)
    tq, tk = q_ref.shape[1], k_ref.shape[1]
    n_heads = q_ref.shape[2] // HEAD_B

    @pl.when(j == 0)
    def _():
        m_scr[...] = jnp.full_like(m_scr, NEG_BIG)
        acc_scr[...] = jnp.zeros_like(acc_scr)

    def block(rows, n_keys, masked):
        n_rows = rows.stop - rows.start
        keys = slice(0, n_keys)
        if masked:
            qpos = lax.broadcasted_iota(jnp.int32, (n_rows, n_keys), 0) + rows.start
            kpos = lax.broadcasted_iota(jnp.int32, (n_rows, n_keys), 1)
            keep = kpos <= qpos
        ones = jnp.ones((n_keys, HEAD_B), BF16)
        for h in range(n_heads):
            hs = slice(h * HEAD_B, (h + 1) * HEAD_B)
            q_aug = jnp.concatenate([q_ref[0, rows, hs], qa_ref[0, rows, hs]], axis=1)
            k_aug = jnp.concatenate([k_ref[0, keys, hs], ka_ref[0, keys, hs]], axis=1)
            s = _dot_nt(q_aug, k_aug)
            if masked:
                s = jnp.where(keep, s, NEG_BIG)
            m_prev = m_scr[h, rows]
            m_next = jnp.maximum(m_prev, jnp.max(s, axis=1, keepdims=True))
            alpha = jnp.exp2(m_prev - m_next)
            p = jnp.exp2(s - jnp.concatenate([m_next] * (n_keys // LANES), axis=1))
            v_aug = jnp.concatenate([v_ref[0, keys, hs], ones], axis=1)
            acc_scr[h, rows] = (jnp.concatenate([alpha, alpha], axis=1) * acc_scr[h, rows]
                                + _dot(p.astype(BF16), v_aug))
            m_scr[h, rows] = m_next

    @pl.when(j < i)
    def _():
        block(slice(0, tq), tk, False)

    @pl.when(j == i)
    def _():
        half = tq // 2
        if half % LANES == 0:
            block(slice(0, half), half, True)
            block(slice(half, tq), tk, True)
        else:
            block(slice(0, tq), tk, True)
        for h in range(n_heads):
            hs = slice(h * HEAD_B, (h + 1) * HEAD_B)
            acc = acc_scr[h]
            o_ref[0, :, hs] = (acc[:, :HEAD_B] / acc[:, HEAD_B:]).astype(o_ref.dtype)


def _flash_attention(q, q_aug, k, k_aug, v):
    bsz, t, d = q.shape
    h = d // HEAD_B
    tq = _row_tile(t)
    nq = t // tq
    q_idx = lambda b, i, j: (b, i, 0)
    kv_idx = lambda b, i, j: (b, jnp.minimum(j, i), 0)
    return pl.pallas_call(
        _flash_kernel,
        grid=(bsz, nq, nq),
        in_specs=[pl.BlockSpec((1, tq, d), q_idx), pl.BlockSpec((1, tq, d), q_idx),
                  pl.BlockSpec((1, tq, d), kv_idx), pl.BlockSpec((1, tq, d), kv_idx),
                  pl.BlockSpec((1, tq, d), kv_idx)],
        out_specs=pl.BlockSpec((1, tq, d), q_idx),
        out_shape=jax.ShapeDtypeStruct((bsz, t, d), BF16),
        scratch_shapes=[pltpu.VMEM((h, tq, LANES), F32), pltpu.VMEM((h, tq, 2 * HEAD_B), F32)],
        compiler_params=_cparams("parallel", "parallel", "arbitrary"),
        name="fox_prompt_attention",
    )(q, q_aug, k, k_aug, v)


PAGES_PER_STEP = 8
SLAB = 8


def _page_suffix_kernel(lf_ref, suf_ref, tot_ref):
    pb, h, page = lf_ref.shape
    u = lax.broadcasted_iota(jnp.int32, (page, page), 0)
    s = lax.broadcasted_iota(jnp.int32, (page, page), 1)
    lf = lf_ref[...].reshape(pb * h, page) * LOG2E
    suf_ref[...] = _dot_l(lf, (u > s).astype(BF16), 3).reshape(pb, h, page)
    tot_ref[...] = _dot_l(lf, jnp.ones((page, page), BF16), 3).reshape(pb, h, page)


def _page_suffix(lf_t):
    n_phys, h, page = lf_t.shape
    pb = math.gcd(n_phys, 64)
    spec = pl.BlockSpec((pb, h, page), lambda i: (i, 0, 0))
    return pl.pallas_call(
        _page_suffix_kernel,
        grid=(n_phys // pb,),
        in_specs=[spec],
        out_specs=[spec, spec],
        out_shape=[jax.ShapeDtypeStruct(lf_t.shape, F32)] * 2,
        compiler_params=_cparams("parallel"),
        name="page_logf_suffix",
    )(lf_t)


def _paged_kernel(pt_ref, q_ref, kn_ref, vn_ref, gn_ref, *refs, n_group, n_slab):
    del pt_ref
    p, n_steps = pl.program_id(1), pl.num_programs(1)
    n_kv = n_group * n_slab
    kp_refs, vp_refs = refs[:n_kv], refs[n_kv:2 * n_kv]
    suf_refs, tot_refs = refs[2 * n_kv:3 * n_kv], refs[3 * n_kv:4 * n_kv]
    o_ref, m_scr, l_scr, acc_scr, cq_scr, carry_scr = refs[4 * n_kv:]
    rows = q_ref.shape[1]
    n_heads = kn_ref.shape[1]
    nq = rows // n_heads
    page = kn_ref.shape[2]
    srows = SLAB * nq
    flat = page * SLAB

    def update(sl, s, pv_of):
        m_old = m_scr[sl]
        m_new = jnp.maximum(m_old, jnp.max(s, axis=1, keepdims=True))
        alpha = jnp.exp2(m_old - m_new)
        pr = jnp.exp2(s - m_new)
        l_scr[sl] = alpha * l_scr[sl] + jnp.sum(pr, axis=1, keepdims=True)
        acc_scr[sl] = alpha * acc_scr[sl] + pv_of(pr)
        m_scr[sl] = m_new

    @pl.when(p == 0)
    def _():
        m_scr[...] = jnp.full_like(m_scr, NEG_BIG)
        l_scr[...] = jnp.zeros_like(l_scr)
        acc_scr[...] = jnp.zeros_like(acc_scr)
        carry_scr[...] = jnp.zeros_like(carry_scr)
        row = lax.broadcasted_iota(jnp.int32, (rows, page), 0)
        lane = lax.broadcasted_iota(jnp.int32, (rows, page), 1)
        u = lax.broadcasted_iota(jnp.int32, (page, page), 0)
        s_ = lax.broadcasted_iota(jnp.int32, (page, page), 1)
        gn = gn_ref[0] * LOG2E
        qi = row % nq
        c_new_col = jnp.sum(jnp.where(lane <= qi, gn, 0.0), axis=1, keepdims=True)
        hi, mid, lo = [x.astype(F32) for x in _split(c_new_col, 3)]
        lane_a = lax.broadcasted_iota(jnp.int32, (rows, LANES), 1)
        cq_scr[...] = jnp.where(lane_a == 0, hi, jnp.where(lane_a == 1, mid,
                                                           jnp.where(lane_a == 2, lo, 0.0))).astype(BF16)
        c_new_row = _dot_l(gn, (u <= s_).astype(BF16), 3)
        s = jnp.concatenate([_dot_nt(q_ref[0, h * nq:(h + 1) * nq, :], kn_ref[0, h])
                             for h in range(n_heads)], axis=0) + (c_new_col - c_new_row)
        s = jnp.where(lane <= qi, s, NEG_BIG)
        for c in range(n_slab):
            sl = slice(c * srows, (c + 1) * srows)
            update(sl, s[sl], lambda pr, c=c: jnp.concatenate(
                [_dot(pr[hh * nq:(hh + 1) * nq, :].astype(BF16), vn_ref[0, c * SLAB + hh])
                 for hh in range(SLAB)], axis=0))

    row_f = lax.broadcasted_iota(jnp.int32, (srows, flat), 0)
    lane_f = lax.broadcasted_iota(jnp.int32, (srows, flat), 1)
    same_head = (row_f // nq) == (lane_f % SLAB)
    lane_k = lax.broadcasted_iota(jnp.int32, (flat, LANES), 1)
    ones_k = (lane_k < 3).astype(BF16)
    slabs = range(n_slab)
    sls = [slice(c * srows, (c + 1) * srows) for c in slabs]
    q_aug = [jnp.concatenate([q_ref[0, sl, :], cq_scr[sl, :]], axis=1) for sl in sls]
    carry = [carry_scr[c:c + 1, :] for c in slabs]
    s = [[] for _ in slabs]
    for g in range(n_group):
        for c in slabs:
            k_rows = kp_refs[g * n_slab + c][...].reshape(flat, HEAD_B).astype(BF16)
            z = _dot_nt(q_aug[c], jnp.concatenate([k_rows, ones_k], axis=1))
            s[c].append(jnp.where(same_head, z + (suf_refs[g * n_slab + c][...] + carry[c]), NEG_BIG))
            carry[c] = carry[c] + tot_refs[g * n_slab + c][...]
    s = [jnp.concatenate(x, axis=1) for x in s]
    m_old = [m_scr[sl] for sl in sls]
    m_new = [jnp.maximum(mo, jnp.max(x, axis=1, keepdims=True)) for mo, x in zip(m_old, s)]
    alpha = [jnp.exp2(mo - mn) for mo, mn in zip(m_old, m_new)]
    pr = [jnp.exp2(x - mn) for x, mn in zip(s, m_new)]
    v_rows = [jnp.concatenate([vp_refs[g * n_slab + c][...].reshape(flat, HEAD_B)
                               for g in range(n_group)], axis=0).astype(BF16) for c in slabs]
    pv = [_dot(x.astype(BF16), vr) for x, vr in zip(pr, v_rows)]
    for c, sl in enumerate(sls):
        carry_scr[c:c + 1, :] = carry[c]
        l_scr[sl] = alpha[c] * l_scr[sl] + jnp.sum(pr[c], axis=1, keepdims=True)
        acc_scr[sl] = alpha[c] * acc_scr[sl] + pv[c]
        m_scr[sl] = m_new[c]

    @pl.when(p == n_steps - 1)
    def _():
        o_ref[0] = acc_scr[...] / l_scr[...]


def _paged_attention(page_table, q, k_new, v_new, g_new, cache_k, cache_v, suffix, total):
    dbs, rows, _ = q.shape
    n_heads = k_new.shape[1]
    n_slab = n_heads // SLAB
    n_pages = page_table.shape[1]
    page = k_new.shape[2]
    flat = page * SLAB
    n_group = math.gcd(PAGES_PER_STEP, n_pages)
    per_b = lambda b, p, pt: (b, 0, 0)
    per_b4 = lambda b, p, pt: (b, 0, 0, 0)
    slot_page = lambda b, p, pt, g: pt[b, n_pages - 1 - (p * n_group + g)]
    slots = [(g, c) for g in range(n_group) for c in range(n_slab)]
    kv_specs = [pl.BlockSpec((None, page, None, SLAB, HEAD_B),
                             lambda b, p, pt, g=g, c=c: (slot_page(b, p, pt, g), 0, c, 0, 0)) for g, c in slots]
    row_specs = [pl.BlockSpec((None, None, 1, flat),
                              lambda b, p, pt, g=g, c=c: (slot_page(b, p, pt, g), c, 0, 0)) for g, c in slots]
    n_kv = len(slots)
    grid_spec = pltpu.PrefetchScalarGridSpec(
        num_scalar_prefetch=1,
        grid=(dbs, n_pages // n_group),
        in_specs=[pl.BlockSpec((1, rows, HEAD_B), per_b),
                  pl.BlockSpec((1, n_heads, page, HEAD_B), per_b4),
                  pl.BlockSpec((1, n_heads, page, HEAD_B), per_b4),
                  pl.BlockSpec((1, rows, page), per_b)] + kv_specs + kv_specs + row_specs + row_specs,
        out_specs=pl.BlockSpec((1, rows, HEAD_B), per_b),
        scratch_shapes=[pltpu.VMEM((rows, 1), F32), pltpu.VMEM((rows, 1), F32),
                        pltpu.VMEM((rows, HEAD_B), F32), pltpu.VMEM((rows, LANES), BF16),
                        pltpu.VMEM((8, flat), F32)])
    return pl.pallas_call(
        functools.partial(_paged_kernel, n_group=n_group, n_slab=n_slab),
        grid_spec=grid_spec,
        out_shape=jax.ShapeDtypeStruct((dbs, rows, HEAD_B), F32),
        compiler_params=_cparams("parallel", "arbitrary"),
        name="fox_paged_attention",
    )(page_table, q, k_new, v_new, g_new, *([cache_k] * n_kv), *([cache_v] * n_kv),
      *([suffix] * n_kv), *([total] * n_kv))


def _trunk(x, shift0, wkv0, past, p):
    bsz, t, d = x.shape
    m = bsz * t
    n_a = p["tm_w_r"].shape[0]
    depth = p["ln1"].shape[0]
    h_b = d // HEAD_B
    h = x.reshape(m, d)
    v_first = None
    shifts, states = [], []
    t_scan = -(-t // SCAN_CHUNK) * SCAN_CHUNK
    heads_first = lambda z: jnp.swapaxes(z.reshape(bsz, t, h_b, HEAD_B), 1, 2)
    for l in range(depth):
        if l == n_a:
            k_new, k_bf, v_new, v_bf = _norm_matmul(h, p["kv_norm"], [p["fox_w_k"], p["fox_w_v"]], (F32, BF16))
            (lf_pad,) = _norm_matmul(h, p["kv_norm"], [p["fox_w_f"]], (F32,), epilogue="log_sigmoid",
                                     bias=p["fox_b_f"])
            logf_new = lf_pad[:, :h_b].reshape(bsz, t, h_b)
            if past is None:
                q_aug, k_aug = _forget_bias_lanes(lf_pad.reshape(bsz, t, LANES), h_b)
            else:
                page = past[0].shape[1]
                pad_keys = lambda z: jnp.pad(heads_first(z), ((0, 0), (0, 0), (0, page - t), (0, 0)))
                k_pad, v_pad = pad_keys(k_bf), pad_keys(v_bf)
                g_new = jnp.repeat(jnp.swapaxes(logf_new, 1, 2), t, axis=1)
                g_new = jnp.pad(g_new, ((0, 0), (0, 0), (0, page - t)))
        if l < n_a:
            shifts.append(_rmsnorm(h.reshape(bsz, t, d)[:, -1], p["ln1"][l]))
            has_v = l > 0
            lv = max(l - 1, 0)
            r, lw, kr, v, a, g = _mix(
                h, shift0[l], t, p["ln1"][l], v_first if has_v else h, p["mu8"][l],
                p["tm_w_r"], p["tm_w_k"], p["tm_w_v"],
                p["tm_w1"], p["tm_w2"], p["tm_a1"], p["tm_a2"],
                p["tm_v1"], p["tm_v2"], p["tm_g1"], p["tm_g2"],
                p["tm_w0"][l].reshape(1, d), p["tm_a0"][l].reshape(1, d), p["tm_v0"][lv].reshape(1, d),
                has_v, l, lv)
            if l == 0:
                v_first = v
            seq = lambda z: jnp.pad(z.reshape(bsz, t, d), ((0, 0), (0, t_scan - t), (0, 0)))
            y, s_bd = _scan(seq(r), seq(lw), seq(kr), seq(v), seq(a), seq(g),
                            p["tm_k_k"][l], p["tm_k_a"][l], p["tm_r_k"][l].reshape(d),
                            p["tm_lnx_w"][l], p["tm_lnx_b"][l], _state_to_bd(wkv0[l]))
            states.append(_bd_to_state(s_bd))
            mixed, w_o, layer_o = y[:, :t].reshape(m, d), p["tm_w_o"], l
        else:
            j = l - n_a
            (q,) = _norm_matmul(h, p["ln1"][l], [p["fox_w_q"]], (BF16,), epilogue="scale",
                                scale=HEAD_B ** -0.5 * LOG2E, layer=j)
            if past is None:
                att = _flash_attention(q.reshape(bsz, t, d), q_aug, k_bf.reshape(bsz, t, d), k_aug,
                                       v_bf.reshape(bsz, t, d)).reshape(m, d)
            else:
                cache_k, cache_v, suffix, total, page_table = past
                o = _paged_attention(page_table, heads_first(q).reshape(bsz, h_b * t, HEAD_B),
                                     k_pad, v_pad, g_new, cache_k, cache_v, suffix, total)
                att = jnp.swapaxes(o.reshape(bsz, h_b, t, HEAD_B), 1, 2).reshape(m, d).astype(BF16)
            mixed, w_o, layer_o = att, p["fox_w_o"], j
        h = _proj_ffn(h, mixed, w_o, layer_o, p["ln2"][l], p["ffn_w1"], p["ffn_w3"], p["ffn_w2"], l,
                      final_gain=p["ln_out"] if l == depth - 1 else None)
    h_sh = (bsz, t, h_b, HEAD_B)
    return (h.reshape(bsz, t, d), jnp.stack(shifts), jnp.stack(states),
            k_new.reshape(h_sh), v_new.reshape(h_sh), logf_new)


def kernel(x_prompt, x_sample, state_wkv, state_shift, cache_k, cache_v, cache_logf, page_table, ln1, ln2, ln_out, tm_mu, tm_w_r, tm_w_k, tm_w_v, tm_w_o, tm_w0, tm_w1, tm_w2, tm_a0, tm_a1, tm_a2, tm_v0, tm_v1, tm_v2, tm_g1, tm_g2, tm_k_k, tm_k_a, tm_r_k, tm_lnx_w, tm_lnx_b, kv_norm, fox_w_k, fox_w_v, fox_w_f, fox_b_f, fox_w_q, fox_w_o, ffn_w1, ffn_w3, ffn_w2):
    bf = lambda w: w.astype(BF16)
    n_a, d = tm_w0.shape
    h_b = d // HEAD_B
    lora = lambda w1, w2: (bf(jnp.pad(w1, ((0, 0), (0, 0), (0, -w1.shape[2] % LANES)))),
                           bf(jnp.pad(w2, ((0, 0), (0, -w2.shape[1] % LANES), (0, 0)))))
    w1, w2 = lora(tm_w1, tm_w2)
    a1, a2 = lora(tm_a1, tm_a2)
    if tm_v1.shape[0] == 0:
        tm_v0 = jnp.zeros((1, d), F32)
        tm_v1 = jnp.zeros((1, d, LANES), F32)
        tm_v2 = jnp.zeros((1, LANES, d), F32)
    v1, v2 = lora(tm_v1, tm_v2)
    g1, g2 = lora(tm_g1, tm_g2)
    p = dict(ln1=ln1, ln2=ln2, ln_out=ln_out,
             mu8=jnp.pad(tm_mu, ((0, 0), (0, 2), (0, 0))),
             tm_w_r=bf(tm_w_r), tm_w_k=bf(tm_w_k), tm_w_v=bf(tm_w_v), tm_w_o=bf(tm_w_o),
             tm_w0=tm_w0, tm_w1=w1, tm_w2=w2, tm_a0=tm_a0, tm_a1=a1, tm_a2=a2,
             tm_v0=tm_v0, tm_v1=v1, tm_v2=v2, tm_g1=g1, tm_g2=g2,
             tm_k_k=tm_k_k, tm_k_a=tm_k_a, tm_r_k=tm_r_k, tm_lnx_w=tm_lnx_w, tm_lnx_b=tm_lnx_b,
             kv_norm=kv_norm, fox_w_k=bf(fox_w_k), fox_w_v=bf(fox_w_v),
             fox_w_f=bf(jnp.pad(fox_w_f, ((0, 0), (0, LANES - h_b)))),
             fox_b_f=jnp.pad(fox_b_f, (0, LANES - h_b)).reshape(1, LANES),
             fox_w_q=bf(fox_w_q), fox_w_o=bf(fox_w_o),
             ffn_w1=bf(ffn_w1), ffn_w3=bf(ffn_w3), ffn_w2=ffn_w2)

    bsz = x_prompt.shape[0]
    h_a = d // HEAD_A
    shift0 = jnp.zeros((n_a, bsz, d), x_prompt.dtype)
    wkv0 = jnp.zeros((n_a, bsz, h_a, HEAD_A, HEAD_A), F32)
    n_phys, page = cache_k.shape[:2]
    n_slab = h_b // SLAB
    slabs = lambda c: c.reshape(n_phys, page, n_slab, SLAB, HEAD_B)
    key_head_rows = lambda z: jnp.transpose(z.reshape(n_phys, n_slab, SLAB, page), (0, 1, 3, 2)).reshape(
        n_phys, n_slab, 1, page * SLAB)
    suffix, total = _page_suffix(jnp.swapaxes(cache_logf, 1, 2))
    past = (slabs(cache_k), slabs(cache_v), key_head_rows(suffix), key_head_rows(total), page_table)
    y_p, sh_p, wkv_p, k_p, v_p, lf_p = _trunk(x_prompt, shift0, wkv0, None, p)
    y_s, sh_s, wkv_s, k_s, v_s, lf_s = _trunk(x_sample, state_shift, state_wkv, past, p)
    return (y_p, y_s, wkv_p, sh_p, k_p, v_p, lf_p, wkv_s, sh_s, k_s, v_s, lf_s)
```

```python
import functools
import math

import jax
import jax.numpy as jnp
from jax import lax
from jax.experimental import pallas as pl
from jax.experimental.pallas import tpu as pltpu

F32 = jnp.float32
BF16 = jnp.bfloat16

HEAD_A = 64
HEAD_B = 128
LANES = 128
RMS_EPS = 1e-6
LNX_EPS = 64e-5
NEG_BIG = -1e30
LOG2E = math.log2(math.e)
VMEM_LIMIT = 56 * 1024 * 1024

ROW_TILE = 512
WIDE_ROW_TILE = 1024
COL_TILE = 512


def _cparams(*sem):
    return pltpu.CompilerParams(dimension_semantics=sem, vmem_limit_bytes=VMEM_LIMIT)


def _dot(a, b):
    return jnp.dot(a, b, preferred_element_type=F32)


def _dot_nt(a, b):
    return lax.dot_general(a, b, (((1,), (1,)), ((), ())), preferred_element_type=F32)


def _split(x, n):
    pieces = []
    for _ in range(n - 1):
        hi = x.astype(BF16)
        pieces.append(hi)
        x = x - hi.astype(F32)
    pieces.append(x.astype(BF16))
    return pieces


def _dot_l(x, w, n):
    return sum(_dot(piece, w) for piece in _split(x, n))


def _dot_l_stacked(x, w_stacked, n):
    return _dot(jnp.concatenate(_split(x, n), axis=1), w_stacked)


def _rms(x, g):
    return x * lax.rsqrt(jnp.mean(x * x, axis=-1, keepdims=True) + RMS_EPS) * g


def _softplus(z):
    return jnp.maximum(z, 0.0) + jnp.log1p(jnp.exp(-jnp.abs(z)))


def _row_tile(m):
    return min(ROW_TILE, m)


def _layer_spec(w, layer, block, index, pipeline_mode=None):
    if w.ndim == 2:
        return pl.BlockSpec(block, index, pipeline_mode=pipeline_mode)
    return pl.BlockSpec((None,) + block, lambda *args: (layer,) + index(*args), pipeline_mode=pipeline_mode)


def _rmsnorm_kernel(x_ref, g_ref, o_ref):
    o_ref[...] = _rms(x_ref[...], g_ref[...])


def _rmsnorm(x, g):
    m, d = x.shape
    tm = _row_tile(m)
    return pl.pallas_call(
        _rmsnorm_kernel,
        grid=(m // tm,),
        in_specs=[pl.BlockSpec((tm, d), lambda i: (i, 0)), pl.BlockSpec((1, d), lambda i: (0, 0))],
        out_specs=pl.BlockSpec((tm, d), lambda i: (i, 0)),
        out_shape=jax.ShapeDtypeStruct((m, d), F32),
        compiler_params=_cparams("parallel"),
        name="rmsnorm",
    )(x, g.reshape(1, d))


def _norm_matmul_kernel(x_ref, g_ref, b_ref, *refs, n_w, n_out, epilogue, scale):
    w_refs, o_refs, xn_scr = refs[:n_w], refs[n_w:-1], refs[-1]

    @pl.when(pl.program_id(1) == 0)
    def _():
        xn_scr[...] = _rms(x_ref[...], g_ref[...]).astype(BF16)

    for wi, w_ref in enumerate(w_refs):
        acc = _dot(xn_scr[...], w_ref[...])
        if epilogue == "log_sigmoid":
            acc = -_softplus(-(acc + b_ref[...]))
        elif epilogue == "scale":
            acc = acc * scale
        for o_ref in o_refs[wi * n_out:(wi + 1) * n_out]:
            o_ref[...] = acc.astype(o_ref.dtype)


def _norm_matmul(x, g, ws, out_dtypes, epilogue="none", bias=None, scale=1.0, layer=None):
    m, d = x.shape
    n = ws[0].shape[-1]
    tm, tn = min(WIDE_ROW_TILE, m), min(COL_TILE, n)
    if bias is None:
        bias = jnp.zeros((1, n), F32)
    n_res = len(ws) * len(out_dtypes)
    outs = pl.pallas_call(
        functools.partial(_norm_matmul_kernel, n_w=len(ws), n_out=len(out_dtypes), epilogue=epilogue,
                          scale=scale),
        grid=(m // tm, n // tn),
        in_specs=[pl.BlockSpec((tm, d), lambda i, j: (i, 0)),
                  pl.BlockSpec((1, d), lambda i, j: (0, 0)),
                  pl.BlockSpec((1, tn), lambda i, j: (0, j))]
                 + [_layer_spec(w, layer, (d, tn), lambda i, j: (0, j)) for w in ws],
        out_specs=[pl.BlockSpec((tm, tn), lambda i, j: (i, j)) for _ in range(n_res)],
        out_shape=[jax.ShapeDtypeStruct((m, n), dt) for _ in ws for dt in out_dtypes],
        scratch_shapes=[pltpu.VMEM((tm, d), BF16)],
        compiler_params=_cparams("parallel", "arbitrary"),
        name="norm_matmul_" + epilogue,
    )(x, g.reshape(1, d), bias, *ws)
    return outs


def _proj_ffn_kernel(h_ref, x_ref, wo_ref, g_ref, w1_ref, w3_ref, w2_ref, gy_ref, o_ref, xn_scr, acc_scr, *,
                     final_norm):
    j = pl.program_id(1)

    @pl.when(j == 0)
    def _():
        h1 = h_ref[...] + _dot(x_ref[...], wo_ref[...])
        xn_scr[...] = _rms(h1, g_ref[...]).astype(BF16)
        acc_scr[...] = h1

    x = xn_scr[...]
    a = _dot(x, w1_ref[...])
    b = _dot(x, w3_ref[...])
    hm = (a * jax.nn.sigmoid(a) * b).astype(BF16)
    acc_scr[...] += _dot(hm, w2_ref[...])

    @pl.when(j == pl.num_programs(1) - 1)
    def _():
        o_ref[...] = _rms(acc_scr[...], gy_ref[...]) if final_norm else acc_scr[...]


def _proj_ffn(h, x, w_o, layer_o, g, w1, w3, w2, layer, final_gain=None):
    m, d = h.shape
    k = x.shape[1]
    f = w1.shape[-1]
    tm, tf = _row_tile(m), COL_TILE
    final_norm = final_gain is not None
    return pl.pallas_call(
        functools.partial(_proj_ffn_kernel, final_norm=final_norm),
        grid=(m // tm, f // tf),
        in_specs=[pl.BlockSpec((tm, d), lambda i, j: (i, 0)),
                  pl.BlockSpec((tm, k), lambda i, j: (i, 0)),
                  _layer_spec(w_o, layer_o, (k, d), lambda i, j: (0, 0), pl.Buffered(1)),
                  pl.BlockSpec((1, d), lambda i, j: (0, 0)),
                  _layer_spec(w1, layer, (d, tf), lambda i, j: (0, j)),
                  _layer_spec(w3, layer, (d, tf), lambda i, j: (0, j)),
                  _layer_spec(w2, layer, (tf, d), lambda i, j: (j, 0)),
                  pl.BlockSpec((1, d), lambda i, j: (0, 0))],
        out_specs=pl.BlockSpec((tm, d), lambda i, j: (i, 0)),
        out_shape=jax.ShapeDtypeStruct((m, d), F32),
        scratch_shapes=[pltpu.VMEM((tm, d), BF16), pltpu.VMEM((tm, d), F32)],
        compiler_params=_cparams("parallel", "arbitrary"),
        name="proj_ffn",
    )(h, x, w_o, g.reshape(1, d), w1, w3, w2, (final_gain if final_norm else g).reshape(1, d))


def _mix_kernel(h_ref, hp_ref, st_ref, ln_ref, vf_ref, mu_ref, wr_ref, wk_ref, wv_ref,
                w1_ref, w2_ref, a1_ref, a2_ref, v1_ref, v2_ref, g1_ref, g2_ref,
                w0_ref, a0_ref, v0_ref,
                r_ref, lw_ref, k_ref, v_ref, a_ref, g_ref,
                xr_scr, xk_scr, xv_scr, hw_scr, ha_scr, hv_scr, hg_scr, *, has_v, seq_len):
    @pl.when(pl.program_id(1) == 0)
    def _():
        tm = h_ref.shape[0]
        xn = _rms(h_ref[...], ln_ref[...])
        row = lax.broadcasted_iota(jnp.int32, (tm, 1), 0)
        before_tile = _rms(hp_ref[7:8, :], ln_ref[...])
        xprev = jnp.where(row == 0, before_tile, pltpu.roll(xn, 1, axis=0))
        if seq_len >= tm:
            starts_here = pl.program_id(0) % (seq_len // tm) == 0
            xprev = jnp.where((row == 0) & starts_here, st_ref[0, 0:1, :], xprev)
        else:
            for s in range(tm // seq_len):
                xprev = jnp.where(row == s * seq_len, st_ref[0, s:s + 1, :], xprev)
        xx = xprev - xn
        mix = lambda i: (xn + xx * mu_ref[i:i + 1, :]).astype(BF16)
        xr_scr[...] = mix(0)
        hw_scr[...] = jnp.tanh(_dot(mix(1), w1_ref[...])).astype(BF16)
        xk_scr[...] = mix(2)
        xv = mix(3)
        xv_scr[...] = xv
        if has_v:
            hv_scr[...] = _dot(xv, v1_ref[...]).astype(BF16)
        ha_scr[...] = _dot(mix(4), a1_ref[...]).astype(BF16)
        hg_scr[...] = jax.nn.sigmoid(_dot(mix(5), g1_ref[...])).astype(BF16)

    r_ref[...] = _dot(xr_scr[...], wr_ref[...]).astype(r_ref.dtype)
    k_ref[...] = _dot(xk_scr[...], wk_ref[...]).astype(k_ref.dtype)
    v = _dot(xv_scr[...], wv_ref[...])
    if has_v:
        v = v + (vf_ref[...] - v) * jax.nn.sigmoid(v0_ref[...] + _dot(hv_scr[...], v2_ref[...]))
    v_ref[...] = v.astype(v_ref.dtype)
    w_log = -_softplus(-(w0_ref[...] + _dot(hw_scr[...], w2_ref[...]))) - 0.5
    lw_ref[...] = -jnp.exp(w_log)
    a_ref[...] = jax.nn.sigmoid(a0_ref[...] + _dot(ha_scr[...], a2_ref[...])).astype(a_ref.dtype)
    g_ref[...] = _dot(hg_scr[...], g2_ref[...]).astype(g_ref.dtype)


def _mix(h, shift_state, seq_len, ln, vfirst, mu, wr, wk, wv, w1, w2, a1, a2, v1, v2, g1, g2, w0, a0, v0,
         has_v, layer, layer_v):
    m, d = h.shape
    tm, tn = _row_tile(m), COL_TILE
    assert seq_len % tm == 0 or tm % seq_len == 0
    n_starts = max(1, tm // seq_len)
    seq_of = [[(i * tm + s * seq_len) // seq_len for s in range(n_starts)] for i in range(m // tm)]
    starts = shift_state[jnp.asarray(seq_of)]
    lw_, la_, lv_, lg_ = w1.shape[-1], a1.shape[-1], v1.shape[-1], g1.shape[-1]
    row = lambda i, j: (i, 0)
    col = lambda i, j: (0, j)
    tile = lambda i, j: (i, j)
    fixed = lambda i, j: (0, 0)
    down = lambda w, lyr: _layer_spec(w, lyr, (d, w.shape[-1]), fixed)
    up = lambda w, lyr: _layer_spec(w, lyr, (w.shape[-2], tn), col)
    outs = pl.pallas_call(
        functools.partial(_mix_kernel, has_v=has_v, seq_len=seq_len),
        grid=(m // tm, d // tn),
        in_specs=[pl.BlockSpec((tm, d), row),
                  pl.BlockSpec((8, d), lambda i, j: (jnp.maximum(i * (tm // 8) - 1, 0), 0)),
                  pl.BlockSpec((1, n_starts, d), lambda i, j: (i, 0, 0)),
                  pl.BlockSpec((1, d), fixed),
                  pl.BlockSpec((tm, tn), tile),
                  pl.BlockSpec((8, d), fixed),
                  up(wr, layer), up(wk, layer), up(wv, layer),
                  down(w1, layer), up(w2, layer), down(a1, layer), up(a2, layer),
                  down(v1, layer_v), up(v2, layer_v), down(g1, layer), up(g2, layer),
                  pl.BlockSpec((1, tn), col), pl.BlockSpec((1, tn), col), pl.BlockSpec((1, tn), col)],
        out_specs=[pl.BlockSpec((tm, tn), tile) for _ in range(6)],
        out_shape=[jax.ShapeDtypeStruct((m, d), dt) for dt in (BF16, F32, BF16, BF16, BF16, BF16)],
        scratch_shapes=[pltpu.VMEM((tm, d), BF16), pltpu.VMEM((tm, d), BF16), pltpu.VMEM((tm, d), BF16),
                        pltpu.VMEM((tm, lw_), BF16), pltpu.VMEM((tm, la_), BF16),
                        pltpu.VMEM((tm, lv_), BF16), pltpu.VMEM((tm, lg_), BF16)],
        compiler_params=_cparams("parallel", "arbitrary"),
        name="rwkv_mix_proj",
    )(h, h, starts, ln.reshape(1, d), vfirst, mu, wr, wk, wv, w1, w2, a1, a2, v1, v2, g1, g2, w0, a0, v0)
    return outs


SCAN_CHUNK = 64
SCAN_PAIRS = 16


def _scan_kernel(r_ref, lw_ref, kr_ref, v_ref, a_ref, g_ref,
                 kk_ref, ka_ref, rk_ref, lnw_ref, lnb_ref, s0_ref,
                 y_ref, sout_ref, st_scr):
    c, n_chunks = pl.program_id(2), pl.num_programs(2)
    C = SCAN_CHUNK
    R = 2 * C
    n_pairs = st_scr.shape[0]
    f32 = lambda ref, sl: ref[0, :, sl].astype(F32)

    @pl.when(c == 0)
    def _():
        st_scr[...] = s0_ref[0]

    lane_c = lax.broadcasted_iota(jnp.int32, (C, LANES), 1)
    head0 = lane_c < HEAD_A
    row = lax.broadcasted_iota(jnp.int32, (R, R), 0)
    col = lax.broadcasted_iota(jnp.int32, (R, R), 1)
    same = (row // C) == (col // C)
    strict = same & ((col % C) < (row % C))
    incl = same & ((col % C) <= (row % C))
    row2 = lax.broadcasted_iota(jnp.int32, (2 * LANES, LANES), 0)
    col2 = lax.broadcasted_iota(jnp.int32, (2 * LANES, LANES), 1)
    ones2 = (((row2 % LANES) // HEAD_A) == (col2 // HEAD_A)).astype(BF16)
    head_sum = lambda x: _dot_l_stacked(x, ones2, 2)
    ti = lax.broadcasted_iota(jnp.int32, (C, 3 * C), 0)
    tj = lax.broadcasted_iota(jnp.int32, (C, 3 * C), 1)
    tri3 = ((tj % C) <= ti).astype(BF16)
    zeros_c = jnp.zeros((C, LANES), F32)

    def stack(x):
        return jnp.concatenate([jnp.where(head0, x, 0.0), jnp.where(head0, 0.0, x)], axis=0)

    sls = [slice(p * LANES, (p + 1) * LANES) for p in range(n_pairs)]
    each = lambda f, *cols: [f(*xs) for xs in zip(*cols)]
    bf = lambda xs: [x.astype(BF16) for x in xs]

    r = [f32(r_ref, sl) for sl in sls]
    lw = [lw_ref[0, :, sl] for sl in sls]
    kr = [f32(kr_ref, sl) for sl in sls]
    v = [f32(v_ref, sl) for sl in sls]
    a = [f32(a_ref, sl) for sl in sls]

    kkr = each(lambda x, sl: x * kk_ref[:, sl], kr, sls)
    ss = each(lambda x: head_sum(x * x), kkr)
    kk = each(lambda x, s: x / jnp.maximum(jnp.sqrt(s), 1e-12), kkr, ss)
    k = each(lambda x, y, sl: x * (1.0 + (y - 1.0) * ka_ref[:, sl]), kr, a, sls)
    b = each(lambda x, y: x * y, kk, a)

    cum = each(lambda x: _dot(tri3, jnp.concatenate(_split(x, 3), axis=0)), lw)
    g_in = each(jnp.exp, cum)
    g_ex = each(lambda x, y: jnp.exp(x - y), cum, lw)
    g_inv = each(lambda x: jnp.exp(-x), cum)
    g_tail = each(lambda x: jnp.exp(x[C - 1:C, :] - x), cum)

    at = each(lambda x, y: stack(-x * y), kk, g_ex)
    at_b = bf(at)
    rt = bf(each(lambda x, y: stack(x * y), r, g_in))
    bt = bf(each(lambda x, y: stack(x * y), b, g_inv))
    kt = bf(each(lambda x, y: stack(x * y), k, g_inv))
    bh = each(lambda x, y: stack(x * y), b, g_tail)
    kh = each(lambda x, y: stack(x * y), k, g_tail)
    vs = bf(each(stack, v))

    sc = each(lambda w, x, y, z: _dot_nt(jnp.concatenate([w, x], axis=0), jnp.concatenate([y, z], axis=0)),
              at_b, rt, bt, kt)
    a_ab = each(lambda x: jnp.where(strict, x[:R, :R], 0.0), sc)
    a_ak = bf(each(lambda x: jnp.where(strict, x[:R, R:], 0.0), sc))
    a_rb = bf(each(lambda x: jnp.where(incl, x[R:, :R], 0.0), sc))
    a_rk = bf(each(lambda x: jnp.where(incl, x[R:, R:], 0.0), sc))

    st = [st_scr[p] for p in range(n_pairs)]
    st_b = bf(st)
    u = each(lambda w, y, s, z: _dot(jnp.concatenate([w, y], axis=1), jnp.concatenate([s, z], axis=0)),
             at_b, a_ak, st_b, vs)
    pw = a_ab
    n_sq = int(math.log2(C))
    for it in range(n_sq):
        pw_b = bf(pw)
        u = each(lambda y, z: y + _dot(z, y.astype(BF16)), u, pw_b)
        if it + 1 < n_sq:
            pw = each(lambda z: _dot(z, z), pw_b)
    u_b = bf(u)
    uv = each(lambda y, t: jnp.concatenate([y, t], axis=0), u_b, vs)
    o_st = each(lambda q, w, z, s, y: _dot(jnp.concatenate([q, w, z], axis=1), jnp.concatenate([s, y], axis=0)),
                rt, a_rb, a_rk, st_b, uv)
    o = each(lambda y: y[:C] + y[C:], o_st)

    def decay_col(x):
        x_t = jnp.concatenate([x, zeros_c], axis=0).T if C < LANES else x.T
        return jnp.exp(jnp.sum(x_t, axis=1, keepdims=True))

    dcol = each(decay_col, lw)
    st_new = each(lambda dc, s, w, z, y: dc * s + _dot(jnp.concatenate([w.T, z.T], axis=1).astype(BF16), y),
                  dcol, st, bh, kh, uv)
    for p in range(n_pairs):
        st_scr[p] = st_new[p]

    mean = each(lambda y: head_sum(y) * (1.0 / HEAD_A), o)
    dlt = each(lambda y, z: y - z, o, mean)
    var = each(lambda y: head_sum(y * y) * (1.0 / HEAD_A), dlt)
    on = each(lambda y, z, sl: y * lax.rsqrt(z + LNX_EPS) * lnw_ref[:, sl] + lnb_ref[:, sl], dlt, var, sls)
    bonus = each(lambda x, y, z, sl: head_sum(x * y * rk_ref[:, sl]) * z, r, k, v, sls)
    for p, sl in enumerate(sls):
        y_ref[0, :, sl] = ((on[p] + bonus[p]) * f32(g_ref, sl)).astype(y_ref.dtype)

    @pl.when(c == n_chunks - 1)
    def _():
        sout_ref[0] = st_scr[...]


def _scan(r, lw, kr, v, a, g, k_k, k_a, r_k, lnx_w, lnx_b, s0_bd):
    bsz, t, d = r.shape
    C = SCAN_CHUNK
    P = min(SCAN_PAIRS, d // LANES)
    w = P * LANES
    n_pg = d // w
    seq = pl.BlockSpec((1, C, w), lambda b, q, c: (b, c, q))
    vec = pl.BlockSpec((1, w), lambda b, q, c: (0, q))
    sbd = pl.BlockSpec((1, P, LANES, LANES), lambda b, q, c: (b, q, 0, 0))
    y, s_out = pl.pallas_call(
        _scan_kernel,
        grid=(bsz, n_pg, t // C),
        in_specs=[seq] * 6 + [vec] * 5 + [sbd],
        out_specs=[seq, sbd],
        out_shape=[jax.ShapeDtypeStruct((bsz, t, d), BF16),
                   jax.ShapeDtypeStruct(s0_bd.shape, F32)],
        scratch_shapes=[pltpu.VMEM((P, LANES, LANES), F32)],
        compiler_params=_cparams("parallel", "parallel", "arbitrary"),
        name="wkv7_scan",
    )(r, lw, kr, v, a, g, k_k.reshape(1, d), k_a.reshape(1, d), r_k.reshape(1, d),
      lnx_w.reshape(1, d), lnx_b.reshape(1, d), s0_bd)
    return y, s_out


def _state_to_bd(s):
    bsz, h = s.shape[:2]
    st = jnp.swapaxes(s, 2, 3).reshape(bsz, h // 2, 2, HEAD_A, HEAD_A)
    eye = jnp.eye(2, dtype=s.dtype)
    bd = st[:, :, :, :, None, :] * eye[None, None, :, None, :, None]
    return bd.reshape(bsz, h // 2, 2 * HEAD_A, 2 * HEAD_A)


def _bd_to_state(bd):
    bsz, hp = bd.shape[:2]
    x = bd.reshape(bsz, hp, 2, HEAD_A, 2, HEAD_A)
    st = jnp.stack([x[:, :, 0, :, 0, :], x[:, :, 1, :, 1, :]], axis=2)
    return jnp.swapaxes(st.reshape(bsz, hp * 2, HEAD_A, HEAD_A), 2, 3)


N_PIECES = 3


def _forget_bias_kernel(x_ref, eq_ref, ek_ref, qa_ref, ka_ref, carry_scr):
    @pl.when(pl.program_id(1) == 0)
    def _():
        carry_scr[...] = jnp.zeros_like(carry_scr)

    tt = x_ref.shape[1]
    ti = lax.broadcasted_iota(jnp.int32, (tt, N_PIECES * tt), 0)
    tj = lax.broadcasted_iota(jnp.int32, (tt, N_PIECES * tt), 1)
    tri = ((tj % tt) <= ti).astype(BF16)
    c = _dot(tri, jnp.concatenate(_split(x_ref[0], N_PIECES), axis=0)) + carry_scr[...]
    carry_scr[...] = c[tt - 1:tt, :]
    pieces = jnp.concatenate(_split(c * LOG2E, N_PIECES), axis=1)
    lane = lax.broadcasted_iota(jnp.int32, qa_ref.shape[1:], 1) % HEAD_B
    qa = _dot(pieces, eq_ref[...]) + jnp.where((lane >= N_PIECES) & (lane < 2 * N_PIECES), 1.0, 0.0)
    ka = _dot(pieces, ek_ref[...]) + jnp.where(lane < N_PIECES, 1.0, 0.0)
    qa_ref[0] = qa.astype(BF16)
    ka_ref[0] = ka.astype(BF16)


def _forget_bias_lanes(logf_lanes, n_heads):
    bsz, t, _ = logf_lanes.shape
    tt = _row_tile(t)
    d = n_heads * HEAD_B
    src = jnp.arange(N_PIECES * LANES)
    piece, head = src // LANES, src % LANES
    place = lambda lane0: (jnp.arange(d)[None, :] == (head * HEAD_B + lane0 + piece)[:, None]) & (
        head < n_heads)[:, None]
    eq = place(0).astype(BF16)
    ek = -place(N_PIECES).astype(BF16)
    out = pl.BlockSpec((1, tt, d), lambda b, i: (b, i, 0))
    return pl.pallas_call(
        _forget_bias_kernel,
        grid=(bsz, t // tt),
        in_specs=[pl.BlockSpec((1, tt, LANES), lambda b, i: (b, i, 0)),
                  pl.BlockSpec(eq.shape, lambda b, i: (0, 0)), pl.BlockSpec(ek.shape, lambda b, i: (0, 0))],
        out_specs=[out, out],
        out_shape=[jax.ShapeDtypeStruct((bsz, t, d), BF16)] * 2,
        scratch_shapes=[pltpu.VMEM((1, LANES), F32)],
        compiler_params=_cparams("parallel", "arbitrary"),
        name="forget_bias_lanes",
    )(logf_lanes, eq, ek)


def _flash_kernel(q_ref, qa_ref, k_ref, ka_ref, v_ref, o_ref, m_scr, acc_scr):
    i, j = pl.program_id(1), pl.program_id(2)
    tq, tk = q_ref.shape[1], k_ref.shape[1]
    n_heads = q_ref.shape[2] // HEAD_B

    @pl.when(j == 0)
    def _():
        m_scr[...] = jnp.full_like(m_scr, NEG_BIG)
        acc_scr[...] = jnp.zeros_like(acc_scr)

    def block(rows, n_keys, masked):
        n_rows = rows.stop - rows.start
        keys = slice(0, n_keys)
        if masked:
            qpos = lax.broadcasted_iota(jnp.int32, (n_rows, n_keys), 0) + rows.start
            kpos = lax.broadcasted_iota(jnp.int32, (n_rows, n_keys), 1)
            keep = kpos <= qpos
        ones = jnp.ones((n_keys, HEAD_B), BF16)
        for h in range(n_heads):
            hs = slice(h * HEAD_B, (h + 1) * HEAD_B)
            q_aug = jnp.concatenate([q_ref[0, rows, hs], qa_ref[0, rows, hs]], axis=1)
            k_aug = jnp.concatenate([k_ref[0, keys, hs], ka_ref[0, keys, hs]], axis=1)
            s = _dot_nt(q_aug, k_aug)
            if masked:
                s = jnp.where(keep, s, NEG_BIG)
            m_prev = m_scr[h, rows]
            m_next = jnp.maximum(m_prev, jnp.max(s, axis=1, keepdims=True))
            alpha = jnp.exp2(m_prev - m_next)
            p = jnp.exp2(s - jnp.concatenate([m_next] * (n_keys // LANES), axis=1))
            v_aug = jnp.concatenate([v_ref[0, keys, hs], ones], axis=1)
            acc_scr[h, rows] = (jnp.concatenate([alpha, alpha], axis=1) * acc_scr[h, rows]
                                + _dot(p.astype(BF16), v_aug))
            m_scr[h, rows] = m_next

    @pl.when(j < i)
    def _():
        block(slice(0, tq), tk, False)

    @pl.when(j == i)
    def _():
        half = tq // 2
        if half % LANES == 0:
            block(slice(0, half), half, True)
            block(slice(half, tq), tk, True)
        else:
            block(slice(0, tq), tk, True)
        for h in range(n_heads):
            hs = slice(h * HEAD_B, (h + 1) * HEAD_B)
            acc = acc_scr[h]
            o_ref[0, :, hs] = (acc[:, :HEAD_B] / acc[:, HEAD_B:]).astype(o_ref.dtype)


def _flash_attention(q, q_aug, k, k_aug, v):
    bsz, t, d = q.shape
    h = d // HEAD_B
    tq = _row_tile(t)
    nq = t // tq
    q_idx = lambda b, i, j: (b, i, 0)
    kv_idx = lambda b, i, j: (b, jnp.minimum(j, i), 0)
    return pl.pallas_call(
        _flash_kernel,
        grid=(bsz, nq, nq),
        in_specs=[pl.BlockSpec((1, tq, d), q_idx), pl.BlockSpec((1, tq, d), q_idx),
                  pl.BlockSpec((1, tq, d), kv_idx), pl.BlockSpec((1, tq, d), kv_idx),
                  pl.BlockSpec((1, tq, d), kv_idx)],
        out_specs=pl.BlockSpec((1, tq, d), q_idx),
        out_shape=jax.ShapeDtypeStruct((bsz, t, d), BF16),
        scratch_shapes=[pltpu.VMEM((h, tq, LANES), F32), pltpu.VMEM((h, tq, 2 * HEAD_B), F32)],
        compiler_params=_cparams("parallel", "parallel", "arbitrary"),
        name="fox_prompt_attention",
    )(q, q_aug, k, k_aug, v)


PAGES_PER_STEP = 8
SLAB = 8


def _page_suffix_kernel(lf_ref, suf_ref, tot_ref):
    pb, h, page = lf_ref.shape
    u = lax.broadcasted_iota(jnp.int32, (page, page), 0)
    s = lax.broadcasted_iota(jnp.int32, (page, page), 1)
    lf = lf_ref[...].reshape(pb * h, page) * LOG2E
    suf_ref[...] = _dot_l(lf, (u > s).astype(BF16), 3).reshape(pb, h, page)
    tot_ref[...] = _dot_l(lf, jnp.ones((page, page), BF16), 3).reshape(pb, h, page)


def _page_suffix(lf_t):
    n_phys, h, page = lf_t.shape
    pb = math.gcd(n_phys, 64)
    spec = pl.BlockSpec((pb, h, page), lambda i: (i, 0, 0))
    return pl.pallas_call(
        _page_suffix_kernel,
        grid=(n_phys // pb,),
        in_specs=[spec],
        out_specs=[spec, spec],
        out_shape=[jax.ShapeDtypeStruct(lf_t.shape, F32)] * 2,
        compiler_params=_cparams("parallel"),
        name="page_logf_suffix",
    )(lf_t)


def _paged_kernel(pt_ref, q_ref, kn_ref, vn_ref, gn_ref, *refs, n_group, n_slab):
    del pt_ref
    p, n_steps = pl.program_id(1), pl.num_programs(1)
    kp_refs, vp_refs, st_refs = refs[:n_group], refs[n_group:2 * n_group], refs[2 * n_group:3 * n_group]
    o_ref, m_scr, l_scr, acc_scr, cq_scr, carry_scr = refs[3 * n_group:]
    rows = q_ref.shape[1]
    n_heads = kn_ref.shape[1]
    nq = rows // n_heads
    page = kn_ref.shape[2]
    srows = SLAB * nq
    flat = page * SLAB

    def update(sl, s, pv_of):
        m_old = m_scr[sl]
        m_new = jnp.maximum(m_old, jnp.max(s, axis=1, keepdims=True))
        alpha = jnp.exp2(m_old - m_new)
        pr = jnp.exp2(s - m_new)
        l_scr[sl] = alpha * l_scr[sl] + jnp.sum(pr, axis=1, keepdims=True)
        acc_scr[sl] = alpha * acc_scr[sl] + pv_of(pr)
        m_scr[sl] = m_new

    @pl.when(p == 0)
    def _():
        m_scr[...] = jnp.full_like(m_scr, NEG_BIG)
        l_scr[...] = jnp.zeros_like(l_scr)
        acc_scr[...] = jnp.zeros_like(acc_scr)
        carry_scr[...] = jnp.zeros_like(carry_scr)
        row = lax.broadcasted_iota(jnp.int32, (rows, page), 0)
        lane = lax.broadcasted_iota(jnp.int32, (rows, page), 1)
        u = lax.broadcasted_iota(jnp.int32, (page, page), 0)
        s_ = lax.broadcasted_iota(jnp.int32, (page, page), 1)
        gn = gn_ref[0] * LOG2E
        qi = row % nq
        c_new_col = jnp.sum(jnp.where(lane <= qi, gn, 0.0), axis=1, keepdims=True)
        hi, mid, lo = [x.astype(F32) for x in _split(c_new_col, 3)]
        lane_a = lax.broadcasted_iota(jnp.int32, (rows, LANES), 1)
        cq_scr[...] = jnp.where(lane_a == 0, hi, jnp.where(lane_a == 1, mid,
                                                           jnp.where(lane_a == 2, lo, 0.0))).astype(BF16)
        c_new_row = _dot_l(gn, (u <= s_).astype(BF16), 3)
        s = jnp.concatenate([_dot_nt(q_ref[0, h * nq:(h + 1) * nq, :], kn_ref[0, h])
                             for h in range(n_heads)], axis=0) + (c_new_col - c_new_row)
        s = jnp.where(lane <= qi, s, NEG_BIG)
        for c in range(n_slab):
            sl = slice(c * srows, (c + 1) * srows)
            update(sl, s[sl], lambda pr, c=c: jnp.concatenate(
                [_dot(pr[hh * nq:(hh + 1) * nq, :].astype(BF16), vn_ref[0, c * SLAB + hh])
                 for hh in range(SLAB)], axis=0))

    row_f = lax.broadcasted_iota(jnp.int32, (srows, flat), 0)
    lane_f = lax.broadcasted_iota(jnp.int32, (srows, flat), 1)
    same_head = (row_f // nq) == (lane_f % SLAB)
    lane_k = lax.broadcasted_iota(jnp.int32, (flat, LANES), 1)
    ones_k = (lane_k < 3).astype(BF16)
    slabs = range(n_slab)
    sls = [slice(c * srows, (c + 1) * srows) for c in slabs]
    q_aug = [jnp.concatenate([q_ref[0, sl, :], cq_scr[sl, :]], axis=1) for sl in sls]
    carry = [carry_scr[c:c + 1, :] for c in slabs]
    s = [[] for _ in slabs]
    for g in range(n_group):
        for c in slabs:
            k_rows = kp_refs[g][:, c].reshape(flat, HEAD_B).astype(BF16)
            z = _dot_nt(q_aug[c], jnp.concatenate([k_rows, ones_k], axis=1))
            s[c].append(jnp.where(same_head, z + (st_refs[g][c, 0:1, :] + carry[c]), NEG_BIG))
            carry[c] = carry[c] + st_refs[g][c, 1:2, :]
    s = [jnp.concatenate(x, axis=1) for x in s]
    m_old = [m_scr[sl] for sl in sls]
    m_new = [jnp.maximum(mo, jnp.max(x, axis=1, keepdims=True)) for mo, x in zip(m_old, s)]
    alpha = [jnp.exp2(mo - mn) for mo, mn in zip(m_old, m_new)]
    pr = [jnp.exp2(x - mn) for x, mn in zip(s, m_new)]
    v_rows = [jnp.concatenate([vp_refs[g][:, c].reshape(flat, HEAD_B)
                               for g in range(n_group)], axis=0).astype(BF16) for c in slabs]
    pv = [_dot(x.astype(BF16), vr) for x, vr in zip(pr, v_rows)]
    for c, sl in enumerate(sls):
        carry_scr[c:c + 1, :] = carry[c]
        l_scr[sl] = alpha[c] * l_scr[sl] + jnp.sum(pr[c], axis=1, keepdims=True)
        acc_scr[sl] = alpha[c] * acc_scr[sl] + pv[c]
        m_scr[sl] = m_new[c]

    @pl.when(p == n_steps - 1)
    def _():
        o_ref[0] = acc_scr[...] / l_scr[...]


def _paged_attention(page_table, q, k_new, v_new, g_new, cache_k, cache_v, bias_rows):
    dbs, rows, _ = q.shape
    n_heads = k_new.shape[1]
    n_slab = n_heads // SLAB
    n_pages = page_table.shape[1]
    page = k_new.shape[2]
    flat = page * SLAB
    n_group = math.gcd(PAGES_PER_STEP, n_pages)
    per_b = lambda b, p, pt: (b, 0, 0)
    per_b4 = lambda b, p, pt: (b, 0, 0, 0)
    slot_page = lambda b, p, pt, g: pt[b, n_pages - 1 - (p * n_group + g)]
    kv_specs = [pl.BlockSpec((None, page, n_slab, SLAB, HEAD_B),
                             lambda b, p, pt, g=g: (slot_page(b, p, pt, g), 0, 0, 0, 0)) for g in range(n_group)]
    row_specs = [pl.BlockSpec((None, n_slab, 2, flat),
                              lambda b, p, pt, g=g: (slot_page(b, p, pt, g), 0, 0, 0)) for g in range(n_group)]
    grid_spec = pltpu.PrefetchScalarGridSpec(
        num_scalar_prefetch=1,
        grid=(dbs, n_pages // n_group),
        in_specs=[pl.BlockSpec((1, rows, HEAD_B), per_b),
                  pl.BlockSpec((1, n_heads, page, HEAD_B), per_b4),
                  pl.BlockSpec((1, n_heads, page, HEAD_B), per_b4),
                  pl.BlockSpec((1, rows, page), per_b)] + kv_specs + kv_specs + row_specs,
        out_specs=pl.BlockSpec((1, rows, HEAD_B), per_b),
        scratch_shapes=[pltpu.VMEM((rows, 1), F32), pltpu.VMEM((rows, 1), F32),
                        pltpu.VMEM((rows, HEAD_B), F32), pltpu.VMEM((rows, LANES), BF16),
                        pltpu.VMEM((8, flat), F32)])
    return pl.pallas_call(
        functools.partial(_paged_kernel, n_group=n_group, n_slab=n_slab),
        grid_spec=grid_spec,
        out_shape=jax.ShapeDtypeStruct((dbs, rows, HEAD_B), F32),
        compiler_params=_cparams("parallel", "arbitrary"),
        name="fox_paged_attention",
    )(page_table, q, k_new, v_new, g_new, *([cache_k] * n_group), *([cache_v] * n_group),
      *([bias_rows] * n_group))


def _trunk(x, shift0, wkv0, past, p):
    bsz, t, d = x.shape
    m = bsz * t
    n_a = p["tm_w_r"].shape[0]
    depth = p["ln1"].shape[0]
    h_b = d // HEAD_B
    h = x.reshape(m, d)
    v_first = None
    shifts, states = [], []
    t_scan = -(-t // SCAN_CHUNK) * SCAN_CHUNK
    heads_first = lambda z: jnp.swapaxes(z.reshape(bsz, t, h_b, HEAD_B), 1, 2)
    for l in range(depth):
        if l == n_a:
            k_new, k_bf, v_new, v_bf = _norm_matmul(h, p["kv_norm"], [p["fox_w_k"], p["fox_w_v"]], (F32, BF16))
            (lf_pad,) = _norm_matmul(h, p["kv_norm"], [p["fox_w_f"]], (F32,), epilogue="log_sigmoid",
                                     bias=p["fox_b_f"])
            logf_new = lf_pad[:, :h_b].reshape(bsz, t, h_b)
            if past is None:
                q_aug, k_aug = _forget_bias_lanes(lf_pad.reshape(bsz, t, LANES), h_b)
            else:
                page = past[0].shape[1]
                pad_keys = lambda z: jnp.pad(heads_first(z), ((0, 0), (0, 0), (0, page - t), (0, 0)))
                k_pad, v_pad = pad_keys(k_bf), pad_keys(v_bf)
                g_new = jnp.repeat(jnp.swapaxes(logf_new, 1, 2), t, axis=1)
                g_new = jnp.pad(g_new, ((0, 0), (0, 0), (0, page - t)))
        if l < n_a:
            shifts.append(_rmsnorm(h.reshape(bsz, t, d)[:, -1], p["ln1"][l]))
            has_v = l > 0
            lv = max(l - 1, 0)
            r, lw, kr, v, a, g = _mix(
                h, shift0[l], t, p["ln1"][l], v_first if has_v else h, p["mu8"][l],
                p["tm_w_r"], p["tm_w_k"], p["tm_w_v"],
                p["tm_w1"], p["tm_w2"], p["tm_a1"], p["tm_a2"],
                p["tm_v1"], p["tm_v2"], p["tm_g1"], p["tm_g2"],
                p["tm_w0"][l].reshape(1, d), p["tm_a0"][l].reshape(1, d), p["tm_v0"][lv].reshape(1, d),
                has_v, l, lv)
            if l == 0:
                v_first = v
            seq = lambda z: jnp.pad(z.reshape(bsz, t, d), ((0, 0), (0, t_scan - t), (0, 0)))
            y, s_bd = _scan(seq(r), seq(lw), seq(kr), seq(v), seq(a), seq(g),
                            p["tm_k_k"][l], p["tm_k_a"][l], p["tm_r_k"][l].reshape(d),
                            p["tm_lnx_w"][l], p["tm_lnx_b"][l], _state_to_bd(wkv0[l]))
            states.append(_bd_to_state(s_bd))
            mixed, w_o, layer_o = y[:, :t].reshape(m, d), p["tm_w_o"], l
        else:
            j = l - n_a
            (q,) = _norm_matmul(h, p["ln1"][l], [p["fox_w_q"]], (BF16,), epilogue="scale",
                                scale=HEAD_B ** -0.5 * LOG2E, layer=j)
            if past is None:
                att = _flash_attention(q.reshape(bsz, t, d), q_aug, k_bf.reshape(bsz, t, d), k_aug,
                                       v_bf.reshape(bsz, t, d)).reshape(m, d)
            else:
                cache_k, cache_v, bias_rows, page_table = past
                o = _paged_attention(page_table, heads_first(q).reshape(bsz, h_b * t, HEAD_B),
                                     k_pad, v_pad, g_new, cache_k, cache_v, bias_rows)
                att = jnp.swapaxes(o.reshape(bsz, h_b, t, HEAD_B), 1, 2).reshape(m, d).astype(BF16)
            mixed, w_o, layer_o = att, p["fox_w_o"], j
        h = _proj_ffn(h, mixed, w_o, layer_o, p["ln2"][l], p["ffn_w1"], p["ffn_w3"], p["ffn_w2"], l,
                      final_gain=p["ln_out"] if l == depth - 1 else None)
    h_sh = (bsz, t, h_b, HEAD_B)
    return (h.reshape(bsz, t, d), jnp.stack(shifts), jnp.stack(states),
            k_new.reshape(h_sh), v_new.reshape(h_sh), logf_new)


def kernel(x_prompt, x_sample, state_wkv, state_shift, cache_k, cache_v, cache_logf, page_table, ln1, ln2, ln_out, tm_mu, tm_w_r, tm_w_k, tm_w_v, tm_w_o, tm_w0, tm_w1, tm_w2, tm_a0, tm_a1, tm_a2, tm_v0, tm_v1, tm_v2, tm_g1, tm_g2, tm_k_k, tm_k_a, tm_r_k, tm_lnx_w, tm_lnx_b, kv_norm, fox_w_k, fox_w_v, fox_w_f, fox_b_f, fox_w_q, fox_w_o, ffn_w1, ffn_w3, ffn_w2):
    bf = lambda w: w.astype(BF16)
    n_a, d = tm_w0.shape
    h_b = d // HEAD_B
    lora = lambda w1, w2: (bf(jnp.pad(w1, ((0, 0), (0, 0), (0, -w1.shape[2] % LANES)))),
                           bf(jnp.pad(w2, ((0, 0), (0, -w2.shape[1] % LANES), (0, 0)))))
    w1, w2 = lora(tm_w1, tm_w2)
    a1, a2 = lora(tm_a1, tm_a2)
    if tm_v1.shape[0] == 0:
        tm_v0 = jnp.zeros((1, d), F32)
        tm_v1 = jnp.zeros((1, d, LANES), F32)
        tm_v2 = jnp.zeros((1, LANES, d), F32)
    v1, v2 = lora(tm_v1, tm_v2)
    g1, g2 = lora(tm_g1, tm_g2)
    p = dict(ln1=ln1, ln2=ln2, ln_out=ln_out,
             mu8=jnp.pad(tm_mu, ((0, 0), (0, 2), (0, 0))),
             tm_w_r=bf(tm_w_r), tm_w_k=bf(tm_w_k), tm_w_v=bf(tm_w_v), tm_w_o=bf(tm_w_o),
             tm_w0=tm_w0, tm_w1=w1, tm_w2=w2, tm_a0=tm_a0, tm_a1=a1, tm_a2=a2,
             tm_v0=tm_v0, tm_v1=v1, tm_v2=v2, tm_g1=g1, tm_g2=g2,
             tm_k_k=tm_k_k, tm_k_a=tm_k_a, tm_r_k=tm_r_k, tm_lnx_w=tm_lnx_w, tm_lnx_b=tm_lnx_b,
             kv_norm=kv_norm, fox_w_k=bf(fox_w_k), fox_w_v=bf(fox_w_v),
             fox_w_f=bf(jnp.pad(fox_w_f, ((0, 0), (0, LANES - h_b)))),
             fox_b_f=jnp.pad(fox_b_f, (0, LANES - h_b)).reshape(1, LANES),
             fox_w_q=bf(fox_w_q), fox_w_o=bf(fox_w_o),
             ffn_w1=bf(ffn_w1), ffn_w3=bf(ffn_w3), ffn_w2=bf(ffn_w2))

    bsz = x_prompt.shape[0]
    h_a = d // HEAD_A
    shift0 = jnp.zeros((n_a, bsz, d), x_prompt.dtype)
    wkv0 = jnp.zeros((n_a, bsz, h_a, HEAD_A, HEAD_A), F32)
    n_phys, page = cache_k.shape[:2]
    n_slab = h_b // SLAB
    slabs = lambda c: c.reshape(n_phys, page, n_slab, SLAB, HEAD_B)
    key_head_rows = lambda z: jnp.transpose(z.reshape(n_phys, n_slab, SLAB, page), (0, 1, 3, 2)).reshape(
        n_phys, n_slab, 1, page * SLAB)
    suffix, total = _page_suffix(jnp.swapaxes(cache_logf, 1, 2))
    bias_rows = jnp.concatenate([key_head_rows(suffix), key_head_rows(total)], axis=2)
    past = (slabs(cache_k), slabs(cache_v), bias_rows, page_table)
    y_p, sh_p, wkv_p, k_p, v_p, lf_p = _trunk(x_prompt, shift0, wkv0, None, p)
    y_s, sh_s, wkv_s, k_s, v_s, lf_s = _trunk(x_sample, state_shift, state_wkv, past, p)
    return (y_p, y_s, wkv_p, sh_p, k_p, v_p, lf_p, wkv_s, sh_s, k_s, v_s, lf_s)
```
